```python
import math
import jax, jax.numpy as jnp
from jax import lax
import numpy as np

D_MODEL = 2048
BATCH = 2
SEQ = 4096
DEPTH = 1
DEC_BATCH = 32
DEC_SEQ = 1
PAST_LEN = 16384
PAGE_SIZE = 128

GDN_HEADS = 8
GDN_DK = 128
GDN_DV = 128
GDN_WIDTH = GDN_HEADS * GDN_DV
CONV_W = 4
CONV_CH = 2 * GDN_HEADS * GDN_DK + GDN_WIDTH
GDN_CHUNK = 64
ATT_HEADS = 8
ATT_DH = 128
ATT_WIDTH = ATT_HEADS * ATT_DH
IDX_HEADS = 16
IDX_DH = 128
TOPK_MAX = 256
QUERY_BLOCK = 128
ROPE_THETA = 10000.0
PLE_DIM = 256
EPS = 1e-6

PROJ_SIZES = (GDN_HEADS * GDN_DK, GDN_HEADS * GDN_DK, GDN_WIDTH, GDN_WIDTH, GDN_HEADS, GDN_HEADS,
              ATT_WIDTH, ATT_WIDTH, ATT_WIDTH, ATT_WIDTH, IDX_HEADS * IDX_DH, IDX_DH, IDX_HEADS,
              D_MODEL, D_MODEL)
D_IN = sum(PROJ_SIZES)

kernel_name = 'gdn_dsa_hybrid_step'


def rmsnorm(x, gain):
    xf = x.astype(jnp.float32)
    y = xf * lax.rsqrt(jnp.mean(xf * xf, axis=-1, keepdims=True) + EPS) * gain.astype(jnp.float32)
    return y.astype(x.dtype)


def l2norm(x):
    xf = x.astype(jnp.float32)
    return xf * lax.rsqrt(jnp.sum(xf * xf, axis=-1, keepdims=True) + EPS)


def rope(x, pos):
    d = x.shape[-1]
    half = d // 2
    inv = ROPE_THETA ** (-jnp.arange(half, dtype=jnp.float32) * 2.0 / d)
    ang = pos.astype(jnp.float32)[:, None] * inv[None, :]
    cos = jnp.cos(ang)[:, None, :]
    sin = jnp.sin(ang)[:, None, :]
    x1 = x[..., :half].astype(jnp.float32)
    x2 = x[..., half:].astype(jnp.float32)
    return jnp.concatenate([x1 * cos - x2 * sin, x2 * cos + x1 * sin], axis=-1).astype(x.dtype)


def split_projection(z):
    out, start = [], 0
    for n in PROJ_SIZES:
        out.append(z[..., start:start + n])
        start += n
    return out


def gated_delta_chunked(q, k, v, beta, g, s0):
    B, T, H, DK = q.shape
    DV = v.shape[-1]
    C = GDN_CHUNK
    n = -(-T // C)
    pad = n * C - T

    def prep(a):
        a = a.astype(jnp.float32)
        a = jnp.pad(a, [(0, 0), (0, pad)] + [(0, 0)] * (a.ndim - 2))
        a = a.reshape((B, n, C) + a.shape[2:])
        a = jnp.moveaxis(a, 3, 2)
        return jnp.moveaxis(a, 1, 0)

    qc, kc, vc, bc, gc = prep(q), prep(k), prep(v), prep(beta), prep(g)
    gcum = jnp.cumsum(gc, axis=-1)
    tri = jnp.tril(jnp.ones((C, C), dtype=bool))
    strict = jnp.tril(jnp.ones((C, C), dtype=bool), -1)
    decay = jnp.exp(jnp.where(tri, gcum[..., :, None] - gcum[..., None, :], -jnp.inf))
    kb = kc * bc[..., None]
    m = jnp.where(strict, jnp.einsum('nbhid,nbhjd->nbhij', kb, kc) * decay, 0.0)
    eye = jnp.eye(C, dtype=jnp.float32)
    tmat = lax.linalg.triangular_solve(eye + m, jnp.broadcast_to(eye, m.shape),
                                       left_side=True, lower=True, unit_diagonal=True)
    u = tmat @ (vc * bc[..., None])
    w = tmat @ (kb * jnp.exp(gcum)[..., None])
    qk = jnp.einsum('nbhid,nbhjd->nbhij', qc, kc) * decay
    qg = qc * jnp.exp(gcum)[..., None]
    kg = kc * jnp.exp(gcum[..., -1:] - gcum)[..., None]
    glast = jnp.exp(gcum[..., -1])

    def step(s, xs):
        u_i, w_i, qk_i, qg_i, kg_i, gl_i = xs
        v_new = u_i - w_i @ s
        o_i = qg_i @ s + qk_i @ v_new
        s = s * gl_i[..., None, None] + jnp.swapaxes(kg_i, -1, -2) @ v_new
        return s, o_i

    s, o = lax.scan(step, s0.astype(jnp.float32), (u, w, qk, qg, kg, glast))
    o = jnp.moveaxis(o, 0, 1)
    o = jnp.moveaxis(o, 2, 3).reshape(B, n * C, H, DV)[:, :T]
    return o, s


def sparse_attend(q, k_sel, v_sel, valid):
    s = jnp.einsum('bqhd,bqkhd->bqhk', q.astype(jnp.float32), k_sel.astype(jnp.float32)) * (q.shape[-1] ** -0.5)
    s = jnp.where(valid[:, :, None, :], s, -jnp.inf)
    p = jax.nn.softmax(s, axis=-1)
    return jnp.einsum('bqhk,bqkhd->bqhd', p, v_sel.astype(jnp.float32)).astype(q.dtype)


def indexer_scores(qi, ki, wi):
    dots = jnp.einsum('bqhd,bsd->bqhs', qi.astype(jnp.float32), ki.astype(jnp.float32))
    return jnp.einsum('bqhs,bqh->bqs', jax.nn.relu(dots), wi.astype(jnp.float32))


def gather_rows(rows, idx):
    return jax.vmap(lambda r, i: r[i])(rows, idx)


def dsa_prompt(q, k, v, qi, ki, wi):
    B, T, H, d = q.shape
    ktop = min(TOPK_MAX, T // 4)
    nb = T // QUERY_BLOCK
    spos = jnp.arange(T)

    def blk(args):
        qb, qib, wib, start = args
        tpos = start + jnp.arange(QUERY_BLOCK)
        sc = indexer_scores(qib, ki, wib)
        sc = jnp.where((spos[None, :] <= tpos[:, None])[None], sc, -jnp.inf)
        _, idx = lax.top_k(sc, ktop)
        valid = idx <= tpos[None, :, None]
        return sparse_attend(qb, gather_rows(k, idx), gather_rows(v, idx), valid)

    to_blocks = lambda a: jnp.swapaxes(a.reshape((B, nb, QUERY_BLOCK) + a.shape[2:]), 0, 1)
    starts = jnp.arange(nb, dtype=jnp.int32) * QUERY_BLOCK
    out = lax.map(blk, (to_blocks(q), to_blocks(qi), to_blocks(wi), starts))
    return jnp.swapaxes(out, 0, 1).reshape(B, T, H * d)


def dsa_sample(q, k, v, qi, ki, wi, cache_k, cache_v, cache_idx_k, page_table):
    DB, S_new, H, d = q.shape
    n_past = page_table.shape[1] * PAGE_SIZE
    ktop = min(TOPK_MAX, (n_past + S_new) // 4)
    ki_past = cache_idx_k[page_table].reshape(DB, n_past, IDX_DH).astype(ki.dtype)
    ki_all = jnp.concatenate([ki_past, ki], axis=1)
    tpos = n_past + jnp.arange(S_new)
    sc = indexer_scores(qi, ki_all, wi)
    sc = jnp.where((jnp.arange(n_past + S_new)[None, :] <= tpos[:, None])[None], sc, -jnp.inf)
    _, idx = lax.top_k(sc, ktop)
    valid = idx <= tpos[None, :, None]
    pidx = jnp.minimum(idx, n_past - 1)
    phys = jax.vmap(lambda pt, i: pt[i])(page_table, pidx // PAGE_SIZE)
    off = pidx % PAGE_SIZE
    nidx = jnp.clip(idx - n_past, 0, S_new - 1)
    is_new = (idx >= n_past)[..., None, None]
    k_sel = jnp.where(is_new, gather_rows(k, nidx), cache_k[phys, off].astype(k.dtype))
    v_sel = jnp.where(is_new, gather_rows(v, nidx), cache_v[phys, off].astype(v.dtype))
    return sparse_attend(q, k_sel, v_sel, valid).reshape(DB, S_new, H * d)


def layer_step(x, p, pos, conv_prev, s0, attend, lw):
    (norm_in, w_in, conv_w, a_log, dt_bias, gdn_norm, q_norm, k_norm,
     w_proj_a, w_proj_b, w_out, ple_norm, w_ple_gate, w_ple_proj) = lw
    B, T, _ = x.shape
    h = rmsnorm(x, norm_in)
    (gq, gk, gv, gz, ga, gb, aq, ak, av, az, iq, ik, iw, mga, mgb) = split_projection(h @ w_in)
    xc = jnp.concatenate([gq, gk, gv], axis=-1)
    xpad = jnp.concatenate([conv_prev.astype(xc.dtype), xc], axis=1)
    conv = jax.nn.silu(sum(xpad[:, j:j + T] * conv_w[j] for j in range(CONV_W)))
    conv_new = xpad[:, T:]
    hk = GDN_HEADS * GDN_DK
    q_a = l2norm(conv[..., :hk].reshape(B, T, GDN_HEADS, GDN_DK)) * (GDN_DK ** -0.5)
    k_a = l2norm(conv[..., hk:2 * hk].reshape(B, T, GDN_HEADS, GDN_DK))
    v_a = conv[..., 2 * hk:].reshape(B, T, GDN_HEADS, GDN_DV)
    beta = jax.nn.sigmoid(gb.astype(jnp.float32))
    g = -jnp.exp(a_log.astype(jnp.float32)) * jax.nn.softplus(ga.astype(jnp.float32) + dt_bias.astype(jnp.float32))
    o_a, s_new = gated_delta_chunked(q_a, k_a, v_a, beta, g, s0)
    o_a = rmsnorm(o_a, gdn_norm).astype(x.dtype) * jax.nn.silu(gz.reshape(B, T, GDN_HEADS, GDN_DV))
    out_a = o_a.reshape(B, T, GDN_WIDTH)
    q_b = rope(rmsnorm(aq.reshape(B, T, ATT_HEADS, ATT_DH), q_norm), pos)
    k_b = rope(rmsnorm(ak.reshape(B, T, ATT_HEADS, ATT_DH), k_norm), pos)
    v_b = av.reshape(B, T, ATT_HEADS, ATT_DH)
    qi = rope(iq.reshape(B, T, IDX_HEADS, IDX_DH), pos)
    ki = rope(ik[:, :, None, :], pos)[:, :, 0]
    wi = iw * (IDX_HEADS ** -0.5 * IDX_DH ** -0.5)
    out_b = attend(q_b, k_b, v_b, qi, ki, wi) * jax.nn.silu(az)
    merged = jax.nn.sigmoid(mga) * (out_a @ w_proj_a) + jax.nn.sigmoid(mgb) * (out_b @ w_proj_b)
    x = x + merged @ w_out
    gate = jax.nn.sigmoid(rmsnorm(x, ple_norm) @ w_ple_gate)
    y = x + gate * (p @ w_ple_proj)
    return y, k_b, v_b, ki, s_new.astype(x.dtype), conv_new


def setup_inputs(seed: int = 0) -> dict:
    key = jax.random.key(seed)
    ks = jax.random.split(key, 32)
    f32 = jnp.float32
    nrm = lambda k, shape, scale=1.0: jax.random.normal(k, shape, f32) * scale
    n_pages = PAST_LEN // PAGE_SIZE
    n_used = DEC_BATCH * n_pages
    n_pool = n_used + max(1, n_used // 4)
    page_table = jax.random.permutation(ks[9], n_pool)[:n_used].reshape(DEC_BATCH, n_pages).astype(jnp.int32)
    dt = jnp.exp(jax.random.uniform(ks[13], (DEPTH, GDN_HEADS), f32, math.log(1e-3), math.log(1e-1)))
    return {
        'x_prompt': nrm(ks[0], (BATCH, SEQ, D_MODEL)),
        'x_sample': nrm(ks[1], (DEC_BATCH, DEC_SEQ, D_MODEL)),
        'p_prompt': nrm(ks[2], (DEPTH, BATCH, SEQ, PLE_DIM)),
        'p_sample': nrm(ks[3], (DEPTH, DEC_BATCH, DEC_SEQ, PLE_DIM)),
        'cache_k': nrm(ks[4], (DEPTH, n_pool, PAGE_SIZE, ATT_HEADS, ATT_DH)),
        'cache_v': nrm(ks[5], (DEPTH, n_pool, PAGE_SIZE, ATT_HEADS, ATT_DH)),
        'cache_idx_k': nrm(ks[6], (DEPTH, n_pool, PAGE_SIZE, IDX_DH)),
        'state_gdn': nrm(ks[7], (DEPTH, DEC_BATCH, GDN_HEADS, GDN_DK, GDN_DV), 0.1),
        'state_conv': nrm(ks[8], (DEPTH, DEC_BATCH, CONV_W - 1, CONV_CH)),
        'page_table': page_table,
        'norm_in': 1.0 + nrm(ks[10], (DEPTH, D_MODEL), 0.02),
        'w_in': nrm(ks[11], (DEPTH, D_MODEL, D_IN), D_MODEL ** -0.5),
        'conv_w': nrm(ks[12], (DEPTH, CONV_W, CONV_CH), CONV_W ** -0.5),
        'a_log': jnp.log(jax.random.uniform(ks[14], (DEPTH, GDN_HEADS), f32, 1.0, 16.0)),
        'dt_bias': jnp.log(jnp.expm1(dt)),
        'gdn_norm': 1.0 + nrm(ks[15], (DEPTH, GDN_DV), 0.02),
        'q_norm': 1.0 + nrm(ks[16], (DEPTH, ATT_DH), 0.02),
        'k_norm': 1.0 + nrm(ks[17], (DEPTH, ATT_DH), 0.02),
        'w_proj_a': nrm(ks[18], (DEPTH, GDN_WIDTH, D_MODEL), GDN_WIDTH ** -0.5),
        'w_proj_b': nrm(ks[19], (DEPTH, ATT_WIDTH, D_MODEL), ATT_WIDTH ** -0.5),
        'w_out': nrm(ks[20], (DEPTH, D_MODEL, D_MODEL), D_MODEL ** -0.5),
        'ple_norm': 1.0 + nrm(ks[21], (DEPTH, D_MODEL), 0.02),
        'w_ple_gate': nrm(ks[22], (DEPTH, D_MODEL, D_MODEL), D_MODEL ** -0.5),
        'w_ple_proj': nrm(ks[23], (DEPTH, PLE_DIM, D_MODEL), PLE_DIM ** -0.5),
    }


def reference(x_prompt, x_sample, p_prompt, p_sample, cache_k, cache_v, cache_idx_k, state_gdn, state_conv,
              page_table, norm_in, w_in, conv_w, a_log, dt_bias, gdn_norm, q_norm, k_norm,
              w_proj_a, w_proj_b, w_out, ple_norm, w_ple_gate, w_ple_proj):
    B, T, _ = x_prompt.shape
    DB, S_new, _ = x_sample.shape
    n_past = page_table.shape[1] * PAGE_SIZE
    pos_p = jnp.arange(T)
    pos_s = n_past + jnp.arange(S_new)
    hp, hs = x_prompt, x_sample
    kp_l, vp_l, ikp_l, sp_l, cp_l = [], [], [], [], []
    ks_l, vs_l, iks_l, ss_l, cs_l = [], [], [], [], []
    for i in range(DEPTH):
        lw = (norm_in[i], w_in[i], conv_w[i], a_log[i], dt_bias[i], gdn_norm[i], q_norm[i], k_norm[i],
              w_proj_a[i], w_proj_b[i], w_out[i], ple_norm[i], w_ple_gate[i], w_ple_proj[i])
        conv0 = jnp.zeros((B, CONV_W - 1, CONV_CH), hp.dtype)
        s00 = jnp.zeros((B, GDN_HEADS, GDN_DK, GDN_DV), hp.dtype)
        hp, kp, vp, ikp, sp, cp = layer_step(hp, p_prompt[i], pos_p, conv0, s00, dsa_prompt, lw)
        ck, cv, cik = cache_k[i], cache_v[i], cache_idx_k[i]
        attend_s = lambda q, k, v, qi, ki, wi: dsa_sample(q, k, v, qi, ki, wi, ck, cv, cik, page_table)
        hs, ksm, vsm, iks, ss, cs = layer_step(hs, p_sample[i], pos_s, state_conv[i], state_gdn[i], attend_s, lw)
        kp_l.append(kp); vp_l.append(vp); ikp_l.append(ikp); sp_l.append(sp); cp_l.append(cp)
        ks_l.append(ksm); vs_l.append(vsm); iks_l.append(iks); ss_l.append(ss); cs_l.append(cs)
    k_prompt = jnp.stack(kp_l)
    v_prompt = jnp.stack(vp_l)
    idx_k_prompt = jnp.stack(ikp_l)
    gdn_state_prompt = jnp.stack(sp_l)
    conv_state_prompt = jnp.stack(cp_l)
    k_sample = jnp.stack(ks_l)
    v_sample = jnp.stack(vs_l)
    idx_k_sample = jnp.stack(iks_l)
    gdn_state_sample = jnp.stack(ss_l)
    conv_state_sample = jnp.stack(cs_l)
    return (hp, hs, k_prompt, v_prompt, idx_k_prompt, gdn_state_prompt, conv_state_prompt,
            k_sample, v_sample, idx_k_sample, gdn_state_sample, conv_state_sample)
```

```python
import functools
import math

import jax
import jax.numpy as jnp
from jax import lax
from jax.experimental import pallas as pl
from jax.experimental.pallas import tpu as pltpu

f32 = jnp.float32
bf16 = jnp.bfloat16
i32 = jnp.int32

D_MODEL = 2048
PAGE_SIZE = 128
GDN_HEADS = 8
GDN_DK = 128
GDN_DV = 128
GDN_WIDTH = GDN_HEADS * GDN_DV
CONV_W = 4
CONV_CH = 2 * GDN_HEADS * GDN_DK + GDN_WIDTH
GDN_CHUNK = 64
ATT_HEADS = 8
ATT_DH = 128
ATT_WIDTH = ATT_HEADS * ATT_DH
IDX_HEADS = 16
IDX_DH = 128
TOPK_MAX = 256
QUERY_BLOCK = 128
ROPE_THETA = 10000.0
PLE_DIM = 256
EPS = 1e-6

LANES = 128
VMEM_LIMIT = 56 * 1024 * 1024
NEG_BIG = -1e30
INT_MIN = -(2 ** 31)

HK = GDN_HEADS * GDN_DK
C_GQ, C_GK, C_GV, C_GZ = 0, HK, 2 * HK, 3 * HK
C_AQ = 4 * HK
C_AK = C_AQ + ATT_WIDTH
C_AV = C_AK + ATT_WIDTH
C_AZ = C_AV + ATT_WIDTH
C_IQ = C_AZ + ATT_WIDTH
C_MGA = C_IQ + IDX_HEADS * IDX_DH
C_MGB = C_MGA + D_MODEL
C_IK = C_MGB + D_MODEL
C_SMALL = C_IK + IDX_DH
NZ = C_SMALL + LANES
SM_GA, SM_GB, SM_IW = 0, GDN_HEADS, 2 * GDN_HEADS


def _cparams(sem):
    return pltpu.CompilerParams(dimension_semantics=sem, vmem_limit_bytes=VMEM_LIMIT)


def _dot(a, b):
    return jnp.dot(a, b, preferred_element_type=f32)


def _dot_nt(a, b):
    return lax.dot_general(a, b, (((1,), (1,)), ((), ())), preferred_element_type=f32)


def _sigmoid(x):
    return 1.0 / (1.0 + jnp.exp(-x))


def _silu(x):
    return x * _sigmoid(x)


def _inproj_kernel(x_ref, g_ref, w_ref, o_ref, h_ref):
    @pl.when(pl.program_id(1) == 0)
    def _():
        x = x_ref[...]
        ms = jnp.mean(x * x, axis=-1, keepdims=True)
        h_ref[...] = (x * lax.rsqrt(ms + EPS) * g_ref[...]).astype(bf16)

    o_ref[...] = _dot(h_ref[...], w_ref[...])


def _inproj(x2d, gain_row, w_bf16, tm, tn):
    m, k = x2d.shape
    n = w_bf16.shape[1]
    return pl.pallas_call(
        _inproj_kernel,
        grid=(m // tm, n // tn),
        in_specs=[
            pl.BlockSpec((tm, k), lambda i, j: (i, 0)),
            pl.BlockSpec((1, k), lambda i, j: (0, 0)),
            pl.BlockSpec((k, tn), lambda i, j: (0, j)),
        ],
        out_specs=pl.BlockSpec((tm, tn), lambda i, j: (i, j)),
        out_shape=jax.ShapeDtypeStruct((m, n), f32),
        scratch_shapes=[pltpu.VMEM((tm, k), bf16)],
        compiler_params=_cparams(("parallel", "arbitrary")),
    )(x2d, gain_row, w_bf16)


def _softplus(x):
    return jnp.maximum(x, 0.0) + jnp.log1p(jnp.exp(-jnp.abs(x)))


def _gdn_heads_out(conv_fn, q_ref, k_ref, v_ref):
    for c in range(3 * GDN_HEADS):
        a = _silu(conv_fn(c))
        h = c % GDN_HEADS
        sl = slice(h * LANES, (h + 1) * LANES)
        if c < 2 * GDN_HEADS:
            nrm = lax.rsqrt(jnp.sum(a * a, axis=-1, keepdims=True) + EPS)
            if c < GDN_HEADS:
                q_ref[:, sl] = a * nrm * (GDN_DK ** -0.5)
            else:
                k_ref[:, sl] = a * nrm
        else:
            v_ref[:, sl] = a


def _gdn_prep_kernel(x_ref, halo_ref, prev_ref, cw_ref, sm_ref, a_ref, dt_ref,
                     q_ref, k_ref, v_ref, bg_ref, xe_ref, *, tm):
    i = pl.program_id(1)
    xe_ref[pl.ds(8, tm), :] = x_ref[0]

    @pl.when(i == 0)
    def _():
        xe_ref[pl.ds(0, 8), :] = prev_ref[0]

    @pl.when(i > 0)
    def _():
        xe_ref[pl.ds(0, 8), :] = halo_ref[0]

    def conv_fn(c):
        sl = slice(c * LANES, (c + 1) * LANES)
        acc = xe_ref[pl.ds(8 - (CONV_W - 1), tm), sl] * cw_ref[0:1, sl]
        for j in range(1, CONV_W):
            acc = acc + xe_ref[pl.ds(8 - (CONV_W - 1) + j, tm), sl] * cw_ref[j:j + 1, sl]
        return acc

    _gdn_heads_out(conv_fn, q_ref.at[0], k_ref.at[0], v_ref.at[0])

    sm = sm_ref[0]
    g = -jnp.exp(a_ref[...]) * _softplus(sm + dt_ref[...])
    row = lax.broadcasted_iota(i32, (tm, LANES), 0) % GDN_CHUNK
    s = 1
    while s < GDN_CHUNK:
        g = g + jnp.where(row >= s, pltpu.roll(g, s, axis=0), 0.0)
        s *= 2
    lane = lax.broadcasted_iota(i32, (tm, LANES), 1)
    bg_ref[0] = jnp.where(lane < GDN_HEADS, g, _sigmoid(sm))


def _gdn_prep(z3, prev8, conv_w, a_row, dt_row, tm):
    b, t, _ = z3.shape
    nblk = tm // 8
    kern = functools.partial(_gdn_prep_kernel, tm=tm)
    out_sd = jax.ShapeDtypeStruct((b, t, HK), f32)
    return pl.pallas_call(
        kern,
        grid=(b, t // tm),
        in_specs=[
            pl.BlockSpec((1, tm, CONV_CH), lambda bi, i: (bi, i, 0)),
            pl.BlockSpec((1, 8, CONV_CH), lambda bi, i: (bi, jnp.maximum(i * nblk - 1, 0), 0)),
            pl.BlockSpec((1, 8, CONV_CH), lambda bi, i: (bi, 0, 0)),
            pl.BlockSpec((CONV_W, CONV_CH), lambda bi, i: (0, 0)),
            pl.BlockSpec((1, tm, LANES), lambda bi, i: (bi, i, C_SMALL // LANES)),
            pl.BlockSpec((1, LANES), lambda bi, i: (0, 0)),
            pl.BlockSpec((1, LANES), lambda bi, i: (0, 0)),
        ],
        out_specs=[
            pl.BlockSpec((1, tm, HK), lambda bi, i: (bi, i, 0)),
            pl.BlockSpec((1, tm, HK), lambda bi, i: (bi, i, 0)),
            pl.BlockSpec((1, tm, HK), lambda bi, i: (bi, i, 0)),
            pl.BlockSpec((1, tm, LANES), lambda bi, i: (bi, i, 0)),
        ],
        out_shape=[out_sd, out_sd, out_sd, jax.ShapeDtypeStruct((b, t, LANES), f32)],
        scratch_shapes=[pltpu.VMEM((tm + 8, CONV_CH), f32)],
        compiler_params=_cparams(("parallel", "arbitrary")),
    )(z3, z3, prev8, conv_w, z3, a_row, dt_row)


def _gdn_prep_sample_kernel(x_ref, s0_ref, s1_ref, s2_ref, cw_ref, sm_ref, a_ref, dt_ref,
                            q_ref, k_ref, v_ref, bg_ref):
    def conv_fn(c):
        sl = slice(c * LANES, (c + 1) * LANES)
        return (s0_ref[:, sl] * cw_ref[0:1, sl] + s1_ref[:, sl] * cw_ref[1:2, sl]
                + s2_ref[:, sl] * cw_ref[2:3, sl] + x_ref[:, sl] * cw_ref[3:4, sl])

    _gdn_heads_out(conv_fn, q_ref, k_ref, v_ref)
    sm = sm_ref[...]
    g = -jnp.exp(a_ref[...]) * _softplus(sm + dt_ref[...])
    lane = lax.broadcasted_iota(i32, sm.shape, 1)
    bg_ref[...] = jnp.where(lane < GDN_HEADS, g, _sigmoid(sm))


def _gdn_prep_sample(zs, sc0, sc1, sc2, conv_w, a_row, dt_row):
    db = zs.shape[0]
    full = lambda shape: pl.BlockSpec(shape, lambda i: (0,) * len(shape))
    out_sd = jax.ShapeDtypeStruct((db, HK), f32)
    return pl.pallas_call(
        _gdn_prep_sample_kernel,
        grid=(1,),
        in_specs=[
            pl.BlockSpec((db, CONV_CH), lambda i: (0, 0)),
            full((db, CONV_CH)), full((db, CONV_CH)), full((db, CONV_CH)),
            full((CONV_W, CONV_CH)),
            pl.BlockSpec((db, LANES), lambda i: (0, C_SMALL // LANES)),
            full((1, LANES)), full((1, LANES)),
        ],
        out_specs=[full((db, HK)), full((db, HK)), full((db, HK)), full((db, LANES))],
        out_shape=[out_sd, out_sd, out_sd, jax.ShapeDtypeStruct((db, LANES), f32)],
        compiler_params=_cparams(("arbitrary",)),
    )(zs, sc0, sc1, sc2, conv_w, zs, a_row, dt_row)


GDN_TS = 512
GDN_HB = 2


def _bmm(a, b):
    return jnp.einsum('bij,bjk->bik', a, b, preferred_element_type=f32)


def _bmm_nt(a, b):
    return jnp.einsum('bid,bjd->bij', a, b, preferred_element_type=f32)


def _gdn_scan_kernel(q_ref, k_ref, v_ref, gz_ref, gc_ref, bt_ref, gr_ref, s0_ref, nrm_ref,
                     o_ref, so_ref, s_ref):
    step = pl.program_id(2)
    nc = GDN_TS // GDN_CHUNK
    c = GDN_CHUNK

    @pl.when(step == 0)
    def _():
        s_ref[...] = s0_ref[0]

    ri = lax.broadcasted_iota(i32, (1, c, c), 1)
    ci = lax.broadcasted_iota(i32, (1, c, c), 2)
    tri = ri >= ci
    strict = ri > ci
    eye = (ri == ci).astype(f32)

    pre = []
    for hh in range(GDN_HB):
        sl = slice(hh * LANES, (hh + 1) * LANES)
        q = q_ref[0, :, sl].reshape(nc, c, LANES)
        k = k_ref[0, :, sl].reshape(nc, c, LANES)
        v = v_ref[0, :, sl].reshape(nc, c, LANES)
        gc = gc_ref[0, hh].reshape(nc, c, 1)
        bt = bt_ref[0, hh].reshape(nc, c, 1)
        gr = gr_ref[0, hh]
        glast = gr[:, :, c - 1:c]
        decay = jnp.exp(jnp.where(tri, gc - gr, -jnp.inf))
        kb = k * bt
        m = jnp.where(strict, _bmm_nt(kb, k) * decay, 0.0)
        x = eye - m
        p = m
        for _ in range(5):
            p = _bmm(p, p)
            x = x + _bmm(x, p)
        eg = jnp.exp(gc)
        u = _bmm(x, v * bt)
        w = _bmm(x, kb * eg)
        qk = _bmm_nt(q, k) * decay
        qg = q * eg
        kg = k * jnp.exp(glast - gc)
        gl = jnp.exp(glast)
        pre.append((u, w, qk, qg, kg, gl))

    gain = nrm_ref[...]
    for ch in range(nc):
        for hh in range(GDN_HB):
            u, w, qk, qg, kg, gl = pre[hh]
            s = s_ref[hh]
            v_new = u[ch] - _dot(w[ch], s)
            o = _dot(qg[ch], s) + _dot(qk[ch], v_new)
            s_ref[hh] = s * gl[ch] + lax.dot_general(
                kg[ch], v_new, (((0,), (0,)), ((), ())), preferred_element_type=f32)
            ms = jnp.mean(o * o, axis=-1, keepdims=True)
            on = o * lax.rsqrt(ms + EPS) * gain
            rows = pl.ds(ch * c, c)
            sl = slice(hh * LANES, (hh + 1) * LANES)
            o_ref[0, rows, sl] = (on * _silu(gz_ref[0, rows, sl])).astype(o_ref.dtype)

    @pl.when(step == pl.num_programs(2) - 1)
    def _():
        so_ref[0] = s_ref[...]


def _gdn_scan(q, k, v, z3, gc_col, bt_col, gc_row, s0, gdn_norm_row):
    b, t, _ = q.shape
    hw = GDN_HB * LANES
    nc = GDN_TS // GDN_CHUNK
    qspec = pl.BlockSpec((1, GDN_TS, hw), lambda bi, hg, i: (bi, i, hg))
    return pl.pallas_call(
        _gdn_scan_kernel,
        grid=(b, GDN_HEADS // GDN_HB, t // GDN_TS),
        in_specs=[
            qspec, qspec, qspec,
            pl.BlockSpec((1, GDN_TS, hw), lambda bi, hg, i: (bi, i, C_GZ // hw + hg)),
            pl.BlockSpec((1, GDN_HB, GDN_TS, 1), lambda bi, hg, i: (bi, hg, i, 0)),
            pl.BlockSpec((1, GDN_HB, GDN_TS, 1), lambda bi, hg, i: (bi, hg, i, 0)),
            pl.BlockSpec((1, GDN_HB, nc, 1, GDN_CHUNK), lambda bi, hg, i: (bi, hg, i, 0, 0)),
            pl.BlockSpec((1, GDN_HB, GDN_DK, GDN_DV), lambda bi, hg, i: (bi, hg, 0, 0)),
            pl.BlockSpec((1, LANES), lambda bi, hg, i: (0, 0)),
        ],
        out_specs=[
            pl.BlockSpec((1, GDN_TS, hw), lambda bi, hg, i: (bi, i, hg)),
            pl.BlockSpec((1, GDN_HB, GDN_DK, GDN_DV), lambda bi, hg, i: (bi, hg, 0, 0)),
        ],
        out_shape=[jax.ShapeDtypeStruct((b, t, GDN_WIDTH), bf16),
                   jax.ShapeDtypeStruct((b, GDN_HEADS, GDN_DK, GDN_DV), f32)],
        scratch_shapes=[pltpu.VMEM((GDN_HB, GDN_DK, GDN_DV), f32)],
        compiler_params=_cparams(("parallel", "parallel", "arbitrary")),
    )(q, k, v, z3, gc_col, bt_col, gc_row, s0, gdn_norm_row)


def _gdn_step_kernel(q_ref, k_ref, v_ref, gz_ref, g_ref, bt_ref, s0_ref, nrm_ref, o_ref, so_ref):
    for h in range(GDN_HEADS):
        s0 = s0_ref[0, h]
        qc = q_ref[0, h]
        kc = k_ref[0, h]
        vr = v_ref[0, h]
        eg = jnp.exp(g_ref[0, h])
        bt = bt_ref[0, h]
        ks = jnp.sum(kc * s0, axis=0, keepdims=True)
        qs = jnp.sum(qc * s0, axis=0, keepdims=True)
        qk = jnp.sum(qc * kc, axis=0, keepdims=True)
        v_new = bt * vr - (bt * eg) * ks
        o = eg * qs + qk * v_new
        so_ref[0, h] = s0 * eg + kc * v_new
        ms = jnp.mean(o * o, axis=-1, keepdims=True)
        on = o * lax.rsqrt(ms + EPS) * nrm_ref[...]
        o_ref[0, h] = (on * _silu(gz_ref[0, h])).astype(o_ref.dtype)


def _gdn_step(q_col, k_col, v_row, gz_row, g11, bt11, s0, gdn_norm_row):
    db = q_col.shape[0]
    col = pl.BlockSpec((1, GDN_HEADS, GDN_DK, 1), lambda i: (i, 0, 0, 0))
    row = pl.BlockSpec((1, GDN_HEADS, 1, GDN_DV), lambda i: (i, 0, 0, 0))
    one = pl.BlockSpec((1, GDN_HEADS, 1, 1), lambda i: (i, 0, 0, 0))
    st = pl.BlockSpec((1, GDN_HEADS, GDN_DK, GDN_DV), lambda i: (i, 0, 0, 0))
    return pl.pallas_call(
        _gdn_step_kernel,
        grid=(db,),
        in_specs=[col, col, row, row, one, one, st, pl.BlockSpec((1, LANES), lambda i: (0, 0))],
        out_specs=[row, st],
        out_shape=[jax.ShapeDtypeStruct((db, GDN_HEADS, 1, GDN_DV), bf16),
                   jax.ShapeDtypeStruct((db, GDN_HEADS, GDN_DK, GDN_DV), f32)],
        compiler_params=_cparams(("parallel",)),
    )(q_col, k_col, v_row, gz_row, g11, bt11, s0, gdn_norm_row)


def _rope(x, cos, sin_signed):
    return x * cos + pltpu.roll(x, LANES // 2, axis=1) * sin_signed


def _dsa_prep_kernel(aq_ref, ak_ref, av_ref, iq_ref, ik_ref, sm_ref, cos_ref, sin_ref,
                     qn_ref, kn_ref, *out_refs, transposed):
    if transposed:
        qb_ref, kf_ref, kb_ref, vt_ref, qi_ref, kif_ref, kib_ref, wt_ref = out_refs
    else:
        qb_ref, kf_ref, qi_ref, kif_ref, kib_ref = out_refs
    cos = cos_ref[...]
    sin = sin_ref[...]
    for h in range(ATT_HEADS):
        sl = slice(h * LANES, (h + 1) * LANES)
        a = aq_ref[0, :, sl]
        a = a * lax.rsqrt(jnp.mean(a * a, axis=-1, keepdims=True) + EPS) * qn_ref[...]
        qb_ref[0, :, sl] = (_rope(a, cos, sin) * (ATT_DH ** -0.5)).astype(qb_ref.dtype)
        a = ak_ref[0, :, sl]
        a = a * lax.rsqrt(jnp.mean(a * a, axis=-1, keepdims=True) + EPS) * kn_ref[...]
        kr = _rope(a, cos, sin)
        kf_ref[0, :, sl] = kr
        if transposed:
            kb_ref[0, :, sl] = kr.astype(bf16)
            vt_ref[0, 0, sl, :] = av_ref[0, :, sl].T.astype(bf16)
    for h in range(IDX_HEADS):
        sl = slice(h * LANES, (h + 1) * LANES)
        qi_ref[0, :, sl] = _rope(iq_ref[0, :, sl], cos, sin).astype(bf16)
    ki = _rope(ik_ref[0], cos, sin)
    kif_ref[0] = ki
    kib_ref[0] = ki.astype(bf16)
    if transposed:
        wt_ref[0] = (sm_ref[0] * (IDX_HEADS ** -0.5 * IDX_DH ** -0.5)).T


def _dsa_prep(z3, cos, sin, qn_row, kn_row, tm, transposed, kc):
    b, t, _ = z3.shape
    zspec = lambda w, off: pl.BlockSpec((1, tm, w), lambda bi, i: (bi, i, off // w))
    tok = lambda w, dt: (pl.BlockSpec((1, tm, w), lambda bi, i: (bi, i, 0)),
                         jax.ShapeDtypeStruct((b, t, w), dt))
    if transposed:
        per = kc // tm
        outs = [tok(ATT_WIDTH, bf16), tok(ATT_WIDTH, f32), tok(ATT_WIDTH, bf16),
                (pl.BlockSpec((1, 1, ATT_WIDTH, tm), lambda bi, i: (bi, i // per, 0, i % per)),
                 jax.ShapeDtypeStruct((b, t // kc, ATT_WIDTH, kc), bf16)),
                tok(IDX_HEADS * IDX_DH, bf16), tok(IDX_DH, f32), tok(IDX_DH, bf16),
                (pl.BlockSpec((1, LANES, tm), lambda bi, i: (bi, 0, i)),
                 jax.ShapeDtypeStruct((b, LANES, t), f32))]
    else:
        outs = [tok(ATT_WIDTH, f32), tok(ATT_WIDTH, f32),
                tok(IDX_HEADS * IDX_DH, bf16), tok(IDX_DH, f32), tok(IDX_DH, bf16)]
    kern = functools.partial(_dsa_prep_kernel, transposed=transposed)
    return pl.pallas_call(
        kern,
        grid=(b, t // tm),
        in_specs=[
            zspec(ATT_WIDTH, C_AQ), zspec(ATT_WIDTH, C_AK), zspec(ATT_WIDTH, C_AV),
            zspec(IDX_HEADS * IDX_DH, C_IQ), zspec(IDX_DH, C_IK), zspec(LANES, C_SMALL),
            pl.BlockSpec((tm, LANES), lambda bi, i: (i, 0)),
            pl.BlockSpec((tm, LANES), lambda bi, i: (i, 0)),
            pl.BlockSpec((1, LANES), lambda bi, i: (0, 0)),
            pl.BlockSpec((1, LANES), lambda bi, i: (0, 0)),
        ],
        out_specs=[o[0] for o in outs],
        out_shape=[o[1] for o in outs],
        compiler_params=_cparams(("parallel", "parallel")),
    )(z3, z3, z3, z3, z3, z3, cos, sin, qn_row, kn_row)


def _key_to_float(key):
    bits = jnp.where(key >= 0, key, key ^ jnp.int32(0x7FFFFFFF))
    return pltpu.bitcast(bits, f32)


def _kth_largest(count_ge, k, shape):
    kf = jnp.float32(k)
    zero_ok = count_ge(jnp.zeros(shape, f32)) >= kf
    cur = jnp.where(zero_ok, jnp.int32(0), jnp.int32(INT_MIN))

    def body(it, cur):
        cand = cur + jnp.left_shift(jnp.int32(1), jnp.int32(30) - it)
        ok = count_ge(_key_to_float(cand)) >= kf
        return jnp.where(ok, cand, cur)

    cur = lax.fori_loop(0, 31, body, cur)
    return _key_to_float(cur)


def _tie_bound(count_eq_le, need, nbits, shape):
    lo = jnp.full(shape, -1, i32)

    def body(it, lo):
        cand = lo + jnp.left_shift(jnp.int32(1), jnp.int32(nbits - 1) - it)
        short = count_eq_le(cand) < need
        return jnp.where(short, cand, lo)

    lo = lax.fori_loop(0, nbits, body, lo)
    return lo + 1


ATT_KC = 512


def _dsa_attend_kernel(qi_ref, wt_ref, ki_ref, qb_ref, kb_ref, vt_ref, az_ref, o_ref, sc_ref, *, t_total):
    i = pl.program_id(1)
    qb = QUERY_BLOCK
    kc = ATT_KC
    nch = (i * qb + qb + kc - 1) // kc
    tq = i * qb + lax.broadcasted_iota(i32, (1, qb), 1)
    ktop = min(TOPK_MAX, t_total // 4)

    def spos(c):
        return c * kc + lax.broadcasted_iota(i32, (kc, 1), 0)

    def idx_body(c, carry):
        kch = ki_ref[0, pl.ds(c * kc, kc), :]
        acc = jnp.zeros((kc, qb), f32)
        for h in range(IDX_HEADS):
            d = _dot_nt(kch, qi_ref[0, :, h * LANES:(h + 1) * LANES])
            acc = acc + jnp.maximum(d, 0.0) * wt_ref[0, SM_IW + h:SM_IW + h + 1, :]
        sc_ref[pl.ds(c * kc, kc), :] = jnp.where(spos(c) <= tq, acc, -jnp.inf)
        return carry

    lax.fori_loop(0, nch, idx_body, 0)

    def col_count(pred_fn):
        def body(c, acc):
            m = pred_fn(sc_ref[pl.ds(c * kc, kc), :], c)
            return acc + jnp.sum(m.astype(f32).reshape(kc // 8, 8, qb), axis=0)
        acc = lax.fori_loop(0, nch, body, jnp.zeros((8, qb), f32))
        return jnp.sum(acc, axis=0, keepdims=True)

    def search():
        thr = _kth_largest(lambda cand: col_count(lambda s, c: s >= cand), ktop, (1, qb))
        c_ge = col_count(lambda s, c: s >= thr)
        c_gt = col_count(lambda s, c: s > thr)
        need = jnp.float32(ktop) - c_gt
        jb = lax.cond(
            jnp.max(c_ge) > jnp.float32(ktop),
            lambda: _tie_bound(
                lambda cand: col_count(lambda s, c: (s == thr) & (spos(c) <= cand)),
                need, max(1, (t_total - 1).bit_length()), (1, qb)),
            lambda: jnp.full((1, qb), t_total, i32))
        return thr, jb

    def take_all():
        return jnp.full((1, qb), -jnp.inf, f32), jnp.full((1, qb), t_total, i32)

    thr, jb = lax.cond((i + 1) * qb <= ktop, take_all, search)

    def bias_body(c, carry):
        s = sc_ref[pl.ds(c * kc, kc), :]
        sp = spos(c)
        sel = ((s > thr) | ((s == thr) & (sp <= jb))) & (sp <= tq)
        sc_ref[pl.ds(c * kc, kc), :] = jnp.where(sel, 0.0, NEG_BIG)
        return carry

    lax.fori_loop(0, nch, bias_body, 0)

    for h in range(ATT_HEADS):
        sl = slice(h * LANES, (h + 1) * LANES)
        qh = qb_ref[0, :, sl]

        def att_body(c, carry):
            m, l, acc = carry
            s = _dot_nt(kb_ref[0, pl.ds(c * kc, kc), sl], qh) + sc_ref[pl.ds(c * kc, kc), :]
            m_new = jnp.maximum(m, jnp.max(s, axis=0, keepdims=True))
            alpha = jnp.exp(m - m_new)
            p = jnp.exp(s - m_new)
            l = alpha * l + jnp.sum(p, axis=0, keepdims=True)
            acc = alpha * acc + _dot(vt_ref[0, c, sl, :], p.astype(bf16))
            return m_new, l, acc

        m0 = jnp.full((1, qb), NEG_BIG, f32)
        l0 = jnp.zeros((1, qb), f32)
        a0 = jnp.zeros((LANES, qb), f32)
        m, l, acc = lax.fori_loop(0, nch, att_body, (m0, l0, a0))
        out = (acc / l).T
        o_ref[0, :, sl] = (out * _silu(az_ref[0, :, sl])).astype(o_ref.dtype)


def _dsa_attend(qi, wt, ki, qb, kb, vt, z3):
    b, t, _ = qb.shape
    kern = functools.partial(_dsa_attend_kernel, t_total=t)
    return pl.pallas_call(
        kern,
        grid=(b, t // QUERY_BLOCK),
        in_specs=[
            pl.BlockSpec((1, QUERY_BLOCK, IDX_HEADS * IDX_DH), lambda bi, i: (bi, i, 0)),
            pl.BlockSpec((1, LANES, QUERY_BLOCK), lambda bi, i: (bi, 0, i)),
            pl.BlockSpec((1, t, IDX_DH), lambda bi, i: (bi, 0, 0)),
            pl.BlockSpec((1, QUERY_BLOCK, ATT_WIDTH), lambda bi, i: (bi, i, 0)),
            pl.BlockSpec((1, t, ATT_WIDTH), lambda bi, i: (bi, 0, 0)),
            pl.BlockSpec((1, t // ATT_KC, ATT_WIDTH, ATT_KC), lambda bi, i: (bi, 0, 0, 0)),
            pl.BlockSpec((1, QUERY_BLOCK, ATT_WIDTH), lambda bi, i: (bi, i, C_AZ // ATT_WIDTH)),
        ],
        out_specs=pl.BlockSpec((1, QUERY_BLOCK, ATT_WIDTH), lambda bi, i: (bi, i, 0)),
        out_shape=jax.ShapeDtypeStruct((b, t, ATT_WIDTH), bf16),
        scratch_shapes=[pltpu.VMEM((t, QUERY_BLOCK), f32)],
        compiler_params=_cparams(("parallel", "arbitrary")),
    )(qi, wt, ki, qb, kb, vt, z3)


SC_PP = 8


def _sample_scores_kernel(pt_ref, qi_ref, w_ref, *rest):
    k_refs, o_ref = rest[:SC_PP], rest[SC_PP]
    qi = qi_ref[0]
    w = w_ref[0]
    for r in range(SC_PP):
        d = _dot_nt(qi, k_refs[r][0].astype(bf16))
        o_ref[0, :, r * PAGE_SIZE:(r + 1) * PAGE_SIZE] = jnp.sum(
            jnp.maximum(d, 0.0) * w, axis=0, keepdims=True)


def _sample_scores(pt_flat, qi3, w3, cache_ik):
    db = qi3.shape[0]
    n_pages = pt_flat.shape[0] // db

    def kspec(r):
        return pl.BlockSpec((1, PAGE_SIZE, IDX_DH),
                            lambda bi, j, pt: (pt[bi * n_pages + j * SC_PP + r], 0, 0))

    grid_spec = pltpu.PrefetchScalarGridSpec(
        num_scalar_prefetch=1,
        grid=(db, n_pages // SC_PP),
        in_specs=[pl.BlockSpec((1, IDX_HEADS, IDX_DH), lambda bi, j, pt: (bi, 0, 0)),
                  pl.BlockSpec((1, IDX_HEADS, 1), lambda bi, j, pt: (bi, 0, 0))]
                 + [kspec(r) for r in range(SC_PP)],
        out_specs=pl.BlockSpec((1, 1, SC_PP * PAGE_SIZE), lambda bi, j, pt: (bi, 0, j)),
    )
    return pl.pallas_call(
        _sample_scores_kernel,
        grid_spec=grid_spec,
        out_shape=jax.ShapeDtypeStruct((db, 1, n_pages * PAGE_SIZE), f32),
        compiler_params=_cparams(("parallel", "arbitrary")),
    )(pt_flat, qi3, w3, *([cache_ik] * SC_PP))


def _sample_select_kernel(sc_ref, qi_ref, ki_ref, w_ref, idx_ref, *, n_past, ktop):
    db, n_pages, ps = sc_ref.shape
    sc = sc_ref[...]

    def red(x):
        return jnp.sum(jnp.sum(x, axis=2, keepdims=True), axis=1, keepdims=True)

    dn = jnp.sum(qi_ref[...].astype(f32) * ki_ref[...].astype(f32), axis=2, keepdims=True)
    s_new = jnp.sum(jnp.maximum(dn, 0.0) * w_ref[...], axis=1, keepdims=True)

    def count(pred):
        return red(pred(sc).astype(f32)) + pred(s_new).astype(f32)

    shape = (db, 1, 1)
    thr = _kth_largest(lambda cand: count(lambda s: s >= cand), ktop, shape)
    c_ge = count(lambda s: s >= thr)
    c_gt = count(lambda s: s > thr)
    need = jnp.float32(ktop) - c_gt
    pos = (lax.broadcasted_iota(i32, (1, n_pages, ps), 1) * ps
           + lax.broadcasted_iota(i32, (1, n_pages, ps), 2))

    def eq_le(cand):
        return (red(((sc == thr) & (pos <= cand)).astype(f32))
                + ((s_new == thr) & (n_past <= cand)).astype(f32))

    jb = lax.cond(jnp.max(c_ge) > jnp.float32(ktop),
                  lambda: _tie_bound(eq_le, need, (n_past + 1).bit_length(), shape),
                  lambda: jnp.full(shape, n_past + 1, i32))
    sel = (sc > thr) | ((sc == thr) & (pos <= jb))

    r_i = lax.broadcasted_iota(i32, (ps, ps), 0)
    c_i = lax.broadcasted_iota(i32, (ps, ps), 1)
    upper = (r_i <= c_i).astype(bf16)
    upper_pg = (lax.broadcasted_iota(i32, (n_pages, n_pages), 0)
                <= lax.broadcasted_iota(i32, (n_pages, n_pages), 1)).astype(bf16)
    jcol = lax.broadcasted_iota(i32, (ktop, 1), 0).astype(f32)
    plane = lax.broadcasted_iota(i32, (1, n_pages), 1).astype(f32)
    ones8 = jnp.ones((8, ps), bf16)

    selb = jnp.where(sel, 1.0, 0.0).astype(bf16)
    for b in range(db):
        sb = selb[b]
        incl = _dot(sb, upper)
        tot_row = _dot_nt(ones8, sb)[0:1]
        cum_row = _dot(jnp.broadcast_to(tot_row, (8, n_pages)).astype(bf16), upper_pg)[0:1]
        page_of = jnp.sum((cum_row <= jcol).astype(f32), axis=1, keepdims=True)
        onehot = (page_of == plane)
        before = jnp.sum(jnp.where(onehot, cum_row - tot_row, 0.0), axis=1, keepdims=True)
        rloc = jcol - before
        incl_rows = _dot(onehot.astype(bf16), incl.astype(bf16))
        off_of = jnp.sum((incl_rows <= rloc).astype(f32), axis=1, keepdims=True)
        idx = jnp.minimum(page_of * ps + off_of, jnp.float32(n_past))
        idx_ref[b] = idx.astype(i32)


def _sample_select(sc3, qi3, ki3, w3, n_past, ktop):
    db = sc3.shape[0]
    kern = functools.partial(_sample_select_kernel, n_past=n_past, ktop=ktop)
    full = lambda a: pl.BlockSpec(a.shape, lambda i: (0,) * a.ndim)
    return pl.pallas_call(
        kern,
        grid=(1,),
        in_specs=[full(sc3), full(qi3), full(ki3), full(w3)],
        out_specs=pl.BlockSpec((db, ktop, 1), lambda i: (0, 0, 0)),
        out_shape=jax.ShapeDtypeStruct((db, ktop, 1), i32),
        compiler_params=_cparams(("arbitrary",)),
    )(sc3, qi3, ki3, w3)


def _sample_attend_kernel(idx_s, pt_s, idxv_ref, q_ref, kn_ref, vn_ref, az_ref, ck_ref, cv_ref,
                          o_ref, kbuf, vbuf, sem, *, n_past, ktop, n_pages):
    b = pl.program_id(0)
    nb = pl.num_programs(0)

    def row_copies(bb, slot, j):
        pidx = jnp.minimum(idx_s[bb * ktop + j], n_past - 1)
        row = pt_s[bb * n_pages + pidx // PAGE_SIZE] * PAGE_SIZE + pidx % PAGE_SIZE
        ck = pltpu.make_async_copy(ck_ref.at[pl.ds(row, 1), :], kbuf.at[slot, pl.ds(j, 1), :], sem.at[0, slot])
        cv = pltpu.make_async_copy(cv_ref.at[pl.ds(row, 1), :], vbuf.at[slot, pl.ds(j, 1), :], sem.at[1, slot])
        return ck, cv

    def issue(bb, slot):
        def body(j, carry):
            ck, cv = row_copies(bb, slot, j)
            ck.start()
            cv.start()
            return carry
        lax.fori_loop(0, ktop, body, 0)

    def wait_all(bb, slot):
        def body(j, carry):
            ck, cv = row_copies(bb, slot, j)
            ck.wait()
            cv.wait()
            return carry
        lax.fori_loop(0, ktop, body, 0)

    slot = b % 2

    @pl.when(b == 0)
    def _():
        issue(b, slot)

    @pl.when(b + 1 < nb)
    def _():
        issue(b + 1, 1 - slot)

    wait_all(b, slot)

    is_new = idxv_ref[0] >= n_past
    newf = is_new.astype(f32)
    for h in range(ATT_HEADS):
        sl = slice(h * LANES, (h + 1) * LANES)
        qh = q_ref[0, :, sl]
        q8 = jnp.broadcast_to(qh, (8, LANES)).astype(bf16)
        s = _dot_nt(q8, kbuf[slot, :, sl].astype(bf16))[0:1]
        s_new = jnp.sum(qh.astype(bf16).astype(f32) * kn_ref[0, :, sl].astype(bf16).astype(f32),
                        axis=-1, keepdims=True)
        s = jnp.where(is_new, s_new, s)
        m = jnp.max(s, axis=-1, keepdims=True)
        p = jnp.exp(s - m)
        l = jnp.sum(p, axis=-1, keepdims=True)
        p_old = jnp.broadcast_to(p * (1.0 - newf), (8, ktop)).astype(bf16)
        pv = _dot(p_old, vbuf[slot, :, sl].astype(bf16))[0:1]
        pv = pv + jnp.sum(p * newf, axis=-1, keepdims=True) * vn_ref[0, :, sl]
        o_ref[0, :, sl] = ((pv / l) * _silu(az_ref[0, :, sl])).astype(o_ref.dtype)


def _sample_attend(idx_flat, pt_flat, idx_row, q3, kn3, vn3, az3, ck2d, cv2d, n_past, ktop):
    db = q3.shape[0]
    n_pages = pt_flat.shape[0] // db
    kern = functools.partial(_sample_attend_kernel, n_past=n_past, ktop=ktop, n_pages=n_pages)
    tok = pl.BlockSpec((1, 1, ATT_WIDTH), lambda bi, a, c: (bi, 0, 0))
    grid_spec = pltpu.PrefetchScalarGridSpec(
        num_scalar_prefetch=2,
        grid=(db,),
        in_specs=[pl.BlockSpec((1, 1, ktop), lambda bi, a, c: (bi, 0, 0)), tok, tok, tok, tok,
                  pl.BlockSpec(memory_space=pl.ANY), pl.BlockSpec(memory_space=pl.ANY)],
        out_specs=tok,
        scratch_shapes=[pltpu.VMEM((2, ktop, ATT_WIDTH), f32), pltpu.VMEM((2, ktop, ATT_WIDTH), f32),
                        pltpu.SemaphoreType.DMA((2, 2))],
    )
    return pl.pallas_call(
        kern,
        grid_spec=grid_spec,
        out_shape=jax.ShapeDtypeStruct((db, 1, ATT_WIDTH), bf16),
        compiler_params=_cparams(("arbitrary",)),
    )(idx_flat, pt_flat, idx_row, q3, kn3, vn3, az3, ck2d, cv2d)


def _tail_kernel(oa_ref, ob_ref, mga_ref, mgb_ref, x_ref, p_ref, wa_ref, wb_ref, wo_ref,
                 pn_ref, wg_ref, wp_ref, y_ref):
    a = _dot(oa_ref[...], wa_ref[...])
    b = _dot(ob_ref[...], wb_ref[...])
    merged = _sigmoid(mga_ref[...]) * a + _sigmoid(mgb_ref[...]) * b
    x2 = x_ref[...] + _dot(merged.astype(bf16), wo_ref[...])
    ms = jnp.mean(x2 * x2, axis=-1, keepdims=True)
    hn = (x2 * lax.rsqrt(ms + EPS) * pn_ref[...]).astype(bf16)
    gate = _sigmoid(_dot(hn, wg_ref[...]))
    y_ref[...] = x2 + gate * _dot(p_ref[...].astype(bf16), wp_ref[...])


def _tail(oa, ob, z2d, x2d, p2d, wa, wb, wo, pn_row, wg, wp, tm):
    m = x2d.shape[0]
    const = lambda a: pl.BlockSpec(a.shape, lambda i: (0, 0), pipeline_mode=pl.Buffered(1))
    return pl.pallas_call(
        _tail_kernel,
        grid=(m // tm,),
        in_specs=[
            pl.BlockSpec((tm, GDN_WIDTH), lambda i: (i, 0)),
            pl.BlockSpec((tm, ATT_WIDTH), lambda i: (i, 0)),
            pl.BlockSpec((tm, D_MODEL), lambda i: (i, C_MGA // D_MODEL)),
            pl.BlockSpec((tm, D_MODEL), lambda i: (i, C_MGB // D_MODEL)),
            pl.BlockSpec((tm, D_MODEL), lambda i: (i, 0)),
            pl.BlockSpec((tm, PLE_DIM), lambda i: (i, 0)),
            const(wa), const(wb), const(wo), const(pn_row), const(wg), const(wp),
        ],
        out_specs=pl.BlockSpec((tm, D_MODEL), lambda i: (i, 0)),
        out_shape=jax.ShapeDtypeStruct((m, D_MODEL), f32),
        compiler_params=_cparams(("parallel",)),
    )(oa, ob, z2d, z2d, x2d, p2d, wa, wb, wo, pn_row, wg, wp)


def _permute_w_in(w):
    sizes = (HK, HK, GDN_WIDTH, GDN_WIDTH, GDN_HEADS, GDN_HEADS, ATT_WIDTH, ATT_WIDTH, ATT_WIDTH, ATT_WIDTH,
             IDX_HEADS * IDX_DH, IDX_DH, IDX_HEADS, D_MODEL, D_MODEL)
    pieces, start = [], 0
    for n in sizes:
        pieces.append(w[:, start:start + n])
        start += n
    gq, gk, gv, gz, ga, gb, aq, ak, av, az, iq, ik, iw, mga, mgb = pieces
    pad = jnp.zeros((w.shape[0], LANES - 2 * GDN_HEADS - IDX_HEADS), w.dtype)
    return jnp.concatenate([gq, gk, gv, gz, aq, ak, av, az, iq, mga, mgb, ik, ga, gb, iw, pad], axis=1)


def _lane_row(v, offset=0):
    row = jnp.zeros((1, LANES), f32)
    return row.at[0, offset:offset + v.shape[0]].set(v.astype(f32))


def _rope_tables(pos):
    half = ATT_DH // 2
    inv = ROPE_THETA ** (-jnp.arange(half, dtype=f32) * 2.0 / ATT_DH)
    ang = pos.astype(f32)[:, None] * inv[None, :]
    cos, sin = jnp.cos(ang), jnp.sin(ang)
    return jnp.concatenate([cos, cos], axis=1), jnp.concatenate([-sin, sin], axis=1)


def kernel(x_prompt, x_sample, p_prompt, p_sample, cache_k, cache_v, cache_idx_k, state_gdn, state_conv,
           page_table, norm_in, w_in, conv_w, a_log, dt_bias, gdn_norm, q_norm, k_norm,
           w_proj_a, w_proj_b, w_out, ple_norm, w_ple_gate, w_ple_proj):
    depth = norm_in.shape[0]
    assert depth == 1, "single-layer trunk"
    B, T, _ = x_prompt.shape
    DB, S_new, _ = x_sample.shape
    assert S_new == 1
    n_pages = page_table.shape[1]
    n_past = n_pages * PAGE_SIZE
    n_pool = cache_k.shape[1]
    li = 0

    w_perm = _permute_w_in(w_in[li]).astype(bf16)
    gain_in = norm_in[li].reshape(1, D_MODEL)
    a_row = _lane_row(a_log[li], SM_GA)
    dt_row = _lane_row(dt_bias[li], SM_GA)
    gdn_norm_row = gdn_norm[li].reshape(1, LANES)
    qn_row = q_norm[li].reshape(1, LANES)
    kn_row = k_norm[li].reshape(1, LANES)
    wa = w_proj_a[li].astype(bf16)
    wb = w_proj_b[li].astype(bf16)
    wo = w_out[li].astype(bf16)
    wg = w_ple_gate[li].astype(bf16)
    wp = w_ple_proj[li].astype(bf16)
    pn_row = ple_norm[li].reshape(1, D_MODEL)
    cw = conv_w[li]

    xp2 = x_prompt.reshape(B * T, D_MODEL)
    zp = _inproj(xp2, gain_in, w_perm, tm=1024, tn=768)
    zp3 = zp.reshape(B, T, NZ)

    prev8 = jnp.zeros((B, 8, CONV_CH), f32)
    qa, ka, va, bg = _gdn_prep(zp3, prev8, cw, a_row, dt_row, tm=256)
    gcum = bg[..., 0:GDN_HEADS]
    beta = bg[..., GDN_HEADS:2 * GDN_HEADS]
    gc_col = jnp.swapaxes(gcum, 1, 2)[..., None]
    bt_col = jnp.swapaxes(beta, 1, 2)[..., None]
    gc_row = jnp.swapaxes(gcum, 1, 2).reshape(B, GDN_HEADS, T // GDN_CHUNK, 1, GDN_CHUNK)
    s00 = jnp.zeros((B, GDN_HEADS, GDN_DK, GDN_DV), f32)
    oa_p, s_p = _gdn_scan(qa, ka, va, zp3, gc_col, bt_col, gc_row, s00, gdn_norm_row)
    conv_p = zp3[:, T - (CONV_W - 1):, C_GQ:C_GQ + CONV_CH]

    cos_p, sin_p = _rope_tables(jnp.arange(T))
    qb_p, kf_p, kb_p, vt_p, qi_p, kif_p, kib_p, wt_p = _dsa_prep(
        zp3, cos_p, sin_p, qn_row, kn_row, tm=256, transposed=True, kc=ATT_KC)
    ob_p = _dsa_attend(qi_p, wt_p, kib_p, qb_p, kb_p, vt_p, zp3)
    v_p = zp3[:, :, C_AV:C_AV + ATT_WIDTH]

    y_p = _tail(oa_p.reshape(B * T, GDN_WIDTH), ob_p.reshape(B * T, ATT_WIDTH), zp, xp2,
                p_prompt[li].reshape(B * T, PLE_DIM), wa, wb, wo, pn_row, wg, wp, tm=256)

    xs2 = x_sample.reshape(DB, D_MODEL)
    zs = _inproj(xs2, gain_in, w_perm, tm=DB, tn=768)
    sconv = state_conv[li]
    qs, ks, vs, bgs = _gdn_prep_sample(zs, sconv[:, 0], sconv[:, 1], sconv[:, 2], cw, a_row, dt_row)
    g_s = bgs[:, 0:GDN_HEADS].reshape(DB, GDN_HEADS, 1, 1)
    bt_s = bgs[:, GDN_HEADS:2 * GDN_HEADS].reshape(DB, GDN_HEADS, 1, 1)
    gz_s = zs[:, C_GZ:C_GZ + GDN_WIDTH].reshape(DB, GDN_HEADS, 1, GDN_DV)
    oa_s, s_s = _gdn_step(qs.reshape(DB, GDN_HEADS, GDN_DK, 1), ks.reshape(DB, GDN_HEADS, GDN_DK, 1),
                          vs.reshape(DB, GDN_HEADS, 1, GDN_DV), gz_s, g_s, bt_s, state_gdn[li], gdn_norm_row)
    conv_s = jnp.concatenate([sconv[:, 1:], zs[:, None, C_GQ:C_GQ + CONV_CH]], axis=1)

    cos_s, sin_s = _rope_tables(jnp.full((DB,), n_past))
    qf_s, kf_s, qi_s, kif_s, kib_s = _dsa_prep(
        zs.reshape(1, DB, NZ), cos_s, sin_s, qn_row, kn_row, tm=DB, transposed=False, kc=ATT_KC)
    v_s = zs[:, C_AV:C_AV + ATT_WIDTH]
    wi_s = zs[:, C_SMALL + SM_IW:C_SMALL + SM_IW + IDX_HEADS] * (IDX_HEADS ** -0.5 * IDX_DH ** -0.5)
    qi3 = qi_s.reshape(DB, IDX_HEADS, IDX_DH)
    w3 = wi_s.reshape(DB, IDX_HEADS, 1)
    pt_flat = page_table.reshape(-1).astype(i32)
    sc = _sample_scores(pt_flat, qi3, w3, cache_idx_k[li])
    ktop = min(TOPK_MAX, (n_past + S_new) // 4)
    idx = _sample_select(sc.reshape(DB, n_pages, PAGE_SIZE), qi3, kib_s.reshape(DB, 1, IDX_DH), w3,
                         n_past, ktop)
    ob_s = _sample_attend(idx.reshape(-1), pt_flat, idx.reshape(DB, 1, ktop),
                          qf_s.reshape(DB, 1, ATT_WIDTH), kf_s.reshape(DB, 1, ATT_WIDTH),
                          v_s.reshape(DB, 1, ATT_WIDTH), zs[:, C_AZ:C_AZ + ATT_WIDTH].reshape(DB, 1, ATT_WIDTH),
                          cache_k[li].reshape(n_pool * PAGE_SIZE, ATT_WIDTH),
                          cache_v[li].reshape(n_pool * PAGE_SIZE, ATT_WIDTH), n_past, ktop)
    y_s = _tail(oa_s.reshape(DB, GDN_WIDTH), ob_s.reshape(DB, ATT_WIDTH), zs, xs2,
                p_sample[li].reshape(DB, PLE_DIM), wa, wb, wo, pn_row, wg, wp, tm=DB)

    shp = (B, T, ATT_HEADS, ATT_DH)
    shs = (DB, S_new, ATT_HEADS, ATT_DH)
    return (y_p.reshape(B, T, D_MODEL), y_s.reshape(DB, S_new, D_MODEL),
            kf_p.reshape(shp)[None], v_p.reshape(shp)[None], kif_p[None],
            s_p[None], conv_p[None],
            kf_s.reshape(shs)[None], v_s.reshape(shs)[None], kif_s.reshape(DB, S_new, IDX_DH)[None],
            s_s[None], conv_s[None])
```

```python
import functools
import math

import jax
import jax.numpy as jnp
from jax import lax
from jax.experimental import pallas as pl
from jax.experimental.pallas import tpu as pltpu

f32 = jnp.float32
bf16 = jnp.bfloat16
i32 = jnp.int32

D_MODEL = 2048
PAGE_SIZE = 128
GDN_HEADS = 8
GDN_DK = 128
GDN_DV = 128
GDN_WIDTH = GDN_HEADS * GDN_DV
CONV_W = 4
CONV_CH = 2 * GDN_HEADS * GDN_DK + GDN_WIDTH
GDN_CHUNK = 64
ATT_HEADS = 8
ATT_DH = 128
ATT_WIDTH = ATT_HEADS * ATT_DH
IDX_HEADS = 16
IDX_DH = 128
TOPK_MAX = 256
QUERY_BLOCK = 128
ROPE_THETA = 10000.0
PLE_DIM = 256
EPS = 1e-6

LANES = 128
VMEM_LIMIT = 56 * 1024 * 1024
NEG_BIG = -1e30
INT_MIN = -(2 ** 31)

HK = GDN_HEADS * GDN_DK
C_GQ, C_GK, C_GV, C_GZ = 0, HK, 2 * HK, 3 * HK
C_AQ = 4 * HK
C_AK = C_AQ + ATT_WIDTH
C_AV = C_AK + ATT_WIDTH
C_AZ = C_AV + ATT_WIDTH
C_IQ = C_AZ + ATT_WIDTH
C_MGA = C_IQ + IDX_HEADS * IDX_DH
C_MGB = C_MGA + D_MODEL
C_IK = C_MGB + D_MODEL
C_SMALL = C_IK + IDX_DH
NZ = C_SMALL + LANES
SM_GA, SM_GB, SM_IW = 0, GDN_HEADS, 2 * GDN_HEADS


def _cparams(sem):
    return pltpu.CompilerParams(dimension_semantics=sem, vmem_limit_bytes=VMEM_LIMIT)


def _dot(a, b):
    return jnp.dot(a, b, preferred_element_type=f32)


def _dot_nt(a, b):
    return lax.dot_general(a, b, (((1,), (1,)), ((), ())), preferred_element_type=f32)


def _sigmoid(x):
    return 1.0 / (1.0 + jnp.exp(-x))


def _silu(x):
    return x * _sigmoid(x)


def _inproj_kernel(x_ref, g_ref, w_ref, o_ref, h_ref):
    @pl.when(pl.program_id(1) == 0)
    def _():
        x = x_ref[...]
        ms = jnp.mean(x * x, axis=-1, keepdims=True)
        h_ref[...] = (x * lax.rsqrt(ms + EPS) * g_ref[...]).astype(bf16)

    o_ref[...] = _dot(h_ref[...], w_ref[...])


def _inproj(x2d, gain_row, w_bf16, tm, tn):
    m, k = x2d.shape
    n = w_bf16.shape[1]
    return pl.pallas_call(
        _inproj_kernel,
        grid=(m // tm, n // tn),
        in_specs=[
            pl.BlockSpec((tm, k), lambda i, j: (i, 0)),
            pl.BlockSpec((1, k), lambda i, j: (0, 0)),
            pl.BlockSpec((k, tn), lambda i, j: (0, j)),
        ],
        out_specs=pl.BlockSpec((tm, tn), lambda i, j: (i, j)),
        out_shape=jax.ShapeDtypeStruct((m, n), f32),
        scratch_shapes=[pltpu.VMEM((tm, k), bf16)],
        compiler_params=_cparams(("parallel", "arbitrary")),
    )(x2d, gain_row, w_bf16)


def _softplus(x):
    return jnp.maximum(x, 0.0) + jnp.log1p(jnp.exp(-jnp.abs(x)))


def _gdn_heads_out(conv_fn, q_ref, k_ref, v_ref):
    for c in range(3 * GDN_HEADS):
        a = _silu(conv_fn(c))
        h = c % GDN_HEADS
        sl = slice(h * LANES, (h + 1) * LANES)
        if c < 2 * GDN_HEADS:
            nrm = lax.rsqrt(jnp.sum(a * a, axis=-1, keepdims=True) + EPS)
            if c < GDN_HEADS:
                q_ref[:, sl] = a * nrm * (GDN_DK ** -0.5)
            else:
                k_ref[:, sl] = a * nrm
        else:
            v_ref[:, sl] = a


def _gdn_prep_kernel(x_ref, halo_ref, prev_ref, cw_ref, sm_ref, a_ref, dt_ref,
                     q_ref, k_ref, v_ref, bg_ref, xe_ref, *, tm):
    i = pl.program_id(1)
    xe_ref[pl.ds(8, tm), :] = x_ref[0]

    @pl.when(i == 0)
    def _():
        xe_ref[pl.ds(0, 8), :] = prev_ref[0]

    @pl.when(i > 0)
    def _():
        xe_ref[pl.ds(0, 8), :] = halo_ref[0]

    def conv_fn(c):
        sl = slice(c * LANES, (c + 1) * LANES)
        acc = xe_ref[pl.ds(8 - (CONV_W - 1), tm), sl] * cw_ref[0:1, sl]
        for j in range(1, CONV_W):
            acc = acc + xe_ref[pl.ds(8 - (CONV_W - 1) + j, tm), sl] * cw_ref[j:j + 1, sl]
        return acc

    _gdn_heads_out(conv_fn, q_ref.at[0], k_ref.at[0], v_ref.at[0])

    sm = sm_ref[0]
    g = -jnp.exp(a_ref[...]) * _softplus(sm + dt_ref[...])
    row = lax.broadcasted_iota(i32, (tm, LANES), 0) % GDN_CHUNK
    s = 1
    while s < GDN_CHUNK:
        g = g + jnp.where(row >= s, pltpu.roll(g, s, axis=0), 0.0)
        s *= 2
    lane = lax.broadcasted_iota(i32, (tm, LANES), 1)
    bg_ref[0] = jnp.where(lane < GDN_HEADS, g, _sigmoid(sm))


def _gdn_prep(z3, prev8, conv_w, a_row, dt_row, tm):
    b, t, _ = z3.shape
    nblk = tm // 8
    kern = functools.partial(_gdn_prep_kernel, tm=tm)
    out_sd = jax.ShapeDtypeStruct((b, t, HK), f32)
    return pl.pallas_call(
        kern,
        grid=(b, t // tm),
        in_specs=[
            pl.BlockSpec((1, tm, CONV_CH), lambda bi, i: (bi, i, 0)),
            pl.BlockSpec((1, 8, CONV_CH), lambda bi, i: (bi, jnp.maximum(i * nblk - 1, 0), 0)),
            pl.BlockSpec((1, 8, CONV_CH), lambda bi, i: (bi, 0, 0)),
            pl.BlockSpec((CONV_W, CONV_CH), lambda bi, i: (0, 0)),
            pl.BlockSpec((1, tm, LANES), lambda bi, i: (bi, i, C_SMALL // LANES)),
            pl.BlockSpec((1, LANES), lambda bi, i: (0, 0)),
            pl.BlockSpec((1, LANES), lambda bi, i: (0, 0)),
        ],
        out_specs=[
            pl.BlockSpec((1, tm, HK), lambda bi, i: (bi, i, 0)),
            pl.BlockSpec((1, tm, HK), lambda bi, i: (bi, i, 0)),
            pl.BlockSpec((1, tm, HK), lambda bi, i: (bi, i, 0)),
            pl.BlockSpec((1, tm, LANES), lambda bi, i: (bi, i, 0)),
        ],
        out_shape=[out_sd, out_sd, out_sd, jax.ShapeDtypeStruct((b, t, LANES), f32)],
        scratch_shapes=[pltpu.VMEM((tm + 8, CONV_CH), f32)],
        compiler_params=_cparams(("parallel", "arbitrary")),
    )(z3, z3, prev8, conv_w, z3, a_row, dt_row)


def _gdn_prep_sample_kernel(x_ref, s0_ref, s1_ref, s2_ref, cw_ref, sm_ref, a_ref, dt_ref,
                            q_ref, k_ref, v_ref, bg_ref):
    def conv_fn(c):
        sl = slice(c * LANES, (c + 1) * LANES)
        return (s0_ref[:, sl] * cw_ref[0:1, sl] + s1_ref[:, sl] * cw_ref[1:2, sl]
                + s2_ref[:, sl] * cw_ref[2:3, sl] + x_ref[:, sl] * cw_ref[3:4, sl])

    _gdn_heads_out(conv_fn, q_ref, k_ref, v_ref)
    sm = sm_ref[...]
    g = -jnp.exp(a_ref[...]) * _softplus(sm + dt_ref[...])
    lane = lax.broadcasted_iota(i32, sm.shape, 1)
    bg_ref[...] = jnp.where(lane < GDN_HEADS, g, _sigmoid(sm))


def _gdn_prep_sample(zs, sc0, sc1, sc2, conv_w, a_row, dt_row):
    db = zs.shape[0]
    full = lambda shape: pl.BlockSpec(shape, lambda i: (0,) * len(shape))
    out_sd = jax.ShapeDtypeStruct((db, HK), f32)
    return pl.pallas_call(
        _gdn_prep_sample_kernel,
        grid=(1,),
        in_specs=[
            pl.BlockSpec((db, CONV_CH), lambda i: (0, 0)),
            full((db, CONV_CH)), full((db, CONV_CH)), full((db, CONV_CH)),
            full((CONV_W, CONV_CH)),
            pl.BlockSpec((db, LANES), lambda i: (0, C_SMALL // LANES)),
            full((1, LANES)), full((1, LANES)),
        ],
        out_specs=[full((db, HK)), full((db, HK)), full((db, HK)), full((db, LANES))],
        out_shape=[out_sd, out_sd, out_sd, jax.ShapeDtypeStruct((db, LANES), f32)],
        compiler_params=_cparams(("arbitrary",)),
    )(zs, sc0, sc1, sc2, conv_w, zs, a_row, dt_row)


GDN_TS = 512
GDN_HB = 2


def _bmm(a, b):
    return jnp.einsum('bij,bjk->bik', a, b, preferred_element_type=f32)


def _bmm_nt(a, b):
    return jnp.einsum('bid,bjd->bij', a, b, preferred_element_type=f32)


def _gdn_scan_kernel(q_ref, k_ref, v_ref, gz_ref, gc_ref, bt_ref, gr_ref, s0_ref, nrm_ref,
                     o_ref, so_ref, s_ref):
    step = pl.program_id(2)
    nc = GDN_TS // GDN_CHUNK
    c = GDN_CHUNK

    @pl.when(step == 0)
    def _():
        s_ref[...] = s0_ref[0]

    ri = lax.broadcasted_iota(i32, (1, c, c), 1)
    ci = lax.broadcasted_iota(i32, (1, c, c), 2)
    tri = ri >= ci
    strict = ri > ci
    eye = (ri == ci).astype(f32)

    pre = []
    for hh in range(GDN_HB):
        sl = slice(hh * LANES, (hh + 1) * LANES)
        q = q_ref[0, :, sl].reshape(nc, c, LANES)
        k = k_ref[0, :, sl].reshape(nc, c, LANES)
        v = v_ref[0, :, sl].reshape(nc, c, LANES)
        gc = gc_ref[0, hh].reshape(nc, c, 1)
        bt = bt_ref[0, hh].reshape(nc, c, 1)
        gr = gr_ref[0, hh]
        glast = gr[:, :, c - 1:c]
        decay = jnp.exp(jnp.where(tri, gc - gr, -jnp.inf))
        kb = k * bt
        m = jnp.where(strict, _bmm_nt(kb, k) * decay, 0.0)
        x = eye - m
        p = m
        for _ in range(5):
            p = _bmm(p, p)
            x = x + _bmm(x, p)
        eg = jnp.exp(gc)
        u = _bmm(x, v * bt)
        w = _bmm(x, kb * eg)
        qk = _bmm_nt(q, k) * decay
        qg = q * eg
        kg = k * jnp.exp(glast - gc)
        gl = jnp.exp(glast)
        pre.append((u, w, qk, qg, kg, gl))

    gain = nrm_ref[...]
    for ch in range(nc):
        for hh in range(GDN_HB):
            u, w, qk, qg, kg, gl = pre[hh]
            s = s_ref[hh]
            v_new = u[ch] - _dot(w[ch], s)
            o = _dot(qg[ch], s) + _dot(qk[ch], v_new)
            s_ref[hh] = s * gl[ch] + lax.dot_general(
                kg[ch], v_new, (((0,), (0,)), ((), ())), preferred_element_type=f32)
            ms = jnp.mean(o * o, axis=-1, keepdims=True)
            on = o * lax.rsqrt(ms + EPS) * gain
            rows = pl.ds(ch * c, c)
            sl = slice(hh * LANES, (hh + 1) * LANES)
            o_ref[0, rows, sl] = (on * _silu(gz_ref[0, rows, sl])).astype(o_ref.dtype)

    @pl.when(step == pl.num_programs(2) - 1)
    def _():
        so_ref[0] = s_ref[...]


def _gdn_scan(q, k, v, z3, gc_col, bt_col, gc_row, s0, gdn_norm_row):
    b, t, _ = q.shape
    hw = GDN_HB * LANES
    nc = GDN_TS // GDN_CHUNK
    qspec = pl.BlockSpec((1, GDN_TS, hw), lambda bi, hg, i: (bi, i, hg))
    return pl.pallas_call(
        _gdn_scan_kernel,
        grid=(b, GDN_HEADS // GDN_HB, t // GDN_TS),
        in_specs=[
            qspec, qspec, qspec,
            pl.BlockSpec((1, GDN_TS, hw), lambda bi, hg, i: (bi, i, C_GZ // hw + hg)),
            pl.BlockSpec((1, GDN_HB, GDN_TS, 1), lambda bi, hg, i: (bi, hg, i, 0)),
            pl.BlockSpec((1, GDN_HB, GDN_TS, 1), lambda bi, hg, i: (bi, hg, i, 0)),
            pl.BlockSpec((1, GDN_HB, nc, 1, GDN_CHUNK), lambda bi, hg, i: (bi, hg, i, 0, 0)),
            pl.BlockSpec((1, GDN_HB, GDN_DK, GDN_DV), lambda bi, hg, i: (bi, hg, 0, 0)),
            pl.BlockSpec((1, LANES), lambda bi, hg, i: (0, 0)),
        ],
        out_specs=[
            pl.BlockSpec((1, GDN_TS, hw), lambda bi, hg, i: (bi, i, hg)),
            pl.BlockSpec((1, GDN_HB, GDN_DK, GDN_DV), lambda bi, hg, i: (bi, hg, 0, 0)),
        ],
        out_shape=[jax.ShapeDtypeStruct((b, t, GDN_WIDTH), bf16),
                   jax.ShapeDtypeStruct((b, GDN_HEADS, GDN_DK, GDN_DV), f32)],
        scratch_shapes=[pltpu.VMEM((GDN_HB, GDN_DK, GDN_DV), f32)],
        compiler_params=_cparams(("parallel", "parallel", "arbitrary")),
    )(q, k, v, z3, gc_col, bt_col, gc_row, s0, gdn_norm_row)


def _gdn_step_kernel(q_ref, k_ref, v_ref, gz_ref, g_ref, bt_ref, s0_ref, nrm_ref, o_ref, so_ref):
    for h in range(GDN_HEADS):
        s0 = s0_ref[0, h]
        qc = q_ref[0, h]
        kc = k_ref[0, h]
        vr = v_ref[0, h]
        eg = jnp.exp(g_ref[0, h])
        bt = bt_ref[0, h]
        ks = jnp.sum(kc * s0, axis=0, keepdims=True)
        qs = jnp.sum(qc * s0, axis=0, keepdims=True)
        qk = jnp.sum(qc * kc, axis=0, keepdims=True)
        v_new = bt * vr - (bt * eg) * ks
        o = eg * qs + qk * v_new
        so_ref[0, h] = s0 * eg + kc * v_new
        ms = jnp.mean(o * o, axis=-1, keepdims=True)
        on = o * lax.rsqrt(ms + EPS) * nrm_ref[...]
        o_ref[0, h] = (on * _silu(gz_ref[0, h])).astype(o_ref.dtype)


def _gdn_step(q_col, k_col, v_row, gz_row, g11, bt11, s0, gdn_norm_row):
    db = q_col.shape[0]
    col = pl.BlockSpec((1, GDN_HEADS, GDN_DK, 1), lambda i: (i, 0, 0, 0))
    row = pl.BlockSpec((1, GDN_HEADS, 1, GDN_DV), lambda i: (i, 0, 0, 0))
    one = pl.BlockSpec((1, GDN_HEADS, 1, 1), lambda i: (i, 0, 0, 0))
    st = pl.BlockSpec((1, GDN_HEADS, GDN_DK, GDN_DV), lambda i: (i, 0, 0, 0))
    return pl.pallas_call(
        _gdn_step_kernel,
        grid=(db,),
        in_specs=[col, col, row, row, one, one, st, pl.BlockSpec((1, LANES), lambda i: (0, 0))],
        out_specs=[row, st],
        out_shape=[jax.ShapeDtypeStruct((db, GDN_HEADS, 1, GDN_DV), bf16),
                   jax.ShapeDtypeStruct((db, GDN_HEADS, GDN_DK, GDN_DV), f32)],
        compiler_params=_cparams(("parallel",)),
    )(q_col, k_col, v_row, gz_row, g11, bt11, s0, gdn_norm_row)


def _rope(x, cos, sin_signed):
    return x * cos + pltpu.roll(x, LANES // 2, axis=1) * sin_signed


def _dsa_prep_kernel(aq_ref, ak_ref, av_ref, iq_ref, ik_ref, sm_ref, cos_ref, sin_ref,
                     qn_ref, kn_ref, *out_refs, transposed, q_scale):
    if transposed:
        qb_ref, kf_ref, vf_ref, kb_ref, vt_ref, qi_ref, kif_ref, kib_ref, wt_ref = out_refs
    else:
        qb_ref, kf_ref, vf_ref, qi_ref, kif_ref, kib_ref = out_refs
    cos = cos_ref[...]
    sin = sin_ref[...]
    for h in range(ATT_HEADS):
        sl = slice(h * LANES, (h + 1) * LANES)
        a = aq_ref[0, :, sl]
        a = a * lax.rsqrt(jnp.mean(a * a, axis=-1, keepdims=True) + EPS) * qn_ref[...]
        qb_ref[0, :, sl] = (_rope(a, cos, sin) * q_scale).astype(qb_ref.dtype)
        a = ak_ref[0, :, sl]
        a = a * lax.rsqrt(jnp.mean(a * a, axis=-1, keepdims=True) + EPS) * kn_ref[...]
        kr = _rope(a, cos, sin)
        kf_ref[0, :, sl] = kr
        v = av_ref[0, :, sl]
        vf_ref[0, :, sl] = v
        if transposed:
            kb_ref[0, :, sl] = kr.astype(bf16)
            vt_ref[0, 0, sl, :] = v.T.astype(bf16)
    for h in range(IDX_HEADS):
        sl = slice(h * LANES, (h + 1) * LANES)
        qi_ref[0, :, sl] = _rope(iq_ref[0, :, sl], cos, sin).astype(bf16)
    ki = _rope(ik_ref[0], cos, sin)
    kif_ref[0] = ki
    kib_ref[0] = ki.astype(bf16)
    if transposed:
        wt_ref[0] = (sm_ref[0] * (IDX_HEADS ** -0.5 * IDX_DH ** -0.5)).T


def _dsa_prep(z3, cos, sin, qn_row, kn_row, tm, transposed, kc, q_scale):
    b, t, _ = z3.shape
    zspec = lambda w, off: pl.BlockSpec((1, tm, w), lambda bi, i: (bi, i, off // w))
    tok = lambda w, dt: (pl.BlockSpec((1, tm, w), lambda bi, i: (bi, i, 0)),
                         jax.ShapeDtypeStruct((b, t, w), dt))
    if transposed:
        per = kc // tm
        outs = [tok(ATT_WIDTH, bf16), tok(ATT_WIDTH, f32), tok(ATT_WIDTH, f32), tok(ATT_WIDTH, bf16),
                (pl.BlockSpec((1, 1, ATT_WIDTH, tm), lambda bi, i: (bi, i // per, 0, i % per)),
                 jax.ShapeDtypeStruct((b, t // kc, ATT_WIDTH, kc), bf16)),
                tok(IDX_HEADS * IDX_DH, bf16), tok(IDX_DH, f32), tok(IDX_DH, bf16),
                (pl.BlockSpec((1, LANES, tm), lambda bi, i: (bi, 0, i)),
                 jax.ShapeDtypeStruct((b, LANES, t), f32))]
    else:
        outs = [tok(ATT_WIDTH, f32), tok(ATT_WIDTH, f32), tok(ATT_WIDTH, f32),
                tok(IDX_HEADS * IDX_DH, bf16), tok(IDX_DH, f32), tok(IDX_DH, bf16)]
    kern = functools.partial(_dsa_prep_kernel, transposed=transposed, q_scale=q_scale)
    return pl.pallas_call(
        kern,
        grid=(b, t // tm),
        in_specs=[
            zspec(ATT_WIDTH, C_AQ), zspec(ATT_WIDTH, C_AK), zspec(ATT_WIDTH, C_AV),
            zspec(IDX_HEADS * IDX_DH, C_IQ), zspec(IDX_DH, C_IK), zspec(LANES, C_SMALL),
            pl.BlockSpec((tm, LANES), lambda bi, i: (i, 0)),
            pl.BlockSpec((tm, LANES), lambda bi, i: (i, 0)),
            pl.BlockSpec((1, LANES), lambda bi, i: (0, 0)),
            pl.BlockSpec((1, LANES), lambda bi, i: (0, 0)),
        ],
        out_specs=[o[0] for o in outs],
        out_shape=[o[1] for o in outs],
        compiler_params=_cparams(("parallel", "parallel")),
    )(z3, z3, z3, z3, z3, z3, cos, sin, qn_row, kn_row)


def _key_to_float(key):
    bits = jnp.where(key >= 0, key, key ^ jnp.int32(0x7FFFFFFF))
    return pltpu.bitcast(bits, f32)


def _kth_largest(count_ge, k, shape):
    kf = jnp.float32(k)
    zero_ok = count_ge(jnp.zeros(shape, f32)) >= kf
    cur = jnp.where(zero_ok, jnp.int32(0), jnp.int32(INT_MIN))

    def body(it, cur):
        cand = cur + jnp.left_shift(jnp.int32(1), jnp.int32(30) - it)
        ok = count_ge(_key_to_float(cand)) >= kf
        return jnp.where(ok, cand, cur)

    cur = lax.fori_loop(0, 31, body, cur)
    return _key_to_float(cur)


def _tie_bound(count_eq_le, need, nbits, shape):
    lo = jnp.full(shape, -1, i32)

    def body(it, lo):
        cand = lo + jnp.left_shift(jnp.int32(1), jnp.int32(nbits - 1) - it)
        short = count_eq_le(cand) < need
        return jnp.where(short, cand, lo)

    lo = lax.fori_loop(0, nbits, body, lo)
    return lo + 1


ATT_KC = 512
ATT_QB = 256
ATT_KS = 128
CNT_ROWS = 64
LOG2E = 1.4426950408889634


def _dsa_attend_kernel(qi_ref, wt_ref, ki_ref, qb_ref, kb_ref, vt_ref, az_ref, o_ref,
                       sc_ref, *head_refs, t_total):
    i = pl.program_id(1)
    qb = ATT_QB
    kc = ATT_KC
    nch = (i * qb + qb + kc - 1) // kc
    tq = i * qb + lax.broadcasted_iota(i32, (1, qb), 1)
    ktop = min(TOPK_MAX, t_total // 4)

    def spos(c):
        return c * kc + lax.broadcasted_iota(i32, (kc, 1), 0)

    def idx_body(c, carry):
        for sub in range(kc // ATT_KS):
            base = c * kc + sub * ATT_KS
            ksub = ki_ref[0, pl.ds(base, ATT_KS), :]
            acc = jnp.zeros((ATT_KS, qb), f32)
            for h in range(IDX_HEADS):
                d = _dot_nt(ksub, qi_ref[0, :, h * LANES:(h + 1) * LANES])
                acc = acc + jnp.maximum(d, 0.0) * wt_ref[0, SM_IW + h:SM_IW + h + 1, :]
            sp = base + lax.broadcasted_iota(i32, (ATT_KS, 1), 0)
            sc_ref[pl.ds(base, ATT_KS), :] = jnp.where(sp <= tq, acc, -jnp.inf)
        return carry

    lax.fori_loop(0, nch, idx_body, 0)

    def col_count(pred_fn):
        def body(c, acc):
            m = pred_fn(sc_ref[pl.ds(c * kc, kc), :], c)
            return acc + jnp.sum(jnp.where(m, 1.0, 0.0).reshape(kc // CNT_ROWS, CNT_ROWS, qb), axis=0)
        acc = lax.fori_loop(0, nch, body, jnp.zeros((CNT_ROWS, qb), f32))
        return jnp.sum(acc, axis=0, keepdims=True)

    def search():
        thr = _kth_largest(lambda cand: col_count(lambda s, c: s >= cand), ktop, (1, qb))
        c_ge = col_count(lambda s, c: s >= thr)
        c_gt = col_count(lambda s, c: s > thr)
        need = jnp.float32(ktop) - c_gt
        jb = lax.cond(
            jnp.max(c_ge) > jnp.float32(ktop),
            lambda: _tie_bound(
                lambda cand: col_count(lambda s, c: (s == thr) & (spos(c) <= cand)),
                need, max(1, (t_total - 1).bit_length()), (1, qb)),
            lambda: jnp.full((1, qb), t_total, i32))
        return thr, jb

    def take_all():
        return jnp.full((1, qb), -jnp.inf, f32), jnp.full((1, qb), t_total, i32)

    thr, jb = lax.cond((i + 1) * qb <= ktop, take_all, search)

    def bias_body(c, carry):
        s = sc_ref[pl.ds(c * kc, kc), :]
        sp = spos(c)
        sel = ((s > thr) | ((s == thr) & (sp <= jb))) & (sp <= tq)
        sc_ref[pl.ds(c * kc, kc), :] = jnp.where(sel, 0.0, NEG_BIG)
        return carry

    lax.fori_loop(0, nch, bias_body, 0)

    nh = ATT_HEADS
    m_refs, l_refs, al_refs, acc_refs, s_refs, p_refs = (head_refs[j * nh:(j + 1) * nh] for j in range(6))
    for h in range(nh):
        m_refs[h][...] = jnp.full(m_refs[h].shape, NEG_BIG, f32)
        l_refs[h][...] = jnp.zeros(l_refs[h].shape, f32)
        acc_refs[h][...] = jnp.zeros(acc_refs[h].shape, f32)

    def att_body(c, carry):
        for h in range(nh):
            sl = slice(h * LANES, (h + 1) * LANES)
            s_refs[h][...] = _dot_nt(kb_ref[0, pl.ds(c * kc, kc), sl], qb_ref[0, :, sl])
        for h in range(nh):
            s = s_refs[h][...] + sc_ref[pl.ds(c * kc, kc), :]
            m = m_refs[h][...]
            m_new = jnp.maximum(m, jnp.max(s, axis=0, keepdims=True))
            alpha = jnp.exp2(m - m_new)
            p = jnp.exp2(s - m_new)
            m_refs[h][...] = m_new
            al_refs[h][...] = alpha
            l_refs[h][...] = alpha * l_refs[h][...] + jnp.sum(p, axis=0, keepdims=True)
            p_refs[h][...] = p.astype(bf16)
        for h in range(nh):
            sl = slice(h * LANES, (h + 1) * LANES)
            acc_refs[h][...] = (al_refs[h][...] * acc_refs[h][...]
                                + _dot(vt_ref[0, c, sl, :], p_refs[h][...]))
        return carry

    lax.fori_loop(0, nch, att_body, 0)

    for h in range(ATT_HEADS):
        sl = slice(h * LANES, (h + 1) * LANES)
        out = (acc_refs[h][...] / l_refs[h][...]).T
        o_ref[0, :, sl] = (out * _silu(az_ref[0, :, sl])).astype(o_ref.dtype)


def _dsa_attend(qi, wt, ki, qb, kb, vt, z3):
    b, t, _ = qb.shape
    kern = functools.partial(_dsa_attend_kernel, t_total=t)
    return pl.pallas_call(
        kern,
        grid=(b, t // ATT_QB),
        in_specs=[
            pl.BlockSpec((1, ATT_QB, IDX_HEADS * IDX_DH), lambda bi, i: (bi, i, 0)),
            pl.BlockSpec((1, LANES, ATT_QB), lambda bi, i: (bi, 0, i)),
            pl.BlockSpec((1, t, IDX_DH), lambda bi, i: (bi, 0, 0)),
            pl.BlockSpec((1, ATT_QB, ATT_WIDTH), lambda bi, i: (bi, i, 0)),
            pl.BlockSpec((1, t, ATT_WIDTH), lambda bi, i: (bi, 0, 0)),
            pl.BlockSpec((1, t // ATT_KC, ATT_WIDTH, ATT_KC), lambda bi, i: (bi, 0, 0, 0)),
            pl.BlockSpec((1, ATT_QB, ATT_WIDTH), lambda bi, i: (bi, i, C_AZ // ATT_WIDTH)),
        ],
        out_specs=pl.BlockSpec((1, ATT_QB, ATT_WIDTH), lambda bi, i: (bi, i, 0)),
        out_shape=jax.ShapeDtypeStruct((b, t, ATT_WIDTH), bf16),
        scratch_shapes=([pltpu.VMEM((t, ATT_QB), f32)]
                        + [pltpu.VMEM((1, ATT_QB), f32) for _ in range(3 * ATT_HEADS)]
                        + [pltpu.VMEM((LANES, ATT_QB), f32) for _ in range(ATT_HEADS)]
                        + [pltpu.VMEM((ATT_KC, ATT_QB), f32) for _ in range(ATT_HEADS)]
                        + [pltpu.VMEM((ATT_KC, ATT_QB), bf16) for _ in range(ATT_HEADS)]),
        compiler_params=_cparams(("parallel", "arbitrary")),
    )(qi, wt, ki, qb, kb, vt, z3)


SC_PG = 16
DMA_UNROLL = 8


def _sample_scores_kernel(pt_s, qi_ref, w_ref, cik_ref, o_ref, buf, sem, *, n_pages):
    b = pl.program_id(0)
    nb = pl.num_programs(0)

    def page_copy(bb, slot, p):
        return pltpu.make_async_copy(cik_ref.at[pt_s[bb * n_pages + p]], buf.at[slot, p], sem.at[slot])

    def issue(bb, slot):
        def body(p, carry):
            page_copy(bb, slot, p).start()
            return carry
        lax.fori_loop(0, n_pages, body, 0, unroll=DMA_UNROLL)

    def wait_all(bb, slot):
        def body(p, carry):
            page_copy(bb, slot, p).wait()
            return carry
        lax.fori_loop(0, n_pages, body, 0, unroll=DMA_UNROLL)

    slot = b % 2

    @pl.when(b == 0)
    def _():
        issue(b, slot)

    @pl.when(b + 1 < nb)
    def _():
        issue(b + 1, 1 - slot)

    wait_all(b, slot)

    qi = qi_ref[0]
    w = w_ref[0]
    gk = SC_PG * PAGE_SIZE
    for g in range(n_pages // SC_PG):
        keys = buf[slot, g * SC_PG:(g + 1) * SC_PG].reshape(gk, IDX_DH).astype(bf16)
        d = _dot_nt(qi, keys)
        o_ref[0, :, g * gk:(g + 1) * gk] = jnp.sum(jnp.maximum(d, 0.0) * w, axis=0, keepdims=True)


def _sample_scores(pt_flat, qi3, w3, cache_ik):
    db = qi3.shape[0]
    n_pages = pt_flat.shape[0] // db
    kern = functools.partial(_sample_scores_kernel, n_pages=n_pages)
    grid_spec = pltpu.PrefetchScalarGridSpec(
        num_scalar_prefetch=1,
        grid=(db,),
        in_specs=[pl.BlockSpec((1, IDX_HEADS, IDX_DH), lambda bi, pt: (bi, 0, 0)),
                  pl.BlockSpec((1, IDX_HEADS, 1), lambda bi, pt: (bi, 0, 0)),
                  pl.BlockSpec(memory_space=pl.ANY)],
        out_specs=pl.BlockSpec((1, 1, n_pages * PAGE_SIZE), lambda bi, pt: (bi, 0, 0)),
        scratch_shapes=[pltpu.VMEM((2, n_pages, PAGE_SIZE, IDX_DH), f32), pltpu.SemaphoreType.DMA((2,))],
    )
    return pl.pallas_call(
        kern,
        grid_spec=grid_spec,
        out_shape=jax.ShapeDtypeStruct((db, 1, n_pages * PAGE_SIZE), f32),
        compiler_params=_cparams(("arbitrary",)),
    )(pt_flat, qi3, w3, cache_ik)


def _sample_select_kernel(sc_ref, qi_ref, ki_ref, w_ref, pt_ref, idx_ref, rows_ref, *, n_past, ktop):
    db, n_pages, ps = sc_ref.shape
    sc = sc_ref[...]

    def red(x):
        return jnp.sum(jnp.sum(x, axis=2, keepdims=True), axis=1, keepdims=True)

    dn = jnp.sum(qi_ref[...].astype(f32) * ki_ref[...].astype(f32), axis=2, keepdims=True)
    s_new = jnp.sum(jnp.maximum(dn, 0.0) * w_ref[...], axis=1, keepdims=True)

    def count(pred):
        return red(pred(sc).astype(f32)) + pred(s_new).astype(f32)

    shape = (db, 1, 1)
    thr = _kth_largest(lambda cand: count(lambda s: s >= cand), ktop, shape)
    c_ge = count(lambda s: s >= thr)
    c_gt = count(lambda s: s > thr)
    need = jnp.float32(ktop) - c_gt
    pos = (lax.broadcasted_iota(i32, (1, n_pages, ps), 1) * ps
           + lax.broadcasted_iota(i32, (1, n_pages, ps), 2))

    def eq_le(cand):
        return (red(((sc == thr) & (pos <= cand)).astype(f32))
                + ((s_new == thr) & (n_past <= cand)).astype(f32))

    jb = lax.cond(jnp.max(c_ge) > jnp.float32(ktop),
                  lambda: _tie_bound(eq_le, need, (n_past + 1).bit_length(), shape),
                  lambda: jnp.full(shape, n_past + 1, i32))
    sel = (sc > thr) | ((sc == thr) & (pos <= jb))

    r_i = lax.broadcasted_iota(i32, (ps, ps), 0)
    c_i = lax.broadcasted_iota(i32, (ps, ps), 1)
    upper = (r_i <= c_i).astype(bf16)
    upper_pg = (lax.broadcasted_iota(i32, (n_pages, n_pages), 0)
                <= lax.broadcasted_iota(i32, (n_pages, n_pages), 1)).astype(bf16)
    jcol = lax.broadcasted_iota(i32, (ktop, 1), 0).astype(f32)
    plane = lax.broadcasted_iota(i32, (1, n_pages), 1).astype(f32)
    ones8 = jnp.ones((8, ps), bf16)

    selb = jnp.where(sel, 1.0, 0.0).astype(bf16)
    for b in range(db):
        sb = selb[b]
        incl = _dot(sb, upper)
        tot_row = _dot_nt(ones8, sb)[0:1]
        cum_row = _dot(jnp.broadcast_to(tot_row, (8, n_pages)).astype(bf16), upper_pg)[0:1]
        page_of = jnp.sum((cum_row <= jcol).astype(f32), axis=1, keepdims=True)
        onehot = (page_of == plane)
        before = jnp.sum(jnp.where(onehot, cum_row - tot_row, 0.0), axis=1, keepdims=True)
        rloc = jcol - before
        incl_rows = _dot(onehot.astype(bf16), incl.astype(bf16))
        off_of = jnp.sum((incl_rows <= rloc).astype(f32), axis=1, keepdims=True)
        idx = jnp.minimum(page_of * ps + off_of, jnp.float32(n_past))
        idx_ref[b] = idx.astype(i32)
        pidx = jnp.minimum(idx, jnp.float32(n_past - 1))
        pg = jnp.floor(pidx * (1.0 / ps))
        phys = jnp.sum(jnp.where(pg == plane, pt_ref[b], 0.0), axis=1, keepdims=True)
        rows_ref[b] = (phys * ps + (pidx - pg * ps)).astype(i32)


def _sample_select(sc3, qi3, ki3, w3, pt3, n_past, ktop):
    db = sc3.shape[0]
    kern = functools.partial(_sample_select_kernel, n_past=n_past, ktop=ktop)
    full = lambda a: pl.BlockSpec(a.shape, lambda i: (0,) * a.ndim)
    out_spec = pl.BlockSpec((db, ktop, 1), lambda i: (0, 0, 0))
    out_sd = jax.ShapeDtypeStruct((db, ktop, 1), i32)
    return pl.pallas_call(
        kern,
        grid=(1,),
        in_specs=[full(sc3), full(qi3), full(ki3), full(w3), full(pt3)],
        out_specs=[out_spec, out_spec],
        out_shape=[out_sd, out_sd],
        compiler_params=_cparams(("arbitrary",)),
    )(sc3, qi3, ki3, w3, pt3)


def _sample_attend_kernel(rows_s, idxv_ref, q_ref, kn_ref, vn_ref, az_ref, ck_ref, cv_ref,
                          o_ref, kbuf, vbuf, sem, *, n_past, ktop):
    b = pl.program_id(0)
    nb = pl.num_programs(0)

    def row_copies(bb, slot, j):
        row = rows_s[bb * ktop + j]
        ck = pltpu.make_async_copy(ck_ref.at[row], kbuf.at[slot, :, j, :], sem.at[0, slot])
        cv = pltpu.make_async_copy(cv_ref.at[row], vbuf.at[slot, :, j, :], sem.at[1, slot])
        return ck, cv

    def issue(bb, slot):
        def body(j, carry):
            ck, cv = row_copies(bb, slot, j)
            ck.start()
            cv.start()
            return carry
        lax.fori_loop(0, ktop, body, 0, unroll=DMA_UNROLL)

    def wait_all(bb, slot):
        def body(j, carry):
            ck, cv = row_copies(bb, slot, j)
            ck.wait()
            cv.wait()
            return carry
        lax.fori_loop(0, ktop, body, 0, unroll=DMA_UNROLL)

    slot = b % 2

    @pl.when(b == 0)
    def _():
        issue(b, slot)

    @pl.when(b + 1 < nb)
    def _():
        issue(b + 1, 1 - slot)

    wait_all(b, slot)

    is_new = idxv_ref[0] >= n_past
    newf = is_new.astype(f32)
    for h in range(ATT_HEADS):
        sl = slice(h * LANES, (h + 1) * LANES)
        qh = q_ref[0, :, sl]
        q8 = jnp.broadcast_to(qh, (8, LANES)).astype(bf16)
        s = _dot_nt(q8, kbuf[slot, h].astype(bf16))[0:1]
        s_new = jnp.sum(qh.astype(bf16).astype(f32) * kn_ref[0, :, sl].astype(bf16).astype(f32),
                        axis=-1, keepdims=True)
        s = jnp.where(is_new, s_new, s)
        m = jnp.max(s, axis=-1, keepdims=True)
        p = jnp.exp(s - m)
        l = jnp.sum(p, axis=-1, keepdims=True)
        p_old = jnp.broadcast_to(p * (1.0 - newf), (8, ktop)).astype(bf16)
        pv = _dot(p_old, vbuf[slot, h].astype(bf16))[0:1]
        pv = pv + jnp.sum(p * newf, axis=-1, keepdims=True) * vn_ref[0, :, sl]
        o_ref[0, :, sl] = ((pv / l) * _silu(az_ref[0, :, sl])).astype(o_ref.dtype)


def _sample_attend(rows_flat, idx_row, q3, kn3, vn3, az3, ck3, cv3, n_past, ktop):
    db = q3.shape[0]
    kern = functools.partial(_sample_attend_kernel, n_past=n_past, ktop=ktop)
    tok = pl.BlockSpec((1, 1, ATT_WIDTH), lambda bi, r: (bi, 0, 0))
    grid_spec = pltpu.PrefetchScalarGridSpec(
        num_scalar_prefetch=1,
        grid=(db,),
        in_specs=[pl.BlockSpec((1, 1, ktop), lambda bi, r: (bi, 0, 0)), tok, tok, tok, tok,
                  pl.BlockSpec(memory_space=pl.ANY), pl.BlockSpec(memory_space=pl.ANY)],
        out_specs=tok,
        scratch_shapes=[pltpu.VMEM((2, ATT_HEADS, ktop, ATT_DH), f32),
                        pltpu.VMEM((2, ATT_HEADS, ktop, ATT_DH), f32),
                        pltpu.SemaphoreType.DMA((2, 2))],
    )
    return pl.pallas_call(
        kern,
        grid_spec=grid_spec,
        out_shape=jax.ShapeDtypeStruct((db, 1, ATT_WIDTH), bf16),
        compiler_params=_cparams(("arbitrary",)),
    )(rows_flat, idx_row, q3, kn3, vn3, az3, ck3, cv3)


def _tail_kernel(oa_ref, ob_ref, mga_ref, mgb_ref, x_ref, p_ref, wa_ref, wb_ref, wo_ref,
                 pn_ref, wg_ref, wp_ref, y_ref):
    a = _dot(oa_ref[...], wa_ref[...])
    b = _dot(ob_ref[...], wb_ref[...])
    merged = _sigmoid(mga_ref[...]) * a + _sigmoid(mgb_ref[...]) * b
    x2 = x_ref[...] + _dot(merged.astype(bf16), wo_ref[...])
    ms = jnp.mean(x2 * x2, axis=-1, keepdims=True)
    hn = (x2 * lax.rsqrt(ms + EPS) * pn_ref[...]).astype(bf16)
    gate = _sigmoid(_dot(hn, wg_ref[...]))
    y_ref[...] = x2 + gate * _dot(p_ref[...].astype(bf16), wp_ref[...])


def _tail(oa, ob, z2d, x2d, p2d, wa, wb, wo, pn_row, wg, wp, tm):
    m = x2d.shape[0]
    const = lambda a: pl.BlockSpec(a.shape, lambda i: (0, 0), pipeline_mode=pl.Buffered(1))
    return pl.pallas_call(
        _tail_kernel,
        grid=(m // tm,),
        in_specs=[
            pl.BlockSpec((tm, GDN_WIDTH), lambda i: (i, 0)),
            pl.BlockSpec((tm, ATT_WIDTH), lambda i: (i, 0)),
            pl.BlockSpec((tm, D_MODEL), lambda i: (i, C_MGA // D_MODEL)),
            pl.BlockSpec((tm, D_MODEL), lambda i: (i, C_MGB // D_MODEL)),
            pl.BlockSpec((tm, D_MODEL), lambda i: (i, 0)),
            pl.BlockSpec((tm, PLE_DIM), lambda i: (i, 0)),
            const(wa), const(wb), const(wo), const(pn_row), const(wg), const(wp),
        ],
        out_specs=pl.BlockSpec((tm, D_MODEL), lambda i: (i, 0)),
        out_shape=jax.ShapeDtypeStruct((m, D_MODEL), f32),
        compiler_params=_cparams(("parallel",)),
    )(oa, ob, z2d, z2d, x2d, p2d, wa, wb, wo, pn_row, wg, wp)


def _permute_w_in(w):
    sizes = (HK, HK, GDN_WIDTH, GDN_WIDTH, GDN_HEADS, GDN_HEADS, ATT_WIDTH, ATT_WIDTH, ATT_WIDTH, ATT_WIDTH,
             IDX_HEADS * IDX_DH, IDX_DH, IDX_HEADS, D_MODEL, D_MODEL)
    pieces, start = [], 0
    for n in sizes:
        pieces.append(w[:, start:start + n])
        start += n
    gq, gk, gv, gz, ga, gb, aq, ak, av, az, iq, ik, iw, mga, mgb = pieces
    pad = jnp.zeros((w.shape[0], LANES - 2 * GDN_HEADS - IDX_HEADS), w.dtype)
    return jnp.concatenate([gq, gk, gv, gz, aq, ak, av, az, iq, mga, mgb, ik, ga, gb, iw, pad], axis=1)


def _lane_row(v, offset=0):
    row = jnp.zeros((1, LANES), f32)
    return row.at[0, offset:offset + v.shape[0]].set(v.astype(f32))


def _rope_tables(pos):
    half = ATT_DH // 2
    inv = ROPE_THETA ** (-jnp.arange(half, dtype=f32) * 2.0 / ATT_DH)
    ang = pos.astype(f32)[:, None] * inv[None, :]
    cos, sin = jnp.cos(ang), jnp.sin(ang)
    return jnp.concatenate([cos, cos], axis=1), jnp.concatenate([-sin, sin], axis=1)


def kernel(x_prompt, x_sample, p_prompt, p_sample, cache_k, cache_v, cache_idx_k, state_gdn, state_conv,
           page_table, norm_in, w_in, conv_w, a_log, dt_bias, gdn_norm, q_norm, k_norm,
           w_proj_a, w_proj_b, w_out, ple_norm, w_ple_gate, w_ple_proj):
    depth = norm_in.shape[0]
    assert depth == 1, "single-layer trunk"
    B, T, _ = x_prompt.shape
    DB, S_new, _ = x_sample.shape
    assert S_new == 1
    n_pages = page_table.shape[1]
    n_past = n_pages * PAGE_SIZE
    n_pool = cache_k.shape[1]
    li = 0

    w_perm = _permute_w_in(w_in[li]).astype(bf16)
    gain_in = norm_in[li].reshape(1, D_MODEL)
    a_row = _lane_row(a_log[li], SM_GA)
    dt_row = _lane_row(dt_bias[li], SM_GA)
    gdn_norm_row = gdn_norm[li].reshape(1, LANES)
    qn_row = q_norm[li].reshape(1, LANES)
    kn_row = k_norm[li].reshape(1, LANES)
    wa = w_proj_a[li].astype(bf16)
    wb = w_proj_b[li].astype(bf16)
    wo = w_out[li].astype(bf16)
    wg = w_ple_gate[li].astype(bf16)
    wp = w_ple_proj[li].astype(bf16)
    pn_row = ple_norm[li].reshape(1, D_MODEL)
    cw = conv_w[li]

    xp2 = x_prompt.reshape(B * T, D_MODEL)
    zp = _inproj(xp2, gain_in, w_perm, tm=1024, tn=768)
    zp3 = zp.reshape(B, T, NZ)

    prev8 = jnp.zeros((B, 8, CONV_CH), f32)
    qa, ka, va, bg = _gdn_prep(zp3, prev8, cw, a_row, dt_row, tm=256)
    gcum = bg[..., 0:GDN_HEADS]
    beta = bg[..., GDN_HEADS:2 * GDN_HEADS]
    gc_col = jnp.swapaxes(gcum, 1, 2)[..., None]
    bt_col = jnp.swapaxes(beta, 1, 2)[..., None]
    gc_row = jnp.swapaxes(gcum, 1, 2).reshape(B, GDN_HEADS, T // GDN_CHUNK, 1, GDN_CHUNK)
    s00 = jnp.zeros((B, GDN_HEADS, GDN_DK, GDN_DV), f32)
    oa_p, s_p = _gdn_scan(qa, ka, va, zp3, gc_col, bt_col, gc_row, s00, gdn_norm_row)
    conv_p = zp3[:, T - (CONV_W - 1):, C_GQ:C_GQ + CONV_CH]

    cos_p, sin_p = _rope_tables(jnp.arange(T))
    qb_p, kf_p, v_p, kb_p, vt_p, qi_p, kif_p, kib_p, wt_p = _dsa_prep(
        zp3, cos_p, sin_p, qn_row, kn_row, tm=256, transposed=True, kc=ATT_KC,
        q_scale=LOG2E * ATT_DH ** -0.5)
    ob_p = _dsa_attend(qi_p, wt_p, kib_p, qb_p, kb_p, vt_p, zp3)

    y_p = _tail(oa_p.reshape(B * T, GDN_WIDTH), ob_p.reshape(B * T, ATT_WIDTH), zp, xp2,
                p_prompt[li].reshape(B * T, PLE_DIM), wa, wb, wo, pn_row, wg, wp, tm=256)

    xs2 = x_sample.reshape(DB, D_MODEL)
    zs = _inproj(xs2, gain_in, w_perm, tm=DB, tn=768)
    sconv = state_conv[li]
    qs, ks, vs, bgs = _gdn_prep_sample(zs, sconv[:, 0], sconv[:, 1], sconv[:, 2], cw, a_row, dt_row)
    g_s = bgs[:, 0:GDN_HEADS].reshape(DB, GDN_HEADS, 1, 1)
    bt_s = bgs[:, GDN_HEADS:2 * GDN_HEADS].reshape(DB, GDN_HEADS, 1, 1)
    gz_s = zs[:, C_GZ:C_GZ + GDN_WIDTH].reshape(DB, GDN_HEADS, 1, GDN_DV)
    oa_s, s_s = _gdn_step(qs.reshape(DB, GDN_HEADS, GDN_DK, 1), ks.reshape(DB, GDN_HEADS, GDN_DK, 1),
                          vs.reshape(DB, GDN_HEADS, 1, GDN_DV), gz_s, g_s, bt_s, state_gdn[li], gdn_norm_row)
    conv_s = jnp.concatenate([sconv[:, 1:], zs[:, None, C_GQ:C_GQ + CONV_CH]], axis=1)

    cos_s, sin_s = _rope_tables(jnp.full((DB,), n_past))
    qf_s, kf_s, v_s, qi_s, kif_s, kib_s = _dsa_prep(
        zs.reshape(1, DB, NZ), cos_s, sin_s, qn_row, kn_row, tm=DB, transposed=False, kc=ATT_KC,
        q_scale=ATT_DH ** -0.5)
    wi_s = zs[:, C_SMALL + SM_IW:C_SMALL + SM_IW + IDX_HEADS] * (IDX_HEADS ** -0.5 * IDX_DH ** -0.5)
    qi3 = qi_s.reshape(DB, IDX_HEADS, IDX_DH)
    w3 = wi_s.reshape(DB, IDX_HEADS, 1)
    pt_flat = page_table.reshape(-1).astype(i32)
    sc = _sample_scores(pt_flat, qi3, w3, cache_idx_k[li])
    ktop = min(TOPK_MAX, (n_past + S_new) // 4)
    idx, rows = _sample_select(sc.reshape(DB, n_pages, PAGE_SIZE), qi3, kib_s.reshape(DB, 1, IDX_DH), w3,
                               page_table.astype(f32).reshape(DB, 1, n_pages), n_past, ktop)
    ob_s = _sample_attend(rows.reshape(-1), idx.reshape(DB, 1, ktop),
                          qf_s.reshape(DB, 1, ATT_WIDTH), kf_s.reshape(DB, 1, ATT_WIDTH),
                          v_s.reshape(DB, 1, ATT_WIDTH), zs[:, C_AZ:C_AZ + ATT_WIDTH].reshape(DB, 1, ATT_WIDTH),
                          cache_k[li].reshape(n_pool * PAGE_SIZE, ATT_HEADS, ATT_DH),
                          cache_v[li].reshape(n_pool * PAGE_SIZE, ATT_HEADS, ATT_DH), n_past, ktop)
    y_s = _tail(oa_s.reshape(DB, GDN_WIDTH), ob_s.reshape(DB, ATT_WIDTH), zs, xs2,
                p_sample[li].reshape(DB, PLE_DIM), wa, wb, wo, pn_row, wg, wp, tm=DB)

    shp = (B, T, ATT_HEADS, ATT_DH)
    shs = (DB, S_new, ATT_HEADS, ATT_DH)
    return (y_p.reshape(B, T, D_MODEL), y_s.reshape(DB, S_new, D_MODEL),
            kf_p.reshape(shp)[None], v_p.reshape(shp)[None], kif_p[None],
            s_p[None], conv_p[None],
            kf_s.reshape(shs)[None], v_s.reshape(shs)[None], kif_s.reshape(DB, S_new, IDX_DH)[None],
            s_s[None], conv_s[None])
```

```python
import functools
import math

import jax
import jax.numpy as jnp
from jax import lax
from jax.experimental import pallas as pl
from jax.experimental.pallas import tpu as pltpu

f32 = jnp.float32
bf16 = jnp.bfloat16
i32 = jnp.int32

D_MODEL = 2048
PAGE_SIZE = 128
GDN_HEADS = 8
GDN_DK = 128
GDN_DV = 128
GDN_WIDTH = GDN_HEADS * GDN_DV
CONV_W = 4
CONV_CH = 2 * GDN_HEADS * GDN_DK + GDN_WIDTH
GDN_CHUNK = 64
ATT_HEADS = 8
ATT_DH = 128
ATT_WIDTH = ATT_HEADS * ATT_DH
IDX_HEADS = 16
IDX_DH = 128
TOPK_MAX = 256
QUERY_BLOCK = 128
ROPE_THETA = 10000.0
PLE_DIM = 256
EPS = 1e-6

LANES = 128
VMEM_LIMIT = 56 * 1024 * 1024
NEG_BIG = -1e30
INT_MIN = -(2 ** 31)

HK = GDN_HEADS * GDN_DK
C_GQ, C_GK, C_GV, C_GZ = 0, HK, 2 * HK, 3 * HK
C_AQ = 4 * HK
C_AK = C_AQ + ATT_WIDTH
C_AV = C_AK + ATT_WIDTH
C_AZ = C_AV + ATT_WIDTH
C_IQ = C_AZ + ATT_WIDTH
C_MGA = C_IQ + IDX_HEADS * IDX_DH
C_MGB = C_MGA + D_MODEL
C_IK = C_MGB + D_MODEL
C_SMALL = C_IK + IDX_DH
NZ = C_SMALL + LANES
SM_GA, SM_GB, SM_IW = 0, GDN_HEADS, 2 * GDN_HEADS


def _cparams(sem):
    return pltpu.CompilerParams(dimension_semantics=sem, vmem_limit_bytes=VMEM_LIMIT)


def _dot(a, b):
    return jnp.dot(a, b, preferred_element_type=f32)


def _dot_nt(a, b):
    return lax.dot_general(a, b, (((1,), (1,)), ((), ())), preferred_element_type=f32)


def _sigmoid(x):
    return 1.0 / (1.0 + jnp.exp(-x))


def _silu(x):
    return x * _sigmoid(x)


O_GA = 4 * HK
O_AQ = O_GA + 2 * GDN_HEADS
O_IQ = O_AQ + 4 * ATT_WIDTH
O_IK = O_IQ + IDX_HEADS * IDX_DH
O_IW = O_IK + IDX_DH
O_MGA = O_IW + IDX_HEADS
D_IN = O_MGA + 2 * D_MODEL


def _shifted_cols(w_ref, src, n):
    a0 = (src // LANES) * LANES
    a1 = min(-(-(src + n) // LANES) * LANES, D_IN)
    win = w_ref[:, a0:a1]
    return win[:, src - a0:src - a0 + n]


def _wprep_kernel(w_ref, o_ref):
    o_ref[:, 0:O_GA] = w_ref[:, 0:O_GA].astype(bf16)
    o_ref[:, C_AQ:C_AQ + 4 * ATT_WIDTH] = _shifted_cols(w_ref, O_AQ, 4 * ATT_WIDTH).astype(bf16)
    o_ref[:, C_IQ:C_IQ + IDX_HEADS * IDX_DH] = _shifted_cols(w_ref, O_IQ, IDX_HEADS * IDX_DH).astype(bf16)
    o_ref[:, C_MGA:C_MGA + 2 * D_MODEL] = _shifted_cols(w_ref, O_MGA, 2 * D_MODEL).astype(bf16)
    o_ref[:, C_IK:C_IK + IDX_DH] = _shifted_cols(w_ref, O_IK, IDX_DH).astype(bf16)
    t_g = w_ref[:, O_GA:O_GA + LANES]
    a_iw = (O_IW // LANES) * LANES
    t_w = w_ref[:, a_iw:a_iw + LANES]
    lane = lax.broadcasted_iota(i32, t_g.shape, 1)
    small = jnp.where(lane < SM_IW, t_g, jnp.where(lane < SM_IW + IDX_HEADS, t_w, 0.0))
    o_ref[:, C_SMALL:C_SMALL + LANES] = small.astype(bf16)


def _wprep(w, tk):
    assert O_GA % LANES == 0 and O_IW - (O_IW // LANES) * LANES == SM_IW and 2 * GDN_HEADS == SM_IW
    k = w.shape[0]
    return pl.pallas_call(
        _wprep_kernel,
        grid=(k // tk,),
        in_specs=[pl.BlockSpec((tk, D_IN), lambda i: (i, 0))],
        out_specs=pl.BlockSpec((tk, NZ), lambda i: (i, 0)),
        out_shape=jax.ShapeDtypeStruct((k, NZ), bf16),
        compiler_params=_cparams(("parallel",)),
    )(w)


def _inproj_kernel(x_ref, g_ref, w_ref, o_ref, h_ref):
    @pl.when(pl.program_id(1) == 0)
    def _():
        x = x_ref[...]
        ms = jnp.mean(x * x, axis=-1, keepdims=True)
        h_ref[...] = (x * lax.rsqrt(ms + EPS) * g_ref[...]).astype(bf16)

    o_ref[...] = _dot(h_ref[...], w_ref[...])


def _inproj(x2d, gain_row, w_bf16, tm, tn):
    m, k = x2d.shape
    n = w_bf16.shape[1]
    return pl.pallas_call(
        _inproj_kernel,
        grid=(m // tm, n // tn),
        in_specs=[
            pl.BlockSpec((tm, k), lambda i, j: (i, 0)),
            pl.BlockSpec((1, k), lambda i, j: (0, 0)),
            pl.BlockSpec((k, tn), lambda i, j: (0, j)),
        ],
        out_specs=pl.BlockSpec((tm, tn), lambda i, j: (i, j)),
        out_shape=jax.ShapeDtypeStruct((m, n), f32),
        scratch_shapes=[pltpu.VMEM((tm, k), bf16)],
        compiler_params=_cparams(("parallel", "arbitrary")),
    )(x2d, gain_row, w_bf16)


def _softplus(x):
    return jnp.maximum(x, 0.0) + jnp.log1p(jnp.exp(-jnp.abs(x)))


def _gdn_heads_out(conv_fn, q_ref, k_ref, v_ref):
    for c in range(3 * GDN_HEADS):
        a = _silu(conv_fn(c))
        h = c % GDN_HEADS
        sl = slice(h * LANES, (h + 1) * LANES)
        if c < 2 * GDN_HEADS:
            nrm = lax.rsqrt(jnp.sum(a * a, axis=-1, keepdims=True) + EPS)
            if c < GDN_HEADS:
                q_ref[:, sl] = a * nrm * (GDN_DK ** -0.5)
            else:
                k_ref[:, sl] = a * nrm
        else:
            v_ref[:, sl] = a


def _gdn_prep_kernel(x_ref, halo_ref, prev_ref, cw_ref, sm_ref, a_ref, dt_ref,
                     q_ref, k_ref, v_ref, gcb_ref, btb_ref, bg_ref, xe_ref, *, tm):
    i = pl.program_id(1)
    xe_ref[pl.ds(8, tm), :] = x_ref[0]

    @pl.when(i == 0)
    def _():
        xe_ref[pl.ds(0, 8), :] = prev_ref[0]

    @pl.when(i > 0)
    def _():
        xe_ref[pl.ds(0, 8), :] = halo_ref[0]

    def conv_fn(c):
        sl = slice(c * LANES, (c + 1) * LANES)
        acc = xe_ref[pl.ds(8 - (CONV_W - 1), tm), sl] * cw_ref[0:1, sl]
        for j in range(1, CONV_W):
            acc = acc + xe_ref[pl.ds(8 - (CONV_W - 1) + j, tm), sl] * cw_ref[j:j + 1, sl]
        return acc

    _gdn_heads_out(conv_fn, q_ref.at[0], k_ref.at[0], v_ref.at[0])

    sm = sm_ref[0]
    g = -jnp.exp(a_ref[...]) * _softplus(sm + dt_ref[...])
    row = lax.broadcasted_iota(i32, (tm, LANES), 0) % GDN_CHUNK
    s = 1
    while s < GDN_CHUNK:
        g = g + jnp.where(row >= s, pltpu.roll(g, s, axis=0), 0.0)
        s *= 2
    lane = lax.broadcasted_iota(i32, (tm, LANES), 1)
    bt = _sigmoid(sm)
    bg_ref[0] = jnp.where(lane < GDN_HEADS, g, bt)
    for h in range(GDN_HEADS):
        sl = slice(h * LANES, (h + 1) * LANES)
        gcb_ref[0, :, sl] = jnp.broadcast_to(g[:, SM_GA + h:SM_GA + h + 1], (tm, LANES))
        btb_ref[0, :, sl] = jnp.broadcast_to(bt[:, SM_GB + h:SM_GB + h + 1], (tm, LANES))


def _gdn_prep(z3, prev8, conv_w, a_row, dt_row, tm):
    b, t, _ = z3.shape
    nblk = tm // 8
    kern = functools.partial(_gdn_prep_kernel, tm=tm)
    tok = pl.BlockSpec((1, tm, HK), lambda bi, i: (bi, i, 0))
    out_sd = jax.ShapeDtypeStruct((b, t, HK), f32)
    return pl.pallas_call(
        kern,
        grid=(b, t // tm),
        in_specs=[
            pl.BlockSpec((1, tm, CONV_CH), lambda bi, i: (bi, i, 0)),
            pl.BlockSpec((1, 8, CONV_CH), lambda bi, i: (bi, jnp.maximum(i * nblk - 1, 0), 0)),
            pl.BlockSpec((1, 8, CONV_CH), lambda bi, i: (bi, 0, 0)),
            pl.BlockSpec((CONV_W, CONV_CH), lambda bi, i: (0, 0)),
            pl.BlockSpec((1, tm, LANES), lambda bi, i: (bi, i, C_SMALL // LANES)),
            pl.BlockSpec((1, LANES), lambda bi, i: (0, 0)),
            pl.BlockSpec((1, LANES), lambda bi, i: (0, 0)),
        ],
        out_specs=[tok] * 5 + [pl.BlockSpec((1, tm, LANES), lambda bi, i: (bi, i, 0))],
        out_shape=[out_sd] * 5 + [jax.ShapeDtypeStruct((b, t, LANES), f32)],
        scratch_shapes=[pltpu.VMEM((tm + 8, CONV_CH), f32)],
        compiler_params=_cparams(("parallel", "arbitrary")),
    )(z3, z3, prev8, conv_w, z3, a_row, dt_row)


def _gdn_prep_sample_kernel(x_ref, s0_ref, s1_ref, s2_ref, cw_ref, sm_ref, a_ref, dt_ref,
                            q_ref, k_ref, v_ref, bg_ref):
    def conv_fn(c):
        sl = slice(c * LANES, (c + 1) * LANES)
        return (s0_ref[:, sl] * cw_ref[0:1, sl] + s1_ref[:, sl] * cw_ref[1:2, sl]
                + s2_ref[:, sl] * cw_ref[2:3, sl] + x_ref[:, sl] * cw_ref[3:4, sl])

    _gdn_heads_out(conv_fn, q_ref, k_ref, v_ref)
    sm = sm_ref[...]
    g = -jnp.exp(a_ref[...]) * _softplus(sm + dt_ref[...])
    lane = lax.broadcasted_iota(i32, sm.shape, 1)
    bg_ref[...] = jnp.where(lane < GDN_HEADS, g, _sigmoid(sm))


def _gdn_prep_sample(zs, sc0, sc1, sc2, conv_w, a_row, dt_row):
    db = zs.shape[0]
    full = lambda shape: pl.BlockSpec(shape, lambda i: (0,) * len(shape))
    out_sd = jax.ShapeDtypeStruct((db, HK), f32)
    return pl.pallas_call(
        _gdn_prep_sample_kernel,
        grid=(1,),
        in_specs=[
            pl.BlockSpec((db, CONV_CH), lambda i: (0, 0)),
            full((db, CONV_CH)), full((db, CONV_CH)), full((db, CONV_CH)),
            full((CONV_W, CONV_CH)),
            pl.BlockSpec((db, LANES), lambda i: (0, C_SMALL // LANES)),
            full((1, LANES)), full((1, LANES)),
        ],
        out_specs=[full((db, HK)), full((db, HK)), full((db, HK)), full((db, LANES))],
        out_shape=[out_sd, out_sd, out_sd, jax.ShapeDtypeStruct((db, LANES), f32)],
        compiler_params=_cparams(("arbitrary",)),
    )(zs, sc0, sc1, sc2, conv_w, zs, a_row, dt_row)


GDN_TS = 512
GDN_HB = 8


def _bmm(a, b):
    return jnp.einsum('bij,bjk->bik', a, b, preferred_element_type=f32)


def _bmm_nt(a, b):
    return jnp.einsum('bid,bjd->bij', a, b, preferred_element_type=f32)


def _gdn_scan_kernel(q_ref, k_ref, v_ref, gz_ref, gcb_ref, btb_ref, gr_ref, s0_ref, nrm_ref,
                     o_ref, so_ref, s_ref):
    step = pl.program_id(2)
    nc = GDN_TS // GDN_CHUNK
    c = GDN_CHUNK

    @pl.when(step == 0)
    def _():
        s_ref[...] = s0_ref[0]

    ri = lax.broadcasted_iota(i32, (1, c, c), 1)
    ci = lax.broadcasted_iota(i32, (1, c, c), 2)
    tri = ri >= ci
    strict = ri > ci
    eye = (ri == ci).astype(f32)

    hb = GDN_HB

    def stack(ref):
        return jnp.concatenate(
            [ref[0, :, hh * LANES:(hh + 1) * LANES].reshape(nc, c, LANES) for hh in range(hb)], axis=0)

    q, k, v = stack(q_ref), stack(k_ref), stack(v_ref)
    gcb = stack(gcb_ref)
    bt = stack(btb_ref)
    gr = gr_ref[0].reshape(hb * nc, 1, c)
    glast = gr[:, :, c - 1:c]
    decay = jnp.exp(jnp.where(tri, gcb[:, :, 0:c] - gr, -jnp.inf))
    kb = k * bt
    m = jnp.where(strict, _bmm_nt(kb, k) * decay, 0.0)
    x = eye - m
    p = m
    for _ in range(5):
        p = _bmm(p, p)
        x = x + _bmm(x, p)
    eg = jnp.exp(gcb)
    by_chunk = lambda a: a.reshape((hb, nc) + a.shape[1:])
    u = by_chunk(_bmm(x, v * bt))
    w = by_chunk(_bmm(x, kb * eg))
    qk = by_chunk(_bmm_nt(q, k) * decay)
    qg = by_chunk(q * eg)
    kg = by_chunk(k * jnp.exp(glast - gcb))
    gl = by_chunk(jnp.exp(glast))

    gain = nrm_ref[...]
    s = s_ref[...]
    for ch in range(nc):
        v_new = u[:, ch] - _bmm(w[:, ch], s)
        o = _bmm(qg[:, ch], s) + _bmm(qk[:, ch], v_new)
        s = s * gl[:, ch] + jnp.einsum('hck,hcv->hkv', kg[:, ch], v_new, preferred_element_type=f32)
        ms = jnp.mean(o * o, axis=-1, keepdims=True)
        on = o * lax.rsqrt(ms + EPS) * gain
        rows = pl.ds(ch * c, c)
        for hh in range(hb):
            sl = slice(hh * LANES, (hh + 1) * LANES)
            o_ref[0, rows, sl] = (on[hh] * _silu(gz_ref[0, rows, sl])).astype(o_ref.dtype)
    s_ref[...] = s

    @pl.when(step == pl.num_programs(2) - 1)
    def _():
        so_ref[0] = s_ref[...]


def _gdn_scan(q, k, v, z3, gcb, btb, gc_row, s0, gdn_norm_row):
    b, t, _ = q.shape
    hw = GDN_HB * LANES
    nc = GDN_TS // GDN_CHUNK
    qspec = pl.BlockSpec((1, GDN_TS, hw), lambda bi, hg, i: (bi, i, hg))
    return pl.pallas_call(
        _gdn_scan_kernel,
        grid=(b, GDN_HEADS // GDN_HB, t // GDN_TS),
        in_specs=[
            qspec, qspec, qspec,
            pl.BlockSpec((1, GDN_TS, hw), lambda bi, hg, i: (bi, i, C_GZ // hw + hg)),
            qspec, qspec,
            pl.BlockSpec((1, GDN_HB, nc, 1, GDN_CHUNK), lambda bi, hg, i: (bi, hg, i, 0, 0)),
            pl.BlockSpec((1, GDN_HB, GDN_DK, GDN_DV), lambda bi, hg, i: (bi, hg, 0, 0)),
            pl.BlockSpec((1, LANES), lambda bi, hg, i: (0, 0)),
        ],
        out_specs=[
            pl.BlockSpec((1, GDN_TS, hw), lambda bi, hg, i: (bi, i, hg)),
            pl.BlockSpec((1, GDN_HB, GDN_DK, GDN_DV), lambda bi, hg, i: (bi, hg, 0, 0)),
        ],
        out_shape=[jax.ShapeDtypeStruct((b, t, GDN_WIDTH), bf16),
                   jax.ShapeDtypeStruct((b, GDN_HEADS, GDN_DK, GDN_DV), f32)],
        scratch_shapes=[pltpu.VMEM((GDN_HB, GDN_DK, GDN_DV), f32)],
        compiler_params=_cparams(("parallel", "parallel", "arbitrary")),
    )(q, k, v, z3, gcb, btb, gc_row, s0, gdn_norm_row)


GS_BB = 8


def _gdn_step_kernel(q_ref, k_ref, v_ref, gz_ref, bg_ref, s0_ref, nrm_ref, o_ref, so_ref):
    for bb in range(GS_BB):
        row = slice(bb, bb + 1)
        for h in range(GDN_HEADS):
            sl = slice(h * LANES, (h + 1) * LANES)
            s0 = s0_ref[bb, h]
            q = q_ref[row, sl]
            k = k_ref[row, sl]
            v = v_ref[row, sl]
            eg = jnp.exp(bg_ref[row, SM_GA + h:SM_GA + h + 1])
            bt = bg_ref[row, SM_GB + h:SM_GB + h + 1]
            kc = jnp.broadcast_to(k, (GDN_DK, LANES)).T
            qc = jnp.broadcast_to(q, (GDN_DK, LANES)).T
            ks = jnp.sum(kc * s0, axis=0, keepdims=True)
            qs = jnp.sum(qc * s0, axis=0, keepdims=True)
            qk = jnp.sum(q * k, axis=-1, keepdims=True)
            v_new = bt * v - (bt * eg) * ks
            o = eg * qs + qk * v_new
            so_ref[bb, h] = s0 * eg + kc * v_new
            ms = jnp.mean(o * o, axis=-1, keepdims=True)
            on = o * lax.rsqrt(ms + EPS) * nrm_ref[...]
            o_ref[row, sl] = on * _silu(gz_ref[row, sl])


def _gdn_step(q, k, v, zs, bgs, s0, gdn_norm_row):
    db = q.shape[0]
    tok = pl.BlockSpec((GS_BB, HK), lambda i: (i, 0))
    st = pl.BlockSpec((GS_BB, GDN_HEADS, GDN_DK, GDN_DV), lambda i: (i, 0, 0, 0))
    return pl.pallas_call(
        _gdn_step_kernel,
        grid=(db // GS_BB,),
        in_specs=[tok, tok, tok, pl.BlockSpec((GS_BB, GDN_WIDTH), lambda i: (i, C_GZ // GDN_WIDTH)),
                  pl.BlockSpec((GS_BB, LANES), lambda i: (i, 0)), st,
                  pl.BlockSpec((1, LANES), lambda i: (0, 0))],
        out_specs=[tok, st],
        out_shape=[jax.ShapeDtypeStruct((db, GDN_WIDTH), f32),
                   jax.ShapeDtypeStruct((db, GDN_HEADS, GDN_DK, GDN_DV), f32)],
        compiler_params=_cparams(("parallel",)),
    )(q, k, v, zs, bgs, s0, gdn_norm_row)


def _rope(x, cos, sin_signed):
    return x * cos + pltpu.roll(x, LANES // 2, axis=1) * sin_signed


def _dsa_prep_kernel(aq_ref, ak_ref, av_ref, iq_ref, ik_ref, sm_ref, cos_ref, sin_ref,
                     qn_ref, kn_ref, *out_refs, transposed, q_scale):
    if transposed:
        qb_ref, kf_ref, vf_ref, kb_ref, vt_ref, qi_ref, kif_ref, kib_ref, wt_ref = out_refs
    else:
        qb_ref, kf_ref, vf_ref, qi_ref, kif_ref, kib_ref = out_refs
    cos = cos_ref[...]
    sin = sin_ref[...]
    for h in range(ATT_HEADS):
        sl = slice(h * LANES, (h + 1) * LANES)
        a = aq_ref[0, :, sl]
        a = a * lax.rsqrt(jnp.mean(a * a, axis=-1, keepdims=True) + EPS) * qn_ref[...]
        qb_ref[0, :, sl] = (_rope(a, cos, sin) * q_scale).astype(qb_ref.dtype)
        a = ak_ref[0, :, sl]
        a = a * lax.rsqrt(jnp.mean(a * a, axis=-1, keepdims=True) + EPS) * kn_ref[...]
        kr = _rope(a, cos, sin)
        kf_ref[0, :, sl] = kr
        v = av_ref[0, :, sl]
        vf_ref[0, :, sl] = v
        if transposed:
            kb_ref[0, :, sl] = kr.astype(bf16)
            vt_ref[0, 0, sl, :] = v.T.astype(bf16)
    for h in range(IDX_HEADS):
        sl = slice(h * LANES, (h + 1) * LANES)
        qi_ref[0, :, sl] = _rope(iq_ref[0, :, sl], cos, sin).astype(bf16)
    ki = _rope(ik_ref[0], cos, sin)
    kif_ref[0] = ki
    kib_ref[0] = ki.astype(bf16)
    if transposed:
        wt_ref[0] = (sm_ref[0] * (IDX_HEADS ** -0.5 * IDX_DH ** -0.5)).T


def _dsa_prep(z3, cos, sin, qn_row, kn_row, tm, transposed, kc, q_scale):
    b, t, _ = z3.shape
    zspec = lambda w, off: pl.BlockSpec((1, tm, w), lambda bi, i: (bi, i, off // w))
    tok = lambda w, dt: (pl.BlockSpec((1, tm, w), lambda bi, i: (bi, i, 0)),
                         jax.ShapeDtypeStruct((b, t, w), dt))
    if transposed:
        per = kc // tm
        outs = [tok(ATT_WIDTH, bf16), tok(ATT_WIDTH, f32), tok(ATT_WIDTH, f32), tok(ATT_WIDTH, bf16),
                (pl.BlockSpec((1, 1, ATT_WIDTH, tm), lambda bi, i: (bi, i // per, 0, i % per)),
                 jax.ShapeDtypeStruct((b, t // kc, ATT_WIDTH, kc), bf16)),
                tok(IDX_HEADS * IDX_DH, bf16), tok(IDX_DH, f32), tok(IDX_DH, bf16),
                (pl.BlockSpec((1, LANES, tm), lambda bi, i: (bi, 0, i)),
                 jax.ShapeDtypeStruct((b, LANES, t), f32))]
    else:
        outs = [tok(ATT_WIDTH, f32), tok(ATT_WIDTH, f32), tok(ATT_WIDTH, f32),
                tok(IDX_HEADS * IDX_DH, bf16), tok(IDX_DH, f32), tok(IDX_DH, bf16)]
    kern = functools.partial(_dsa_prep_kernel, transposed=transposed, q_scale=q_scale)
    return pl.pallas_call(
        kern,
        grid=(b, t // tm),
        in_specs=[
            zspec(ATT_WIDTH, C_AQ), zspec(ATT_WIDTH, C_AK), zspec(ATT_WIDTH, C_AV),
            zspec(IDX_HEADS * IDX_DH, C_IQ), zspec(IDX_DH, C_IK), zspec(LANES, C_SMALL),
            pl.BlockSpec((tm, LANES), lambda bi, i: (i, 0)),
            pl.BlockSpec((tm, LANES), lambda bi, i: (i, 0)),
            pl.BlockSpec((1, LANES), lambda bi, i: (0, 0)),
            pl.BlockSpec((1, LANES), lambda bi, i: (0, 0)),
        ],
        out_specs=[o[0] for o in outs],
        out_shape=[o[1] for o in outs],
        compiler_params=_cparams(("parallel", "parallel")),
    )(z3, z3, z3, z3, z3, z3, cos, sin, qn_row, kn_row)


def _key_to_float(key):
    bits = jnp.where(key >= 0, key, key ^ jnp.int32(0x7FFFFFFF))
    return pltpu.bitcast(bits, f32)


def _kth_largest(count_ge, k, shape):
    kf = jnp.float32(k)
    zero_ok = count_ge(jnp.zeros(shape, f32)) >= kf
    cur = jnp.where(zero_ok, jnp.int32(0), jnp.int32(INT_MIN))

    def body(it, cur):
        cand = cur + jnp.left_shift(jnp.int32(1), jnp.int32(30) - it)
        ok = count_ge(_key_to_float(cand)) >= kf
        return jnp.where(ok, cand, cur)

    cur = lax.fori_loop(0, 31, body, cur)
    return _key_to_float(cur)


def _tie_bound(count_eq_le, need, nbits, shape):
    lo = jnp.full(shape, -1, i32)

    def body(it, lo):
        cand = lo + jnp.left_shift(jnp.int32(1), jnp.int32(nbits - 1) - it)
        short = count_eq_le(cand) < need
        return jnp.where(short, cand, lo)

    lo = lax.fori_loop(0, nbits, body, lo)
    return lo + 1


ATT_KC = 512
ATT_QB = 256
ATT_KS = 128
CNT_ROWS = 64
LOG2E = 1.4426950408889634


def _dsa_attend_kernel(qi_ref, wt_ref, ki_ref, qb_ref, kb_ref, vt_ref, az_ref, o_ref,
                       sc_ref, *head_refs, t_total):
    i = pl.program_id(1)
    qb = ATT_QB
    kc = ATT_KC
    nch = (i * qb + qb + kc - 1) // kc
    tq = i * qb + lax.broadcasted_iota(i32, (1, qb), 1)
    ktop = min(TOPK_MAX, t_total // 4)

    def spos(c):
        return c * kc + lax.broadcasted_iota(i32, (kc, 1), 0)

    def idx_body(c, carry):
        for sub in range(kc // ATT_KS):
            base = c * kc + sub * ATT_KS
            ksub = ki_ref[0, pl.ds(base, ATT_KS), :]
            acc = jnp.zeros((ATT_KS, qb), f32)
            for h in range(IDX_HEADS):
                d = _dot_nt(ksub, qi_ref[0, :, h * LANES:(h + 1) * LANES])
                acc = acc + jnp.maximum(d, 0.0) * wt_ref[0, SM_IW + h:SM_IW + h + 1, :]
            sp = base + lax.broadcasted_iota(i32, (ATT_KS, 1), 0)
            sc_ref[pl.ds(base, ATT_KS), :] = jnp.where(sp <= tq, acc, -jnp.inf)
        return carry

    lax.fori_loop(0, nch, idx_body, 0)

    def col_count(pred_fn):
        def body(c, acc):
            m = pred_fn(sc_ref[pl.ds(c * kc, kc), :], c)
            return acc + jnp.sum(jnp.where(m, 1.0, 0.0).reshape(kc // CNT_ROWS, CNT_ROWS, qb), axis=0)
        acc = lax.fori_loop(0, nch, body, jnp.zeros((CNT_ROWS, qb), f32))
        return jnp.sum(acc, axis=0, keepdims=True)

    def search():
        thr = _kth_largest(lambda cand: col_count(lambda s, c: s >= cand), ktop, (1, qb))
        c_ge = col_count(lambda s, c: s >= thr)
        c_gt = col_count(lambda s, c: s > thr)
        need = jnp.float32(ktop) - c_gt
        jb = lax.cond(
            jnp.max(c_ge) > jnp.float32(ktop),
            lambda: _tie_bound(
                lambda cand: col_count(lambda s, c: (s == thr) & (spos(c) <= cand)),
                need, max(1, (t_total - 1).bit_length()), (1, qb)),
            lambda: jnp.full((1, qb), t_total, i32))
        return thr, jb

    def take_all():
        return jnp.full((1, qb), -jnp.inf, f32), jnp.full((1, qb), t_total, i32)

    thr, jb = lax.cond((i + 1) * qb <= ktop, take_all, search)

    def bias_body(c, carry):
        s = sc_ref[pl.ds(c * kc, kc), :]
        sp = spos(c)
        sel = ((s > thr) | ((s == thr) & (sp <= jb))) & (sp <= tq)
        sc_ref[pl.ds(c * kc, kc), :] = jnp.where(sel, 0.0, NEG_BIG)
        return carry

    lax.fori_loop(0, nch, bias_body, 0)

    nh = ATT_HEADS
    m_refs, l_refs, al_refs, acc_refs, s_refs, p_refs = (head_refs[j * nh:(j + 1) * nh] for j in range(6))
    for h in range(nh):
        m_refs[h][...] = jnp.full(m_refs[h].shape, NEG_BIG, f32)
        l_refs[h][...] = jnp.zeros(l_refs[h].shape, f32)
        acc_refs[h][...] = jnp.zeros(acc_refs[h].shape, f32)

    def att_body(c, carry):
        for h in range(nh):
            sl = slice(h * LANES, (h + 1) * LANES)
            s_refs[h][...] = _dot_nt(kb_ref[0, pl.ds(c * kc, kc), sl], qb_ref[0, :, sl])
        for h in range(nh):
            s = s_refs[h][...] + sc_ref[pl.ds(c * kc, kc), :]
            m = m_refs[h][...]
            m_new = jnp.maximum(m, jnp.max(s, axis=0, keepdims=True))
            alpha = jnp.exp2(m - m_new)
            p = jnp.exp2(s - m_new)
            m_refs[h][...] = m_new
            al_refs[h][...] = alpha
            l_refs[h][...] = alpha * l_refs[h][...] + jnp.sum(p, axis=0, keepdims=True)
            p_refs[h][...] = p.astype(bf16)
        for h in range(nh):
            sl = slice(h * LANES, (h + 1) * LANES)
            acc_refs[h][...] = (al_refs[h][...] * acc_refs[h][...]
                                + _dot(vt_ref[0, c, sl, :], p_refs[h][...]))
        return carry

    lax.fori_loop(0, nch, att_body, 0)

    for h in range(ATT_HEADS):
        sl = slice(h * LANES, (h + 1) * LANES)
        out = (acc_refs[h][...] / l_refs[h][...]).T
        o_ref[0, :, sl] = (out * _silu(az_ref[0, :, sl])).astype(o_ref.dtype)


def _dsa_attend(qi, wt, ki, qb, kb, vt, z3):
    b, t, _ = qb.shape
    kern = functools.partial(_dsa_attend_kernel, t_total=t)
    return pl.pallas_call(
        kern,
        grid=(b, t // ATT_QB),
        in_specs=[
            pl.BlockSpec((1, ATT_QB, IDX_HEADS * IDX_DH), lambda bi, i: (bi, i, 0)),
            pl.BlockSpec((1, LANES, ATT_QB), lambda bi, i: (bi, 0, i)),
            pl.BlockSpec((1, t, IDX_DH), lambda bi, i: (bi, 0, 0)),
            pl.BlockSpec((1, ATT_QB, ATT_WIDTH), lambda bi, i: (bi, i, 0)),
            pl.BlockSpec((1, t, ATT_WIDTH), lambda bi, i: (bi, 0, 0)),
            pl.BlockSpec((1, t // ATT_KC, ATT_WIDTH, ATT_KC), lambda bi, i: (bi, 0, 0, 0)),
            pl.BlockSpec((1, ATT_QB, ATT_WIDTH), lambda bi, i: (bi, i, C_AZ // ATT_WIDTH)),
        ],
        out_specs=pl.BlockSpec((1, ATT_QB, ATT_WIDTH), lambda bi, i: (bi, i, 0)),
        out_shape=jax.ShapeDtypeStruct((b, t, ATT_WIDTH), bf16),
        scratch_shapes=([pltpu.VMEM((t, ATT_QB), f32)]
                        + [pltpu.VMEM((1, ATT_QB), f32) for _ in range(3 * ATT_HEADS)]
                        + [pltpu.VMEM((LANES, ATT_QB), f32) for _ in range(ATT_HEADS)]
                        + [pltpu.VMEM((ATT_KC, ATT_QB), f32) for _ in range(ATT_HEADS)]
                        + [pltpu.VMEM((ATT_KC, ATT_QB), bf16) for _ in range(ATT_HEADS)]),
        compiler_params=_cparams(("parallel", "arbitrary")),
    )(qi, wt, ki, qb, kb, vt, z3)


SC_PG = 16
DMA_UNROLL = 8


def _sample_scores_kernel(pt_s, qi_ref, w_ref, cik_ref, o_ref, buf, sem, *, n_pages):
    b = pl.program_id(0)
    nb = pl.num_programs(0)

    def page_copy(bb, slot, p):
        return pltpu.make_async_copy(cik_ref.at[pt_s[bb * n_pages + p]], buf.at[slot, p], sem.at[slot])

    def issue(bb, slot):
        def body(p, carry):
            page_copy(bb, slot, p).start()
            return carry
        lax.fori_loop(0, n_pages, body, 0, unroll=DMA_UNROLL)

    def wait_all(bb, slot):
        def body(p, carry):
            page_copy(bb, slot, p).wait()
            return carry
        lax.fori_loop(0, n_pages, body, 0, unroll=DMA_UNROLL)

    slot = b % 2

    @pl.when(b == 0)
    def _():
        issue(b, slot)

    @pl.when(b + 1 < nb)
    def _():
        issue(b + 1, 1 - slot)

    wait_all(b, slot)

    qi = qi_ref[0]
    w = w_ref[0]
    gk = SC_PG * PAGE_SIZE
    for g in range(n_pages // SC_PG):
        keys = buf[slot, g * SC_PG:(g + 1) * SC_PG].reshape(gk, IDX_DH).astype(bf16)
        d = _dot_nt(qi, keys)
        o_ref[0, :, g * gk:(g + 1) * gk] = jnp.sum(jnp.maximum(d, 0.0) * w, axis=0, keepdims=True)


def _sample_scores(pt_flat, qi3, w3, cache_ik):
    db = qi3.shape[0]
    n_pages = pt_flat.shape[0] // db
    kern = functools.partial(_sample_scores_kernel, n_pages=n_pages)
    grid_spec = pltpu.PrefetchScalarGridSpec(
        num_scalar_prefetch=1,
        grid=(db,),
        in_specs=[pl.BlockSpec((1, IDX_HEADS, IDX_DH), lambda bi, pt: (bi, 0, 0)),
                  pl.BlockSpec((1, IDX_HEADS, 1), lambda bi, pt: (bi, 0, 0)),
                  pl.BlockSpec(memory_space=pl.ANY)],
        out_specs=pl.BlockSpec((1, 1, n_pages * PAGE_SIZE), lambda bi, pt: (bi, 0, 0)),
        scratch_shapes=[pltpu.VMEM((2, n_pages, PAGE_SIZE, IDX_DH), f32), pltpu.SemaphoreType.DMA((2,))],
    )
    return pl.pallas_call(
        kern,
        grid_spec=grid_spec,
        out_shape=jax.ShapeDtypeStruct((db, 1, n_pages * PAGE_SIZE), f32),
        compiler_params=_cparams(("arbitrary",)),
    )(pt_flat, qi3, w3, cache_ik)


def _sample_select_kernel(sc_ref, qi_ref, ki_ref, w_ref, pt_ref, idx_ref, rows_ref, *, n_past, ktop):
    db, n_pages, ps = sc_ref.shape
    sc = sc_ref[...]

    def red(x):
        return jnp.sum(jnp.sum(x, axis=2, keepdims=True), axis=1, keepdims=True)

    dn = jnp.sum(qi_ref[...].astype(f32) * ki_ref[...].astype(f32), axis=2, keepdims=True)
    s_new = jnp.sum(jnp.maximum(dn, 0.0) * w_ref[...], axis=1, keepdims=True)

    def count(pred):
        return red(pred(sc).astype(f32)) + pred(s_new).astype(f32)

    shape = (db, 1, 1)
    thr = _kth_largest(lambda cand: count(lambda s: s >= cand), ktop, shape)
    c_ge = count(lambda s: s >= thr)
    c_gt = count(lambda s: s > thr)
    need = jnp.float32(ktop) - c_gt
    pos = (lax.broadcasted_iota(i32, (1, n_pages, ps), 1) * ps
           + lax.broadcasted_iota(i32, (1, n_pages, ps), 2))

    def eq_le(cand):
        return (red(((sc == thr) & (pos <= cand)).astype(f32))
                + ((s_new == thr) & (n_past <= cand)).astype(f32))

    jb = lax.cond(jnp.max(c_ge) > jnp.float32(ktop),
                  lambda: _tie_bound(eq_le, need, (n_past + 1).bit_length(), shape),
                  lambda: jnp.full(shape, n_past + 1, i32))
    sel = (sc > thr) | ((sc == thr) & (pos <= jb))

    r_i = lax.broadcasted_iota(i32, (ps, ps), 0)
    c_i = lax.broadcasted_iota(i32, (ps, ps), 1)
    upper = (r_i <= c_i).astype(bf16)
    upper_pg = (lax.broadcasted_iota(i32, (n_pages, n_pages), 0)
                <= lax.broadcasted_iota(i32, (n_pages, n_pages), 1)).astype(bf16)
    jcol = lax.broadcasted_iota(i32, (ktop, 1), 0).astype(f32)
    plane = lax.broadcasted_iota(i32, (1, n_pages), 1).astype(f32)
    ones8 = jnp.ones((8, ps), bf16)

    selb = jnp.where(sel, 1.0, 0.0).astype(bf16)
    for b in range(db):
        sb = selb[b]
        incl = _dot(sb, upper)
        tot_row = _dot_nt(ones8, sb)[0:1]
        cum_row = _dot(jnp.broadcast_to(tot_row, (8, n_pages)).astype(bf16), upper_pg)[0:1]
        page_of = jnp.sum((cum_row <= jcol).astype(f32), axis=1, keepdims=True)
        onehot = (page_of == plane)
        before = jnp.sum(jnp.where(onehot, cum_row - tot_row, 0.0), axis=1, keepdims=True)
        rloc = jcol - before
        incl_rows = _dot(onehot.astype(bf16), incl.astype(bf16))
        off_of = jnp.sum((incl_rows <= rloc).astype(f32), axis=1, keepdims=True)
        idx = jnp.minimum(page_of * ps + off_of, jnp.float32(n_past))
        idx_ref[b] = idx.astype(i32)
        pidx = jnp.minimum(idx, jnp.float32(n_past - 1))
        pg = jnp.floor(pidx * (1.0 / ps))
        phys = jnp.sum(jnp.where(pg == plane, pt_ref[b], 0.0), axis=1, keepdims=True)
        rows_ref[b] = (phys * ps + (pidx - pg * ps)).astype(i32)


def _sample_select(sc3, qi3, ki3, w3, pt3, n_past, ktop):
    db = sc3.shape[0]
    kern = functools.partial(_sample_select_kernel, n_past=n_past, ktop=ktop)
    full = lambda a: pl.BlockSpec(a.shape, lambda i: (0,) * a.ndim)
    out_spec = pl.BlockSpec((db, ktop, 1), lambda i: (0, 0, 0))
    out_sd = jax.ShapeDtypeStruct((db, ktop, 1), i32)
    return pl.pallas_call(
        kern,
        grid=(1,),
        in_specs=[full(sc3), full(qi3), full(ki3), full(w3), full(pt3)],
        out_specs=[out_spec, out_spec],
        out_shape=[out_sd, out_sd],
        compiler_params=_cparams(("arbitrary",)),
    )(sc3, qi3, ki3, w3, pt3)


def _sample_attend_kernel(rows_s, idxv_ref, q_ref, kn_ref, vn_ref, az_ref, ck_ref, cv_ref,
                          o_ref, kbuf, vbuf, sem, *, n_past, ktop):
    b = pl.program_id(0)
    nb = pl.num_programs(0)

    def row_copies(bb, slot, j):
        row = rows_s[bb * ktop + j]
        ck = pltpu.make_async_copy(ck_ref.at[row], kbuf.at[slot, :, j, :], sem.at[0, slot])
        cv = pltpu.make_async_copy(cv_ref.at[row], vbuf.at[slot, :, j, :], sem.at[1, slot])
        return ck, cv

    def issue(bb, slot):
        def body(j, carry):
            ck, cv = row_copies(bb, slot, j)
            ck.start()
            cv.start()
            return carry
        lax.fori_loop(0, ktop, body, 0, unroll=DMA_UNROLL)

    def wait_all(bb, slot):
        def body(j, carry):
            ck, cv = row_copies(bb, slot, j)
            ck.wait()
            cv.wait()
            return carry
        lax.fori_loop(0, ktop, body, 0, unroll=DMA_UNROLL)

    slot = b % 2

    @pl.when(b == 0)
    def _():
        issue(b, slot)

    @pl.when(b + 1 < nb)
    def _():
        issue(b + 1, 1 - slot)

    wait_all(b, slot)

    is_new = idxv_ref[0] >= n_past
    newf = is_new.astype(f32)
    for h in range(ATT_HEADS):
        sl = slice(h * LANES, (h + 1) * LANES)
        qh = q_ref[0, :, sl]
        q8 = jnp.broadcast_to(qh, (8, LANES)).astype(bf16)
        s = _dot_nt(q8, kbuf[slot, h].astype(bf16))[0:1]
        s_new = jnp.sum(qh.astype(bf16).astype(f32) * kn_ref[0, :, sl].astype(bf16).astype(f32),
                        axis=-1, keepdims=True)
        s = jnp.where(is_new, s_new, s)
        m = jnp.max(s, axis=-1, keepdims=True)
        p = jnp.exp(s - m)
        l = jnp.sum(p, axis=-1, keepdims=True)
        p_old = jnp.broadcast_to(p * (1.0 - newf), (8, ktop)).astype(bf16)
        pv = _dot(p_old, vbuf[slot, h].astype(bf16))[0:1]
        pv = pv + jnp.sum(p * newf, axis=-1, keepdims=True) * vn_ref[0, :, sl]
        o_ref[0, :, sl] = ((pv / l) * _silu(az_ref[0, :, sl])).astype(o_ref.dtype)


def _sample_attend(rows_flat, idx_row, q3, kn3, vn3, az3, ck3, cv3, n_past, ktop):
    db = q3.shape[0]
    kern = functools.partial(_sample_attend_kernel, n_past=n_past, ktop=ktop)
    tok = pl.BlockSpec((1, 1, ATT_WIDTH), lambda bi, r: (bi, 0, 0))
    grid_spec = pltpu.PrefetchScalarGridSpec(
        num_scalar_prefetch=1,
        grid=(db,),
        in_specs=[pl.BlockSpec((1, 1, ktop), lambda bi, r: (bi, 0, 0)), tok, tok, tok, tok,
                  pl.BlockSpec(memory_space=pl.ANY), pl.BlockSpec(memory_space=pl.ANY)],
        out_specs=tok,
        scratch_shapes=[pltpu.VMEM((2, ATT_HEADS, ktop, ATT_DH), f32),
                        pltpu.VMEM((2, ATT_HEADS, ktop, ATT_DH), f32),
                        pltpu.SemaphoreType.DMA((2, 2))],
    )
    return pl.pallas_call(
        kern,
        grid_spec=grid_spec,
        out_shape=jax.ShapeDtypeStruct((db, 1, ATT_WIDTH), bf16),
        compiler_params=_cparams(("arbitrary",)),
    )(rows_flat, idx_row, q3, kn3, vn3, az3, ck3, cv3)


def _tail_kernel(oa_ref, ob_ref, mga_ref, mgb_ref, x_ref, p_ref, wa_ref, wb_ref, wo_ref,
                 pn_ref, wg_ref, wp_ref, y_ref):
    a = _dot(oa_ref[...].astype(bf16), wa_ref[...])
    b = _dot(ob_ref[...].astype(bf16), wb_ref[...])
    merged = _sigmoid(mga_ref[...]) * a + _sigmoid(mgb_ref[...]) * b
    x2 = x_ref[...] + _dot(merged.astype(bf16), wo_ref[...])
    ms = jnp.mean(x2 * x2, axis=-1, keepdims=True)
    hn = (x2 * lax.rsqrt(ms + EPS) * pn_ref[...]).astype(bf16)
    gate = _sigmoid(_dot(hn, wg_ref[...]))
    y_ref[...] = x2 + gate * _dot(p_ref[...].astype(bf16), wp_ref[...])


def _tail(oa, ob, z2d, x2d, p2d, wa, wb, wo, pn_row, wg, wp, tm):
    m = x2d.shape[0]
    const = lambda a: pl.BlockSpec(a.shape, lambda i: (0, 0), pipeline_mode=pl.Buffered(1))
    return pl.pallas_call(
        _tail_kernel,
        grid=(m // tm,),
        in_specs=[
            pl.BlockSpec((tm, GDN_WIDTH), lambda i: (i, 0)),
            pl.BlockSpec((tm, ATT_WIDTH), lambda i: (i, 0)),
            pl.BlockSpec((tm, D_MODEL), lambda i: (i, C_MGA // D_MODEL)),
            pl.BlockSpec((tm, D_MODEL), lambda i: (i, C_MGB // D_MODEL)),
            pl.BlockSpec((tm, D_MODEL), lambda i: (i, 0)),
            pl.BlockSpec((tm, PLE_DIM), lambda i: (i, 0)),
            const(wa), const(wb), const(wo), const(pn_row), const(wg), const(wp),
        ],
        out_specs=pl.BlockSpec((tm, D_MODEL), lambda i: (i, 0)),
        out_shape=jax.ShapeDtypeStruct((m, D_MODEL), f32),
        compiler_params=_cparams(("parallel",)),
    )(oa, ob, z2d, z2d, x2d, p2d, wa, wb, wo, pn_row, wg, wp)


def _lane_row(v, offset=0):
    row = jnp.zeros((1, LANES), f32)
    return row.at[0, offset:offset + v.shape[0]].set(v.astype(f32))


def _rope_tables(pos):
    half = ATT_DH // 2
    inv = ROPE_THETA ** (-jnp.arange(half, dtype=f32) * 2.0 / ATT_DH)
    ang = pos.astype(f32)[:, None] * inv[None, :]
    cos, sin = jnp.cos(ang), jnp.sin(ang)
    return jnp.concatenate([cos, cos], axis=1), jnp.concatenate([-sin, sin], axis=1)


def kernel(x_prompt, x_sample, p_prompt, p_sample, cache_k, cache_v, cache_idx_k, state_gdn, state_conv,
           page_table, norm_in, w_in, conv_w, a_log, dt_bias, gdn_norm, q_norm, k_norm,
           w_proj_a, w_proj_b, w_out, ple_norm, w_ple_gate, w_ple_proj):
    depth = norm_in.shape[0]
    assert depth == 1, "single-layer trunk"
    B, T, _ = x_prompt.shape
    DB, S_new, _ = x_sample.shape
    assert S_new == 1
    n_pages = page_table.shape[1]
    n_past = n_pages * PAGE_SIZE
    n_pool = cache_k.shape[1]
    li = 0

    assert w_in.shape[2] == D_IN
    w_perm = _wprep(w_in[li], tk=128)
    gain_in = norm_in[li].reshape(1, D_MODEL)
    a_row = _lane_row(a_log[li], SM_GA)
    dt_row = _lane_row(dt_bias[li], SM_GA)
    gdn_norm_row = gdn_norm[li].reshape(1, LANES)
    qn_row = q_norm[li].reshape(1, LANES)
    kn_row = k_norm[li].reshape(1, LANES)
    wa = w_proj_a[li].astype(bf16)
    wb = w_proj_b[li].astype(bf16)
    wo = w_out[li].astype(bf16)
    wg = w_ple_gate[li].astype(bf16)
    wp = w_ple_proj[li].astype(bf16)
    pn_row = ple_norm[li].reshape(1, D_MODEL)
    cw = conv_w[li]

    xp2 = x_prompt.reshape(B * T, D_MODEL)
    zp = _inproj(xp2, gain_in, w_perm, tm=1024, tn=768)
    zp3 = zp.reshape(B, T, NZ)

    prev8 = jnp.zeros((B, 8, CONV_CH), f32)
    qa, ka, va, gcb, btb, bg = _gdn_prep(zp3, prev8, cw, a_row, dt_row, tm=256)
    gc_row = jnp.swapaxes(bg[..., SM_GA:SM_GA + GDN_HEADS], 1, 2).reshape(
        B, GDN_HEADS, T // GDN_CHUNK, 1, GDN_CHUNK)
    s00 = jnp.zeros((B, GDN_HEADS, GDN_DK, GDN_DV), f32)
    oa_p, s_p = _gdn_scan(qa, ka, va, zp3, gcb, btb, gc_row, s00, gdn_norm_row)
    conv_p = zp3[:, T - (CONV_W - 1):, C_GQ:C_GQ + CONV_CH]

    cos_p, sin_p = _rope_tables(jnp.arange(T))
    qb_p, kf_p, v_p, kb_p, vt_p, qi_p, kif_p, kib_p, wt_p = _dsa_prep(
        zp3, cos_p, sin_p, qn_row, kn_row, tm=256, transposed=True, kc=ATT_KC,
        q_scale=LOG2E * ATT_DH ** -0.5)
    ob_p = _dsa_attend(qi_p, wt_p, kib_p, qb_p, kb_p, vt_p, zp3)

    y_p = _tail(oa_p.reshape(B * T, GDN_WIDTH), ob_p.reshape(B * T, ATT_WIDTH), zp, xp2,
                p_prompt[li].reshape(B * T, PLE_DIM), wa, wb, wo, pn_row, wg, wp, tm=256)

    xs2 = x_sample.reshape(DB, D_MODEL)
    zs = _inproj(xs2, gain_in, w_perm, tm=DB, tn=NZ // 6)
    sconv = state_conv[li]
    qs, ks, vs, bgs = _gdn_prep_sample(zs, sconv[:, 0], sconv[:, 1], sconv[:, 2], cw, a_row, dt_row)
    oa_s, s_s = _gdn_step(qs, ks, vs, zs, bgs, state_gdn[li], gdn_norm_row)
    conv_s = jnp.concatenate([sconv[:, 1:], zs[:, None, C_GQ:C_GQ + CONV_CH]], axis=1)

    cos_s, sin_s = _rope_tables(jnp.full((DB,), n_past))
    qf_s, kf_s, v_s, qi_s, kif_s, kib_s = _dsa_prep(
        zs.reshape(1, DB, NZ), cos_s, sin_s, qn_row, kn_row, tm=DB, transposed=False, kc=ATT_KC,
        q_scale=ATT_DH ** -0.5)
    wi_s = zs[:, C_SMALL + SM_IW:C_SMALL + SM_IW + IDX_HEADS] * (IDX_HEADS ** -0.5 * IDX_DH ** -0.5)
    qi3 = qi_s.reshape(DB, IDX_HEADS, IDX_DH)
    w3 = wi_s.reshape(DB, IDX_HEADS, 1)
    pt_flat = page_table.reshape(-1).astype(i32)
    sc = _sample_scores(pt_flat, qi3, w3, cache_idx_k[li])
    ktop = min(TOPK_MAX, (n_past + S_new) // 4)
    idx, rows = _sample_select(sc.reshape(DB, n_pages, PAGE_SIZE), qi3, kib_s.reshape(DB, 1, IDX_DH), w3,
                               page_table.astype(f32).reshape(DB, 1, n_pages), n_past, ktop)
    ob_s = _sample_attend(rows.reshape(-1), idx.reshape(DB, 1, ktop),
                          qf_s.reshape(DB, 1, ATT_WIDTH), kf_s.reshape(DB, 1, ATT_WIDTH),
                          v_s.reshape(DB, 1, ATT_WIDTH), zs[:, C_AZ:C_AZ + ATT_WIDTH].reshape(DB, 1, ATT_WIDTH),
                          cache_k[li].reshape(n_pool * PAGE_SIZE, ATT_HEADS, ATT_DH),
                          cache_v[li].reshape(n_pool * PAGE_SIZE, ATT_HEADS, ATT_DH), n_past, ktop)
    y_s = _tail(oa_s.reshape(DB, GDN_WIDTH), ob_s.reshape(DB, ATT_WIDTH), zs, xs2,
                p_sample[li].reshape(DB, PLE_DIM), wa, wb, wo, pn_row, wg, wp, tm=DB)

    shp = (B, T, ATT_HEADS, ATT_DH)
    shs = (DB, S_new, ATT_HEADS, ATT_DH)
    return (y_p.reshape(B, T, D_MODEL), y_s.reshape(DB, S_new, D_MODEL),
            kf_p.reshape(shp)[None], v_p.reshape(shp)[None], kif_p[None],
            s_p[None], conv_p[None],
            kf_s.reshape(shs)[None], v_s.reshape(shs)[None], kif_s.reshape(DB, S_new, IDX_DH)[None],
            s_s[None], conv_s[None])
```

```python
import functools
import math

import jax
import jax.numpy as jnp
from jax import lax
from jax.experimental import pallas as pl
from jax.experimental.pallas import tpu as pltpu

f32 = jnp.float32
bf16 = jnp.bfloat16
i32 = jnp.int32

D_MODEL = 2048
PAGE_SIZE = 128
GDN_HEADS = 8
GDN_DK = 128
GDN_DV = 128
GDN_WIDTH = GDN_HEADS * GDN_DV
CONV_W = 4
CONV_CH = 2 * GDN_HEADS * GDN_DK + GDN_WIDTH
GDN_CHUNK = 64
ATT_HEADS = 8
ATT_DH = 128
ATT_WIDTH = ATT_HEADS * ATT_DH
IDX_HEADS = 16
IDX_DH = 128
TOPK_MAX = 256
QUERY_BLOCK = 128
ROPE_THETA = 10000.0
PLE_DIM = 256
EPS = 1e-6

LANES = 128
VMEM_LIMIT = 56 * 1024 * 1024
NEG_BIG = -1e30
INT_MIN = -(2 ** 31)
KEY_MOST_NEGATIVE = INT_MIN + 0x00800000

HK = GDN_HEADS * GDN_DK
C_GQ, C_GK, C_GV, C_GZ = 0, HK, 2 * HK, 3 * HK
C_AQ = 4 * HK
C_AK = C_AQ + ATT_WIDTH
C_AV = C_AK + ATT_WIDTH
C_AZ = C_AV + ATT_WIDTH
C_IQ = C_AZ + ATT_WIDTH
C_MGA = C_IQ + IDX_HEADS * IDX_DH
C_MGB = C_MGA + D_MODEL
C_IK = C_MGB + D_MODEL
C_SMALL = C_IK + IDX_DH
NZ = C_SMALL + LANES
SM_GA, SM_GB, SM_IW = 0, GDN_HEADS, 2 * GDN_HEADS


def _cparams(sem):
    return pltpu.CompilerParams(dimension_semantics=sem, vmem_limit_bytes=VMEM_LIMIT)


def _dot(a, b):
    return jnp.dot(a, b, preferred_element_type=f32)


def _dot_nt(a, b):
    return lax.dot_general(a, b, (((1,), (1,)), ((), ())), preferred_element_type=f32)


def _sigmoid(x):
    return 1.0 / (1.0 + jnp.exp(-x))


def _silu(x):
    return x * _sigmoid(x)


O_GA = 4 * HK
O_AQ = O_GA + 2 * GDN_HEADS
O_IQ = O_AQ + 4 * ATT_WIDTH
O_IK = O_IQ + IDX_HEADS * IDX_DH
O_IW = O_IK + IDX_DH
O_MGA = O_IW + IDX_HEADS
D_IN = O_MGA + 2 * D_MODEL


IW_ROWS = 2 * SM_IW


def _wprep_src_rows():
    rows = []
    for j in range(NZ // LANES):
        c = j * LANES
        if c < C_AQ:
            rows.append(c)
        elif c < C_MGA:
            rows.append(c - C_AQ + O_AQ)
        elif c < C_IK:
            rows.append(c - C_MGA + O_MGA)
        elif c < C_SMALL:
            rows.append(O_IK)
        else:
            rows.append(O_GA)
    return jnp.asarray(rows, i32)


def _wprep_kernel(src_ref, a_ref, b_ref, o_ref):
    j = pl.program_id(0)
    a = a_ref[...]
    row = lax.broadcasted_iota(i32, (IW_ROWS, a.shape[1]), 0)
    head = jnp.where(row < SM_IW, a[0:IW_ROWS], b_ref[...])
    small = jnp.concatenate([head, jnp.zeros((LANES - IW_ROWS, a.shape[1]), f32)], axis=0)
    tile = jnp.where(j == pl.num_programs(0) - 1, small, a)
    o_ref[...] = tile.T.astype(bf16)


def _wprep(wt):
    assert O_GA % LANES == 0 and O_IW % IW_ROWS == SM_IW and 2 * GDN_HEADS == SM_IW
    k = wt.shape[1]
    grid_spec = pltpu.PrefetchScalarGridSpec(
        num_scalar_prefetch=1,
        grid=(NZ // LANES,),
        in_specs=[pl.BlockSpec((pl.Element(LANES), pl.Element(k)), lambda j, src: (pl.multiple_of(src[j], SM_IW), 0)),
                  pl.BlockSpec((IW_ROWS, k), lambda j, src: (O_IW // IW_ROWS, 0))],
        out_specs=pl.BlockSpec((k, LANES), lambda j, src: (0, j)),
    )
    return pl.pallas_call(
        _wprep_kernel,
        grid_spec=grid_spec,
        out_shape=jax.ShapeDtypeStruct((k, NZ), bf16),
        compiler_params=_cparams(("arbitrary",)),
    )(_wprep_src_rows(), wt, wt)


def _inproj_kernel(x_ref, g_ref, w_ref, o_ref, h_ref):
    @pl.when(pl.program_id(1) == 0)
    def _():
        x = x_ref[...]
        ms = jnp.mean(x * x, axis=-1, keepdims=True)
        h_ref[...] = (x * lax.rsqrt(ms + EPS) * g_ref[...]).astype(bf16)

    o_ref[...] = _dot(h_ref[...], w_ref[...])


def _inproj(x2d, gain_row, w_bf16, tm, tn):
    m, k = x2d.shape
    n = w_bf16.shape[1]
    return pl.pallas_call(
        _inproj_kernel,
        grid=(m // tm, n // tn),
        in_specs=[
            pl.BlockSpec((tm, k), lambda i, j: (i, 0)),
            pl.BlockSpec((1, k), lambda i, j: (0, 0)),
            pl.BlockSpec((k, tn), lambda i, j: (0, j)),
        ],
        out_specs=pl.BlockSpec((tm, tn), lambda i, j: (i, j)),
        out_shape=jax.ShapeDtypeStruct((m, n), f32),
        scratch_shapes=[pltpu.VMEM((tm, k), bf16)],
        compiler_params=_cparams(("parallel", "arbitrary")),
    )(x2d, gain_row, w_bf16)


def _softplus(x):
    return jnp.maximum(x, 0.0) + jnp.log1p(jnp.exp(-jnp.abs(x)))


def _gdn_heads_out(conv_fn, q_ref, k_ref, v_ref):
    for c in range(3 * GDN_HEADS):
        a = _silu(conv_fn(c))
        h = c % GDN_HEADS
        sl = slice(h * LANES, (h + 1) * LANES)
        if c < 2 * GDN_HEADS:
            nrm = lax.rsqrt(jnp.sum(a * a, axis=-1, keepdims=True) + EPS)
            if c < GDN_HEADS:
                q_ref[:, sl] = a * nrm * (GDN_DK ** -0.5)
            else:
                k_ref[:, sl] = a * nrm
        else:
            v_ref[:, sl] = a


def _gdn_prep_kernel(x_ref, halo_ref, prev_ref, cw_ref, sm_ref, a_ref, dt_ref,
                     q_ref, k_ref, v_ref, gcb_ref, btb_ref, bg_ref, xe_ref, *, tm):
    i = pl.program_id(1)
    xe_ref[pl.ds(8, tm), :] = x_ref[0]

    @pl.when(i == 0)
    def _():
        xe_ref[pl.ds(0, 8), :] = prev_ref[0]

    @pl.when(i > 0)
    def _():
        xe_ref[pl.ds(0, 8), :] = halo_ref[0]

    def conv_fn(c):
        sl = slice(c * LANES, (c + 1) * LANES)
        acc = xe_ref[pl.ds(8 - (CONV_W - 1), tm), sl] * cw_ref[0:1, sl]
        for j in range(1, CONV_W):
            acc = acc + xe_ref[pl.ds(8 - (CONV_W - 1) + j, tm), sl] * cw_ref[j:j + 1, sl]
        return acc

    _gdn_heads_out(conv_fn, q_ref.at[0], k_ref.at[0], v_ref.at[0])

    sm = sm_ref[0]
    g = -jnp.exp(a_ref[...]) * _softplus(sm + dt_ref[...])
    row = lax.broadcasted_iota(i32, (tm, LANES), 0) % GDN_CHUNK
    s = 1
    while s < GDN_CHUNK:
        g = g + jnp.where(row >= s, pltpu.roll(g, s, axis=0), 0.0)
        s *= 2
    lane = lax.broadcasted_iota(i32, (tm, LANES), 1)
    bt = _sigmoid(sm)
    bg_ref[0] = jnp.where(lane < GDN_HEADS, g, bt)
    for h in range(GDN_HEADS):
        sl = slice(h * LANES, (h + 1) * LANES)
        gcb_ref[0, :, sl] = jnp.broadcast_to(g[:, SM_GA + h:SM_GA + h + 1], (tm, LANES))
        btb_ref[0, :, sl] = jnp.broadcast_to(bt[:, SM_GB + h:SM_GB + h + 1], (tm, LANES))


def _gdn_prep(z3, prev8, conv_w, a_row, dt_row, tm):
    b, t, _ = z3.shape
    nblk = tm // 8
    kern = functools.partial(_gdn_prep_kernel, tm=tm)
    tok = pl.BlockSpec((1, tm, HK), lambda bi, i: (bi, i, 0))
    out_sd = jax.ShapeDtypeStruct((b, t, HK), f32)
    return pl.pallas_call(
        kern,
        grid=(b, t // tm),
        in_specs=[
            pl.BlockSpec((1, tm, CONV_CH), lambda bi, i: (bi, i, 0)),
            pl.BlockSpec((1, 8, CONV_CH), lambda bi, i: (bi, jnp.maximum(i * nblk - 1, 0), 0)),
            pl.BlockSpec((1, 8, CONV_CH), lambda bi, i: (bi, 0, 0)),
            pl.BlockSpec((CONV_W, CONV_CH), lambda bi, i: (0, 0)),
            pl.BlockSpec((1, tm, LANES), lambda bi, i: (bi, i, C_SMALL // LANES)),
            pl.BlockSpec((1, LANES), lambda bi, i: (0, 0)),
            pl.BlockSpec((1, LANES), lambda bi, i: (0, 0)),
        ],
        out_specs=[tok] * 5 + [pl.BlockSpec((1, tm, LANES), lambda bi, i: (bi, i, 0))],
        out_shape=[out_sd] * 5 + [jax.ShapeDtypeStruct((b, t, LANES), f32)],
        scratch_shapes=[pltpu.VMEM((tm + 8, CONV_CH), f32)],
        compiler_params=_cparams(("parallel", "arbitrary")),
    )(z3, z3, prev8, conv_w, z3, a_row, dt_row)


def _gdn_prep_sample_kernel(x_ref, s0_ref, s1_ref, s2_ref, cw_ref, sm_ref, a_ref, dt_ref,
                            q_ref, k_ref, v_ref, bg_ref):
    def conv_fn(c):
        sl = slice(c * LANES, (c + 1) * LANES)
        return (s0_ref[:, sl] * cw_ref[0:1, sl] + s1_ref[:, sl] * cw_ref[1:2, sl]
                + s2_ref[:, sl] * cw_ref[2:3, sl] + x_ref[:, sl] * cw_ref[3:4, sl])

    _gdn_heads_out(conv_fn, q_ref, k_ref, v_ref)
    sm = sm_ref[...]
    g = -jnp.exp(a_ref[...]) * _softplus(sm + dt_ref[...])
    lane = lax.broadcasted_iota(i32, sm.shape, 1)
    bg_ref[...] = jnp.where(lane < GDN_HEADS, g, _sigmoid(sm))


def _gdn_prep_sample(zs, sc0, sc1, sc2, conv_w, a_row, dt_row):
    db = zs.shape[0]
    full = lambda shape: pl.BlockSpec(shape, lambda i: (0,) * len(shape))
    out_sd = jax.ShapeDtypeStruct((db, HK), f32)
    return pl.pallas_call(
        _gdn_prep_sample_kernel,
        grid=(1,),
        in_specs=[
            pl.BlockSpec((db, CONV_CH), lambda i: (0, 0)),
            full((db, CONV_CH)), full((db, CONV_CH)), full((db, CONV_CH)),
            full((CONV_W, CONV_CH)),
            pl.BlockSpec((db, LANES), lambda i: (0, C_SMALL // LANES)),
            full((1, LANES)), full((1, LANES)),
        ],
        out_specs=[full((db, HK)), full((db, HK)), full((db, HK)), full((db, LANES))],
        out_shape=[out_sd, out_sd, out_sd, jax.ShapeDtypeStruct((db, LANES), f32)],
        compiler_params=_cparams(("arbitrary",)),
    )(zs, sc0, sc1, sc2, conv_w, zs, a_row, dt_row)


GDN_TS = 512
GDN_HB = 8


def _bmm(a, b):
    return jnp.einsum('bij,bjk->bik', a, b, preferred_element_type=f32)


def _bmm_nt(a, b):
    return jnp.einsum('bid,bjd->bij', a, b, preferred_element_type=f32)


def _gdn_scan_kernel(q_ref, k_ref, v_ref, gz_ref, gcb_ref, btb_ref, gr_ref, s0_ref, nrm_ref,
                     o_ref, so_ref, s_ref):
    step = pl.program_id(2)
    nc = GDN_TS // GDN_CHUNK
    c = GDN_CHUNK

    @pl.when(step == 0)
    def _():
        s_ref[...] = s0_ref[0]

    ri = lax.broadcasted_iota(i32, (1, c, c), 1)
    ci = lax.broadcasted_iota(i32, (1, c, c), 2)
    tri = ri >= ci
    strict = ri > ci
    eye = (ri == ci).astype(f32)

    hb = GDN_HB

    def stack(ref):
        return jnp.concatenate(
            [ref[0, :, hh * LANES:(hh + 1) * LANES].reshape(nc, c, LANES) for hh in range(hb)], axis=0)

    q, k, v = stack(q_ref), stack(k_ref), stack(v_ref)
    gcb = stack(gcb_ref)
    bt = stack(btb_ref)
    gr = gr_ref[0].reshape(hb * nc, 1, c)
    glast = gr[:, :, c - 1:c]
    decay = jnp.exp(jnp.where(tri, gcb[:, :, 0:c] - gr, -jnp.inf))
    kb = k * bt
    m = jnp.where(strict, _bmm_nt(kb, k) * decay, 0.0)
    x = eye - m
    p = m
    for _ in range(5):
        p = _bmm(p, p)
        x = x + _bmm(x, p)
    eg = jnp.exp(gcb)
    by_chunk = lambda a: a.reshape((hb, nc) + a.shape[1:])
    u = by_chunk(_bmm(x, v * bt))
    w = by_chunk(_bmm(x, kb * eg))
    qk = by_chunk(_bmm_nt(q, k) * decay)
    qg = by_chunk(q * eg)
    kg = by_chunk(k * jnp.exp(glast - gcb))
    gl = by_chunk(jnp.exp(glast))

    gain = nrm_ref[...]
    s = s_ref[...]
    for ch in range(nc):
        v_new = u[:, ch] - _bmm(w[:, ch], s)
        o = _bmm(qg[:, ch], s) + _bmm(qk[:, ch], v_new)
        s = s * gl[:, ch] + jnp.einsum('hck,hcv->hkv', kg[:, ch], v_new, preferred_element_type=f32)
        ms = jnp.mean(o * o, axis=-1, keepdims=True)
        on = o * lax.rsqrt(ms + EPS) * gain
        rows = pl.ds(ch * c, c)
        for hh in range(hb):
            sl = slice(hh * LANES, (hh + 1) * LANES)
            o_ref[0, rows, sl] = (on[hh] * _silu(gz_ref[0, rows, sl])).astype(o_ref.dtype)
    s_ref[...] = s

    @pl.when(step == pl.num_programs(2) - 1)
    def _():
        so_ref[0] = s_ref[...]


def _gdn_scan(q, k, v, z3, gcb, btb, gc_row, s0, gdn_norm_row):
    b, t, _ = q.shape
    hw = GDN_HB * LANES
    nc = GDN_TS // GDN_CHUNK
    qspec = pl.BlockSpec((1, GDN_TS, hw), lambda bi, hg, i: (bi, i, hg))
    return pl.pallas_call(
        _gdn_scan_kernel,
        grid=(b, GDN_HEADS // GDN_HB, t // GDN_TS),
        in_specs=[
            qspec, qspec, qspec,
            pl.BlockSpec((1, GDN_TS, hw), lambda bi, hg, i: (bi, i, C_GZ // hw + hg)),
            qspec, qspec,
            pl.BlockSpec((1, GDN_HB, nc, 1, GDN_CHUNK), lambda bi, hg, i: (bi, hg, i, 0, 0)),
            pl.BlockSpec((1, GDN_HB, GDN_DK, GDN_DV), lambda bi, hg, i: (bi, hg, 0, 0)),
            pl.BlockSpec((1, LANES), lambda bi, hg, i: (0, 0)),
        ],
        out_specs=[
            pl.BlockSpec((1, GDN_TS, hw), lambda bi, hg, i: (bi, i, hg)),
            pl.BlockSpec((1, GDN_HB, GDN_DK, GDN_DV), lambda bi, hg, i: (bi, hg, 0, 0)),
        ],
        out_shape=[jax.ShapeDtypeStruct((b, t, GDN_WIDTH), bf16),
                   jax.ShapeDtypeStruct((b, GDN_HEADS, GDN_DK, GDN_DV), f32)],
        scratch_shapes=[pltpu.VMEM((GDN_HB, GDN_DK, GDN_DV), f32)],
        compiler_params=_cparams(("parallel", "parallel", "arbitrary")),
    )(q, k, v, z3, gcb, btb, gc_row, s0, gdn_norm_row)


GS_BB = 8


def _gdn_step_kernel(q_ref, k_ref, v_ref, gz_ref, bg_ref, s0_ref, nrm_ref, o_ref, so_ref):
    for bb in range(GS_BB):
        row = slice(bb, bb + 1)
        for h in range(GDN_HEADS):
            sl = slice(h * LANES, (h + 1) * LANES)
            s0 = s0_ref[bb, h]
            q = q_ref[row, sl]
            k = k_ref[row, sl]
            v = v_ref[row, sl]
            eg = jnp.exp(bg_ref[row, SM_GA + h:SM_GA + h + 1])
            bt = bg_ref[row, SM_GB + h:SM_GB + h + 1]
            kc = jnp.broadcast_to(k, (GDN_DK, LANES)).T
            qc = jnp.broadcast_to(q, (GDN_DK, LANES)).T
            ks = jnp.sum(kc * s0, axis=0, keepdims=True)
            qs = jnp.sum(qc * s0, axis=0, keepdims=True)
            qk = jnp.sum(q * k, axis=-1, keepdims=True)
            v_new = bt * v - (bt * eg) * ks
            o = eg * qs + qk * v_new
            so_ref[bb, h] = s0 * eg + kc * v_new
            ms = jnp.mean(o * o, axis=-1, keepdims=True)
            on = o * lax.rsqrt(ms + EPS) * nrm_ref[...]
            o_ref[row, sl] = on * _silu(gz_ref[row, sl])


def _gdn_step(q, k, v, zs, bgs, s0, gdn_norm_row):
    db = q.shape[0]
    tok = pl.BlockSpec((GS_BB, HK), lambda i: (i, 0))
    st = pl.BlockSpec((GS_BB, GDN_HEADS, GDN_DK, GDN_DV), lambda i: (i, 0, 0, 0))
    return pl.pallas_call(
        _gdn_step_kernel,
        grid=(db // GS_BB,),
        in_specs=[tok, tok, tok, pl.BlockSpec((GS_BB, GDN_WIDTH), lambda i: (i, C_GZ // GDN_WIDTH)),
                  pl.BlockSpec((GS_BB, LANES), lambda i: (i, 0)), st,
                  pl.BlockSpec((1, LANES), lambda i: (0, 0))],
        out_specs=[tok, st],
        out_shape=[jax.ShapeDtypeStruct((db, GDN_WIDTH), f32),
                   jax.ShapeDtypeStruct((db, GDN_HEADS, GDN_DK, GDN_DV), f32)],
        compiler_params=_cparams(("parallel",)),
    )(q, k, v, zs, bgs, s0, gdn_norm_row)


def _rope(x, cos, sin_signed):
    return x * cos + pltpu.roll(x, LANES // 2, axis=1) * sin_signed


def _dsa_prep_kernel(aq_ref, ak_ref, av_ref, iq_ref, ik_ref, sm_ref, cos_ref, sin_ref,
                     qn_ref, kn_ref, *out_refs, transposed, q_scale):
    if transposed:
        qb_ref, kf_ref, vf_ref, kb_ref, vt_ref, qi_ref, kif_ref, kib_ref, wt_ref = out_refs
    else:
        qb_ref, kf_ref, vf_ref, qi_ref, kif_ref, kib_ref = out_refs
    cos = cos_ref[...]
    sin = sin_ref[...]
    for h in range(ATT_HEADS):
        sl = slice(h * LANES, (h + 1) * LANES)
        a = aq_ref[0, :, sl]
        a = a * lax.rsqrt(jnp.mean(a * a, axis=-1, keepdims=True) + EPS) * qn_ref[...]
        qb_ref[0, :, sl] = (_rope(a, cos, sin) * q_scale).astype(qb_ref.dtype)
        a = ak_ref[0, :, sl]
        a = a * lax.rsqrt(jnp.mean(a * a, axis=-1, keepdims=True) + EPS) * kn_ref[...]
        kr = _rope(a, cos, sin)
        kf_ref[0, :, sl] = kr
        v = av_ref[0, :, sl]
        vf_ref[0, :, sl] = v
        if transposed:
            kb_ref[0, :, sl] = kr.astype(bf16)
            vt_ref[0, 0, sl, :] = v.T.astype(bf16)
    for h in range(IDX_HEADS):
        sl = slice(h * LANES, (h + 1) * LANES)
        qi_ref[0, :, sl] = _rope(iq_ref[0, :, sl], cos, sin).astype(bf16)
    ki = _rope(ik_ref[0], cos, sin)
    kif_ref[0] = ki
    kib_ref[0] = ki.astype(bf16)
    if transposed:
        wt_ref[0] = (sm_ref[0] * (IDX_HEADS ** -0.5 * IDX_DH ** -0.5)).T


def _dsa_prep(z3, cos, sin, qn_row, kn_row, tm, transposed, kc, q_scale):
    b, t, _ = z3.shape
    zspec = lambda w, off: pl.BlockSpec((1, tm, w), lambda bi, i: (bi, i, off // w))
    tok = lambda w, dt: (pl.BlockSpec((1, tm, w), lambda bi, i: (bi, i, 0)),
                         jax.ShapeDtypeStruct((b, t, w), dt))
    if transposed:
        per = kc // tm
        outs = [tok(ATT_WIDTH, bf16), tok(ATT_WIDTH, f32), tok(ATT_WIDTH, f32), tok(ATT_WIDTH, bf16),
                (pl.BlockSpec((1, 1, ATT_WIDTH, tm), lambda bi, i: (bi, i // per, 0, i % per)),
                 jax.ShapeDtypeStruct((b, t // kc, ATT_WIDTH, kc), bf16)),
                tok(IDX_HEADS * IDX_DH, bf16), tok(IDX_DH, f32), tok(IDX_DH, bf16),
                (pl.BlockSpec((1, LANES, tm), lambda bi, i: (bi, 0, i)),
                 jax.ShapeDtypeStruct((b, LANES, t), f32))]
    else:
        outs = [tok(ATT_WIDTH, f32), tok(ATT_WIDTH, f32), tok(ATT_WIDTH, f32),
                tok(IDX_HEADS * IDX_DH, bf16), tok(IDX_DH, f32), tok(IDX_DH, bf16)]
    kern = functools.partial(_dsa_prep_kernel, transposed=transposed, q_scale=q_scale)
    return pl.pallas_call(
        kern,
        grid=(b, t // tm),
        in_specs=[
            zspec(ATT_WIDTH, C_AQ), zspec(ATT_WIDTH, C_AK), zspec(ATT_WIDTH, C_AV),
            zspec(IDX_HEADS * IDX_DH, C_IQ), zspec(IDX_DH, C_IK), zspec(LANES, C_SMALL),
            pl.BlockSpec((tm, LANES), lambda bi, i: (i, 0)),
            pl.BlockSpec((tm, LANES), lambda bi, i: (i, 0)),
            pl.BlockSpec((1, LANES), lambda bi, i: (0, 0)),
            pl.BlockSpec((1, LANES), lambda bi, i: (0, 0)),
        ],
        out_specs=[o[0] for o in outs],
        out_shape=[o[1] for o in outs],
        compiler_params=_cparams(("parallel", "parallel")),
    )(z3, z3, z3, z3, z3, z3, cos, sin, qn_row, kn_row)


def _key_to_float(key):
    bits = jnp.where(key >= 0, key, key ^ jnp.int32(0x7FFFFFFF))
    return pltpu.bitcast(bits, f32)


def _kth_largest(count_ge, k, shape):
    kf = jnp.float32(k)
    zero_ok = count_ge(jnp.zeros(shape, f32)) >= kf
    cur = jnp.where(zero_ok, jnp.int32(0), jnp.int32(INT_MIN))

    def body(it, cur):
        cand = cur + jnp.left_shift(jnp.int32(1), jnp.int32(30) - it)
        ok = count_ge(_key_to_float(cand)) >= kf
        return jnp.where(ok, cand, cur)

    cur = lax.fori_loop(0, 31, body, cur)
    return _key_to_float(cur)


def _tie_bound(count_eq_le, need, nbits, shape):
    lo = jnp.full(shape, -1, i32)

    def body(it, lo):
        cand = lo + jnp.left_shift(jnp.int32(1), jnp.int32(nbits - 1) - it)
        short = count_eq_le(cand) < need
        return jnp.where(short, cand, lo)

    lo = lax.fori_loop(0, nbits, body, lo)
    return lo + 1


ATT_KC = 512
ATT_QB = 256
ATT_KS = 128
CNT_ROWS = 64
LOG2E = 1.4426950408889634


def _dsa_attend_kernel(qi_ref, wt_ref, ki_ref, qb_ref, kb_ref, vt_ref, az_ref, o_ref,
                       sc_ref, sc16_ref, *head_refs, t_total):
    i = pl.program_id(1)
    qb = ATT_QB
    kc = ATT_KC
    nch = (i * qb + qb + kc - 1) // kc
    tq = i * qb + lax.broadcasted_iota(i32, (1, qb), 1)
    ktop = min(TOPK_MAX, t_total // 4)

    def spos(c):
        return c * kc + lax.broadcasted_iota(i32, (kc, 1), 0)

    def idx_body(c, carry):
        for sub in range(kc // ATT_KS):
            base = c * kc + sub * ATT_KS
            ksub = ki_ref[0, pl.ds(base, ATT_KS), :]
            acc = jnp.zeros((ATT_KS, qb), f32)
            for h in range(IDX_HEADS):
                d = _dot_nt(ksub, qi_ref[0, :, h * LANES:(h + 1) * LANES])
                acc = acc + jnp.maximum(d, 0.0) * wt_ref[0, SM_IW + h:SM_IW + h + 1, :]
            sp = base + lax.broadcasted_iota(i32, (ATT_KS, 1), 0)
            masked = jnp.where(sp <= tq, acc, -jnp.inf)
            sc_ref[pl.ds(base, ATT_KS), :] = masked
            sc16_ref[pl.ds(base, ATT_KS), :] = masked.astype(bf16)
        return carry

    lax.fori_loop(0, nch, idx_body, 0)

    def col_count(pred_fn):
        def body(c, acc):
            m = pred_fn(sc_ref[pl.ds(c * kc, kc), :], c)
            return acc + jnp.sum(jnp.where(m, 1.0, 0.0).reshape(kc // CNT_ROWS, CNT_ROWS, qb), axis=0)
        acc = lax.fori_loop(0, nch, body, jnp.zeros((CNT_ROWS, qb), f32))
        return jnp.sum(acc, axis=0, keepdims=True)

    def col_count16(cand):
        cand16 = cand.astype(bf16)

        def body(c, acc):
            one = jnp.where(sc16_ref[pl.ds(c * kc, kc), :] >= cand16, jnp.bfloat16(1.0), jnp.bfloat16(0.0))
            for r in range(kc // CNT_ROWS):
                acc = acc + one[r * CNT_ROWS:(r + 1) * CNT_ROWS]
            return acc
        acc = lax.fori_loop(0, nch, body, jnp.zeros((CNT_ROWS, qb), bf16))
        return jnp.sum(acc.astype(f32), axis=0, keepdims=True)

    def search():
        kf = jnp.float32(ktop)
        zero_ok = col_count16(jnp.zeros((1, qb), f32)) >= kf
        cur = jnp.where(zero_ok, jnp.int32(0), jnp.int32(INT_MIN))

        def body16(it, cur):
            cand = cur + jnp.left_shift(jnp.int32(1), jnp.int32(30) - it)
            return jnp.where(col_count16(_key_to_float(cand)) >= kf, cand, cur)

        cur = lax.fori_loop(0, 15, body16, cur)
        step = 1 << 16
        lo = jnp.maximum(jnp.maximum(cur, jnp.int32(INT_MIN + step)) - step, jnp.int32(KEY_MOST_NEGATIVE))

        def body32(it, carry):
            off, cnt = carry
            cand = off + jnp.left_shift(jnp.int32(1), jnp.int32(16) - it)
            cand_f = _key_to_float(lo + cand)
            c_new = col_count(lambda s, c: s >= cand_f)
            ok = c_new >= kf
            return jnp.where(ok, cand, off), jnp.where(ok, c_new, cnt)

        lo_f = _key_to_float(lo)
        c_lo = col_count(lambda s, c: s >= lo_f)
        off, c_ge = lax.fori_loop(0, 17, body32, (jnp.zeros((1, qb), i32), c_lo))
        thr = _key_to_float(lo + off)

        def tie_bound():
            need = kf - col_count(lambda s, c: s > thr)
            return _tie_bound(
                lambda cand: col_count(lambda s, c: (s == thr) & (spos(c) <= cand)),
                need, max(1, (t_total - 1).bit_length()), (1, qb))

        jb = lax.cond(jnp.max(c_ge) > kf, tie_bound, lambda: jnp.full((1, qb), t_total, i32))
        return thr, jb

    def take_all():
        return jnp.full((1, qb), -jnp.inf, f32), jnp.full((1, qb), t_total, i32)

    thr, jb = lax.cond((i + 1) * qb <= ktop, take_all, search)

    def bias_body(c, carry):
        s = sc_ref[pl.ds(c * kc, kc), :]
        sp = spos(c)
        sel = ((s > thr) | ((s == thr) & (sp <= jb))) & (sp <= tq)
        sc_ref[pl.ds(c * kc, kc), :] = jnp.where(sel, 0.0, NEG_BIG)
        return carry

    lax.fori_loop(0, nch, bias_body, 0)

    nh = ATT_HEADS
    m_refs, l_refs, al_refs, acc_refs, s_refs, p_refs = (head_refs[j * nh:(j + 1) * nh] for j in range(6))
    for h in range(nh):
        m_refs[h][...] = jnp.full(m_refs[h].shape, NEG_BIG, f32)
        l_refs[h][...] = jnp.zeros(l_refs[h].shape, f32)
        acc_refs[h][...] = jnp.zeros(acc_refs[h].shape, f32)

    def att_body(c, carry):
        for h in range(nh):
            sl = slice(h * LANES, (h + 1) * LANES)
            s_refs[h][...] = _dot_nt(kb_ref[0, pl.ds(c * kc, kc), sl], qb_ref[0, :, sl])
        for h in range(nh):
            s = s_refs[h][...] + sc_ref[pl.ds(c * kc, kc), :]
            m = m_refs[h][...]
            m_new = jnp.maximum(m, jnp.max(s, axis=0, keepdims=True))
            alpha = jnp.exp2(m - m_new)
            p = jnp.exp2(s - m_new)
            m_refs[h][...] = m_new
            al_refs[h][...] = alpha
            l_refs[h][...] = alpha * l_refs[h][...] + jnp.sum(p, axis=0, keepdims=True)
            p_refs[h][...] = p.astype(bf16)
        for h in range(nh):
            sl = slice(h * LANES, (h + 1) * LANES)
            acc_refs[h][...] = (al_refs[h][...] * acc_refs[h][...]
                                + _dot(vt_ref[0, c, sl, :], p_refs[h][...]))
        return carry

    lax.fori_loop(0, nch, att_body, 0)

    for h in range(ATT_HEADS):
        sl = slice(h * LANES, (h + 1) * LANES)
        out = (acc_refs[h][...] / l_refs[h][...]).T
        o_ref[0, :, sl] = (out * _silu(az_ref[0, :, sl])).astype(o_ref.dtype)


def _dsa_attend(qi, wt, ki, qb, kb, vt, z3):
    b, t, _ = qb.shape
    kern = functools.partial(_dsa_attend_kernel, t_total=t)
    return pl.pallas_call(
        kern,
        grid=(b, t // ATT_QB),
        in_specs=[
            pl.BlockSpec((1, ATT_QB, IDX_HEADS * IDX_DH), lambda bi, i: (bi, i, 0)),
            pl.BlockSpec((1, LANES, ATT_QB), lambda bi, i: (bi, 0, i)),
            pl.BlockSpec((1, t, IDX_DH), lambda bi, i: (bi, 0, 0)),
            pl.BlockSpec((1, ATT_QB, ATT_WIDTH), lambda bi, i: (bi, i, 0)),
            pl.BlockSpec((1, t, ATT_WIDTH), lambda bi, i: (bi, 0, 0)),
            pl.BlockSpec((1, t // ATT_KC, ATT_WIDTH, ATT_KC), lambda bi, i: (bi, 0, 0, 0)),
            pl.BlockSpec((1, ATT_QB, ATT_WIDTH), lambda bi, i: (bi, i, C_AZ // ATT_WIDTH)),
        ],
        out_specs=pl.BlockSpec((1, ATT_QB, ATT_WIDTH), lambda bi, i: (bi, i, 0)),
        out_shape=jax.ShapeDtypeStruct((b, t, ATT_WIDTH), bf16),
        scratch_shapes=([pltpu.VMEM((t, ATT_QB), f32), pltpu.VMEM((t, ATT_QB), bf16)]
                        + [pltpu.VMEM((1, ATT_QB), f32) for _ in range(3 * ATT_HEADS)]
                        + [pltpu.VMEM((LANES, ATT_QB), f32) for _ in range(ATT_HEADS)]
                        + [pltpu.VMEM((ATT_KC, ATT_QB), f32) for _ in range(ATT_HEADS)]
                        + [pltpu.VMEM((ATT_KC, ATT_QB), bf16) for _ in range(ATT_HEADS)]),
        compiler_params=_cparams(("parallel", "arbitrary")),
    )(qi, wt, ki, qb, kb, vt, z3)


SC_PG = 16
DMA_UNROLL = 8


def _sample_scores_kernel(pt_s, qi_ref, w_ref, cik_ref, o_ref, buf, sem, *, n_pages):
    b = pl.program_id(0)
    nb = pl.num_programs(0)

    def page_copy(bb, slot, p):
        return pltpu.make_async_copy(cik_ref.at[pt_s[bb * n_pages + p]], buf.at[slot, p], sem.at[slot])

    def issue(bb, slot):
        def body(p, carry):
            page_copy(bb, slot, p).start()
            return carry
        lax.fori_loop(0, n_pages, body, 0, unroll=DMA_UNROLL)

    def wait_all(bb, slot):
        def body(p, carry):
            page_copy(bb, slot, p).wait()
            return carry
        lax.fori_loop(0, n_pages, body, 0, unroll=DMA_UNROLL)

    slot = b % 2

    @pl.when(b == 0)
    def _():
        issue(b, slot)

    @pl.when(b + 1 < nb)
    def _():
        issue(b + 1, 1 - slot)

    wait_all(b, slot)

    qi = qi_ref[0]
    w = w_ref[0]
    gk = SC_PG * PAGE_SIZE
    for g in range(n_pages // SC_PG):
        keys = buf[slot, g * SC_PG:(g + 1) * SC_PG].reshape(gk, IDX_DH).astype(bf16)
        d = _dot_nt(qi, keys)
        o_ref[0, :, g * gk:(g + 1) * gk] = jnp.sum(jnp.maximum(d, 0.0) * w, axis=0, keepdims=True)


def _sample_scores(pt_flat, qi3, w3, cache_ik):
    db = qi3.shape[0]
    n_pages = pt_flat.shape[0] // db
    kern = functools.partial(_sample_scores_kernel, n_pages=n_pages)
    grid_spec = pltpu.PrefetchScalarGridSpec(
        num_scalar_prefetch=1,
        grid=(db,),
        in_specs=[pl.BlockSpec((1, IDX_HEADS, IDX_DH), lambda bi, pt: (bi, 0, 0)),
                  pl.BlockSpec((1, IDX_HEADS, 1), lambda bi, pt: (bi, 0, 0)),
                  pl.BlockSpec(memory_space=pl.ANY)],
        out_specs=pl.BlockSpec((1, 1, n_pages * PAGE_SIZE), lambda bi, pt: (bi, 0, 0)),
        scratch_shapes=[pltpu.VMEM((2, n_pages, PAGE_SIZE, IDX_DH), f32), pltpu.SemaphoreType.DMA((2,))],
    )
    return pl.pallas_call(
        kern,
        grid_spec=grid_spec,
        out_shape=jax.ShapeDtypeStruct((db, 1, n_pages * PAGE_SIZE), f32),
        compiler_params=_cparams(("arbitrary",)),
    )(pt_flat, qi3, w3, cache_ik)


def _sample_select_kernel(sc_ref, qi_ref, ki_ref, w_ref, pt_ref, idx_ref, rows_ref, *, n_past, ktop):
    db, n_pages, ps = sc_ref.shape
    sc = sc_ref[...]

    def red(x):
        return jnp.sum(jnp.sum(x, axis=2, keepdims=True), axis=1, keepdims=True)

    dn = jnp.sum(qi_ref[...].astype(f32) * ki_ref[...].astype(f32), axis=2, keepdims=True)
    s_new = jnp.sum(jnp.maximum(dn, 0.0) * w_ref[...], axis=1, keepdims=True)

    def count(pred):
        return red(pred(sc).astype(f32)) + pred(s_new).astype(f32)

    shape = (db, 1, 1)
    thr = _kth_largest(lambda cand: count(lambda s: s >= cand), ktop, shape)
    c_ge = count(lambda s: s >= thr)
    c_gt = count(lambda s: s > thr)
    need = jnp.float32(ktop) - c_gt
    pos = (lax.broadcasted_iota(i32, (1, n_pages, ps), 1) * ps
           + lax.broadcasted_iota(i32, (1, n_pages, ps), 2))

    def eq_le(cand):
        return (red(((sc == thr) & (pos <= cand)).astype(f32))
                + ((s_new == thr) & (n_past <= cand)).astype(f32))

    jb = lax.cond(jnp.max(c_ge) > jnp.float32(ktop),
                  lambda: _tie_bound(eq_le, need, (n_past + 1).bit_length(), shape),
                  lambda: jnp.full(shape, n_past + 1, i32))
    sel = (sc > thr) | ((sc == thr) & (pos <= jb))

    r_i = lax.broadcasted_iota(i32, (ps, ps), 0)
    c_i = lax.broadcasted_iota(i32, (ps, ps), 1)
    upper = (r_i <= c_i).astype(bf16)
    upper_pg = (lax.broadcasted_iota(i32, (n_pages, n_pages), 0)
                <= lax.broadcasted_iota(i32, (n_pages, n_pages), 1)).astype(bf16)
    jcol = lax.broadcasted_iota(i32, (ktop, 1), 0).astype(f32)
    plane = lax.broadcasted_iota(i32, (1, n_pages), 1).astype(f32)
    ones8 = jnp.ones((8, ps), bf16)

    selb = jnp.where(sel, 1.0, 0.0).astype(bf16)
    for b in range(db):
        sb = selb[b]
        incl = _dot(sb, upper)
        tot_row = _dot_nt(ones8, sb)[0:1]
        cum_row = _dot(jnp.broadcast_to(tot_row, (8, n_pages)).astype(bf16), upper_pg)[0:1]
        page_of = jnp.sum((cum_row <= jcol).astype(f32), axis=1, keepdims=True)
        onehot = (page_of == plane)
        before = jnp.sum(jnp.where(onehot, cum_row - tot_row, 0.0), axis=1, keepdims=True)
        rloc = jcol - before
        incl_rows = _dot(onehot.astype(bf16), incl.astype(bf16))
        off_of = jnp.sum((incl_rows <= rloc).astype(f32), axis=1, keepdims=True)
        idx = jnp.minimum(page_of * ps + off_of, jnp.float32(n_past))
        idx_ref[b] = idx.astype(i32)
        pidx = jnp.minimum(idx, jnp.float32(n_past - 1))
        pg = jnp.floor(pidx * (1.0 / ps))
        phys = jnp.sum(jnp.where(pg == plane, pt_ref[b], 0.0), axis=1, keepdims=True)
        rows_ref[b] = (phys * ps + (pidx - pg * ps)).astype(i32)


def _sample_select(sc3, qi3, ki3, w3, pt3, n_past, ktop):
    db = sc3.shape[0]
    kern = functools.partial(_sample_select_kernel, n_past=n_past, ktop=ktop)
    full = lambda a: pl.BlockSpec(a.shape, lambda i: (0,) * a.ndim)
    out_spec = pl.BlockSpec((db, ktop, 1), lambda i: (0, 0, 0))
    out_sd = jax.ShapeDtypeStruct((db, ktop, 1), i32)
    return pl.pallas_call(
        kern,
        grid=(1,),
        in_specs=[full(sc3), full(qi3), full(ki3), full(w3), full(pt3)],
        out_specs=[out_spec, out_spec],
        out_shape=[out_sd, out_sd],
        compiler_params=_cparams(("arbitrary",)),
    )(sc3, qi3, ki3, w3, pt3)


def _sample_attend_kernel(rows_s, idxv_ref, q_ref, kn_ref, vn_ref, az_ref, ck_ref, cv_ref,
                          o_ref, kbuf, vbuf, sem, *, n_past, ktop):
    b = pl.program_id(0)
    nb = pl.num_programs(0)

    def row_copies(bb, slot, j):
        row = rows_s[bb * ktop + j]
        ck = pltpu.make_async_copy(ck_ref.at[row], kbuf.at[slot, :, j, :], sem.at[0, slot])
        cv = pltpu.make_async_copy(cv_ref.at[row], vbuf.at[slot, :, j, :], sem.at[1, slot])
        return ck, cv

    def issue(bb, slot):
        def body(j, carry):
            ck, cv = row_copies(bb, slot, j)
            ck.start()
            cv.start()
            return carry
        lax.fori_loop(0, ktop, body, 0, unroll=DMA_UNROLL)

    def wait_all(bb, slot):
        def body(j, carry):
            ck, cv = row_copies(bb, slot, j)
            ck.wait()
            cv.wait()
            return carry
        lax.fori_loop(0, ktop, body, 0, unroll=DMA_UNROLL)

    slot = b % 2

    @pl.when(b == 0)
    def _():
        issue(b, slot)

    @pl.when(b + 1 < nb)
    def _():
        issue(b + 1, 1 - slot)

    wait_all(b, slot)

    is_new = idxv_ref[0] >= n_past
    newf = is_new.astype(f32)
    heads = [slice(h * LANES, (h + 1) * LANES) for h in range(ATT_HEADS)]
    scores = []
    for h, sl in enumerate(heads):
        q8 = jnp.broadcast_to(q_ref[0, :, sl], (8, LANES)).astype(bf16)
        scores.append(_dot_nt(q8, kbuf[slot, h].astype(bf16))[0:1])
    probs = []
    for h, sl in enumerate(heads):
        qh = q_ref[0, :, sl]
        s_new = jnp.sum(qh.astype(bf16).astype(f32) * kn_ref[0, :, sl].astype(bf16).astype(f32),
                        axis=-1, keepdims=True)
        s = jnp.where(is_new, s_new, scores[h])
        p = jnp.exp(s - jnp.max(s, axis=-1, keepdims=True))
        probs.append((p, jnp.sum(p, axis=-1, keepdims=True)))
    for h, sl in enumerate(heads):
        p, l = probs[h]
        p_old = jnp.broadcast_to(p * (1.0 - newf), (8, ktop)).astype(bf16)
        pv = _dot(p_old, vbuf[slot, h].astype(bf16))[0:1]
        pv = pv + jnp.sum(p * newf, axis=-1, keepdims=True) * vn_ref[0, :, sl]
        o_ref[0, :, sl] = ((pv / l) * _silu(az_ref[0, :, sl])).astype(o_ref.dtype)


def _sample_attend(rows_flat, idx_row, q3, kn3, vn3, az3, ck3, cv3, n_past, ktop):
    db = q3.shape[0]
    kern = functools.partial(_sample_attend_kernel, n_past=n_past, ktop=ktop)
    tok = pl.BlockSpec((1, 1, ATT_WIDTH), lambda bi, r: (bi, 0, 0))
    grid_spec = pltpu.PrefetchScalarGridSpec(
        num_scalar_prefetch=1,
        grid=(db,),
        in_specs=[pl.BlockSpec((1, 1, ktop), lambda bi, r: (bi, 0, 0)), tok, tok, tok, tok,
                  pl.BlockSpec(memory_space=pl.ANY), pl.BlockSpec(memory_space=pl.ANY)],
        out_specs=tok,
        scratch_shapes=[pltpu.VMEM((2, ATT_HEADS, ktop, ATT_DH), f32),
                        pltpu.VMEM((2, ATT_HEADS, ktop, ATT_DH), f32),
                        pltpu.SemaphoreType.DMA((2, 2))],
    )
    return pl.pallas_call(
        kern,
        grid_spec=grid_spec,
        out_shape=jax.ShapeDtypeStruct((db, 1, ATT_WIDTH), bf16),
        compiler_params=_cparams(("arbitrary",)),
    )(rows_flat, idx_row, q3, kn3, vn3, az3, ck3, cv3)


def _tail_kernel(oa_ref, ob_ref, mga_ref, mgb_ref, x_ref, p_ref, wa_ref, wb_ref, wo_ref,
                 pn_ref, wg_ref, wp_ref, y_ref):
    a = _dot(oa_ref[...].astype(bf16), wa_ref[...])
    b = _dot(ob_ref[...].astype(bf16), wb_ref[...])
    merged = _sigmoid(mga_ref[...]) * a + _sigmoid(mgb_ref[...]) * b
    x2 = x_ref[...] + _dot(merged.astype(bf16), wo_ref[...])
    ms = jnp.mean(x2 * x2, axis=-1, keepdims=True)
    hn = (x2 * lax.rsqrt(ms + EPS) * pn_ref[...]).astype(bf16)
    gate = _sigmoid(_dot(hn, wg_ref[...]))
    y_ref[...] = x2 + gate * _dot(p_ref[...].astype(bf16), wp_ref[...])


def _tail(oa, ob, z2d, x2d, p2d, wa, wb, wo, pn_row, wg, wp, tm):
    m = x2d.shape[0]
    const = lambda a: pl.BlockSpec(a.shape, lambda i: (0, 0), pipeline_mode=pl.Buffered(1))
    return pl.pallas_call(
        _tail_kernel,
        grid=(m // tm,),
        in_specs=[
            pl.BlockSpec((tm, GDN_WIDTH), lambda i: (i, 0)),
            pl.BlockSpec((tm, ATT_WIDTH), lambda i: (i, 0)),
            pl.BlockSpec((tm, D_MODEL), lambda i: (i, C_MGA // D_MODEL)),
            pl.BlockSpec((tm, D_MODEL), lambda i: (i, C_MGB // D_MODEL)),
            pl.BlockSpec((tm, D_MODEL), lambda i: (i, 0)),
            pl.BlockSpec((tm, PLE_DIM), lambda i: (i, 0)),
            const(wa), const(wb), const(wo), const(pn_row), const(wg), const(wp),
        ],
        out_specs=pl.BlockSpec((tm, D_MODEL), lambda i: (i, 0)),
        out_shape=jax.ShapeDtypeStruct((m, D_MODEL), f32),
        compiler_params=_cparams(("parallel",)),
    )(oa, ob, z2d, z2d, x2d, p2d, wa, wb, wo, pn_row, wg, wp)


def _lane_row(v, offset=0):
    row = jnp.zeros((1, LANES), f32)
    return row.at[0, offset:offset + v.shape[0]].set(v.astype(f32))


def _rope_tables(pos):
    half = ATT_DH // 2
    inv = ROPE_THETA ** (-jnp.arange(half, dtype=f32) * 2.0 / ATT_DH)
    ang = pos.astype(f32)[:, None] * inv[None, :]
    cos, sin = jnp.cos(ang), jnp.sin(ang)
    return jnp.concatenate([cos, cos], axis=1), jnp.concatenate([-sin, sin], axis=1)


def kernel(x_prompt, x_sample, p_prompt, p_sample, cache_k, cache_v, cache_idx_k, state_gdn, state_conv,
           page_table, norm_in, w_in, conv_w, a_log, dt_bias, gdn_norm, q_norm, k_norm,
           w_proj_a, w_proj_b, w_out, ple_norm, w_ple_gate, w_ple_proj):
    depth = norm_in.shape[0]
    assert depth == 1, "single-layer trunk"
    B, T, _ = x_prompt.shape
    DB, S_new, _ = x_sample.shape
    assert S_new == 1
    n_pages = page_table.shape[1]
    n_past = n_pages * PAGE_SIZE
    n_pool = cache_k.shape[1]
    li = 0

    assert w_in.shape[2] == D_IN
    w_perm = _wprep(jnp.swapaxes(w_in[li], 0, 1))
    gain_in = norm_in[li].reshape(1, D_MODEL)
    a_row = _lane_row(a_log[li], SM_GA)
    dt_row = _lane_row(dt_bias[li], SM_GA)
    gdn_norm_row = gdn_norm[li].reshape(1, LANES)
    qn_row = q_norm[li].reshape(1, LANES)
    kn_row = k_norm[li].reshape(1, LANES)
    wa = w_proj_a[li].astype(bf16)
    wb = w_proj_b[li].astype(bf16)
    wo = w_out[li].astype(bf16)
    wg = w_ple_gate[li].astype(bf16)
    wp = w_ple_proj[li].astype(bf16)
    pn_row = ple_norm[li].reshape(1, D_MODEL)
    cw = conv_w[li]

    xp2 = x_prompt.reshape(B * T, D_MODEL)
    zp = _inproj(xp2, gain_in, w_perm, tm=1024, tn=768)
    zp3 = zp.reshape(B, T, NZ)

    prev8 = jnp.zeros((B, 8, CONV_CH), f32)
    qa, ka, va, gcb, btb, bg = _gdn_prep(zp3, prev8, cw, a_row, dt_row, tm=256)
    gc_row = jnp.swapaxes(bg[..., SM_GA:SM_GA + GDN_HEADS], 1, 2).reshape(
        B, GDN_HEADS, T // GDN_CHUNK, 1, GDN_CHUNK)
    s00 = jnp.zeros((B, GDN_HEADS, GDN_DK, GDN_DV), f32)
    oa_p, s_p = _gdn_scan(qa, ka, va, zp3, gcb, btb, gc_row, s00, gdn_norm_row)
    conv_p = zp3[:, T - (CONV_W - 1):, C_GQ:C_GQ + CONV_CH]

    cos_p, sin_p = _rope_tables(jnp.arange(T))
    qb_p, kf_p, v_p, kb_p, vt_p, qi_p, kif_p, kib_p, wt_p = _dsa_prep(
        zp3, cos_p, sin_p, qn_row, kn_row, tm=256, transposed=True, kc=ATT_KC,
        q_scale=LOG2E * ATT_DH ** -0.5)
    ob_p = _dsa_attend(qi_p, wt_p, kib_p, qb_p, kb_p, vt_p, zp3)

    y_p = _tail(oa_p.reshape(B * T, GDN_WIDTH), ob_p.reshape(B * T, ATT_WIDTH), zp, xp2,
                p_prompt[li].reshape(B * T, PLE_DIM), wa, wb, wo, pn_row, wg, wp, tm=256)

    xs2 = x_sample.reshape(DB, D_MODEL)
    zs = _inproj(xs2, gain_in, w_perm, tm=DB, tn=NZ // 6)
    sconv = state_conv[li]
    qs, ks, vs, bgs = _gdn_prep_sample(zs, sconv[:, 0], sconv[:, 1], sconv[:, 2], cw, a_row, dt_row)
    oa_s, s_s = _gdn_step(qs, ks, vs, zs, bgs, state_gdn[li], gdn_norm_row)
    conv_s = jnp.concatenate([sconv[:, 1:], zs[:, None, C_GQ:C_GQ + CONV_CH]], axis=1)

    cos_s, sin_s = _rope_tables(jnp.full((DB,), n_past))
    qf_s, kf_s, v_s, qi_s, kif_s, kib_s = _dsa_prep(
        zs.reshape(1, DB, NZ), cos_s, sin_s, qn_row, kn_row, tm=DB, transposed=False, kc=ATT_KC,
        q_scale=ATT_DH ** -0.5)
    wi_s = zs[:, C_SMALL + SM_IW:C_SMALL + SM_IW + IDX_HEADS] * (IDX_HEADS ** -0.5 * IDX_DH ** -0.5)
    qi3 = qi_s.reshape(DB, IDX_HEADS, IDX_DH)
    w3 = wi_s.reshape(DB, IDX_HEADS, 1)
    pt_flat = page_table.reshape(-1).astype(i32)
    sc = _sample_scores(pt_flat, qi3, w3, cache_idx_k[li])
    ktop = min(TOPK_MAX, (n_past + S_new) // 4)
    idx, rows = _sample_select(sc.reshape(DB, n_pages, PAGE_SIZE), qi3, kib_s.reshape(DB, 1, IDX_DH), w3,
                               page_table.astype(f32).reshape(DB, 1, n_pages), n_past, ktop)
    ob_s = _sample_attend(rows.reshape(-1), idx.reshape(DB, 1, ktop),
                          qf_s.reshape(DB, 1, ATT_WIDTH), kf_s.reshape(DB, 1, ATT_WIDTH),
                          v_s.reshape(DB, 1, ATT_WIDTH), zs[:, C_AZ:C_AZ + ATT_WIDTH].reshape(DB, 1, ATT_WIDTH),
                          cache_k[li].reshape(n_pool * PAGE_SIZE, ATT_HEADS, ATT_DH),
                          cache_v[li].reshape(n_pool * PAGE_SIZE, ATT_HEADS, ATT_DH), n_past, ktop)
    y_s = _tail(oa_s.reshape(DB, GDN_WIDTH), ob_s.reshape(DB, ATT_WIDTH), zs, xs2,
                p_sample[li].reshape(DB, PLE_DIM), wa, wb, wo, pn_row, wg, wp, tm=DB)

    shp = (B, T, ATT_HEADS, ATT_DH)
    shs = (DB, S_new, ATT_HEADS, ATT_DH)
    return (y_p.reshape(B, T, D_MODEL), y_s.reshape(DB, S_new, D_MODEL),
            kf_p.reshape(shp)[None], v_p.reshape(shp)[None], kif_p[None],
            s_p[None], conv_p[None],
            kf_s.reshape(shs)[None], v_s.reshape(shs)[None], kif_s.reshape(DB, S_new, IDX_DH)[None],
            s_s[None], conv_s[None])
```

```python
import functools
import math

import jax
import jax.numpy as jnp
from jax import lax
from jax.experimental import pallas as pl
from jax.experimental.pallas import tpu as pltpu

f32 = jnp.float32
bf16 = jnp.bfloat16
i32 = jnp.int32

D_MODEL = 2048
PAGE_SIZE = 128
GDN_HEADS = 8
GDN_DK = 128
GDN_DV = 128
GDN_WIDTH = GDN_HEADS * GDN_DV
CONV_W = 4
CONV_CH = 2 * GDN_HEADS * GDN_DK + GDN_WIDTH
GDN_CHUNK = 64
ATT_HEADS = 8
ATT_DH = 128
ATT_WIDTH = ATT_HEADS * ATT_DH
IDX_HEADS = 16
IDX_DH = 128
TOPK_MAX = 256
QUERY_BLOCK = 128
ROPE_THETA = 10000.0
PLE_DIM = 256
EPS = 1e-6

LANES = 128
VMEM_LIMIT = 56 * 1024 * 1024
NEG_BIG = -1e30
INT_MIN = -(2 ** 31)
KEY_MOST_NEGATIVE = INT_MIN + 0x00800000

HK = GDN_HEADS * GDN_DK
C_GQ, C_GK, C_GV, C_GZ = 0, HK, 2 * HK, 3 * HK
C_AQ = 4 * HK
C_AK = C_AQ + ATT_WIDTH
C_AV = C_AK + ATT_WIDTH
C_AZ = C_AV + ATT_WIDTH
C_IQ = C_AZ + ATT_WIDTH
C_MGA = C_IQ + IDX_HEADS * IDX_DH
C_MGB = C_MGA + D_MODEL
C_IK = C_MGB + D_MODEL
C_SMALL = C_IK + IDX_DH
NZ = C_SMALL + LANES
SM_GA, SM_GB, SM_IW = 0, GDN_HEADS, 2 * GDN_HEADS


def _cparams(sem):
    return pltpu.CompilerParams(dimension_semantics=sem, vmem_limit_bytes=VMEM_LIMIT)


def _dot(a, b):
    return jnp.dot(a, b, preferred_element_type=f32)


def _dot_nt(a, b):
    return lax.dot_general(a, b, (((1,), (1,)), ((), ())), preferred_element_type=f32)


def _sigmoid(x):
    return 1.0 / (1.0 + jnp.exp(-x))


def _silu(x):
    return x * _sigmoid(x)


O_GA = 4 * HK
O_AQ = O_GA + 2 * GDN_HEADS
O_IQ = O_AQ + 4 * ATT_WIDTH
O_IK = O_IQ + IDX_HEADS * IDX_DH
O_IW = O_IK + IDX_DH
O_MGA = O_IW + IDX_HEADS
D_IN = O_MGA + 2 * D_MODEL


WP_TW = 2 * LANES


def _wprep_src_rows():
    rows = []
    for j in range(NZ // WP_TW):
        c = j * WP_TW
        if c < C_AQ:
            rows.append(c)
        elif c < C_MGA:
            rows.append(c - C_AQ + O_AQ)
        elif c < C_IK:
            rows.append(c - C_MGA + O_MGA)
        else:
            rows.append(O_IK)
    return jnp.asarray(rows, i32)


def _wprep_kernel(src_ref, a_ref, b_ref, o_ref):
    j = pl.program_id(0)
    a = a_ref[...]
    k = a.shape[1]
    small = jnp.concatenate([b_ref[...], a[IDX_DH:IDX_DH + IDX_HEADS],
                             jnp.zeros((LANES - SM_IW - IDX_HEADS, k), f32)], axis=0)
    last = jnp.concatenate([a[0:IDX_DH], small], axis=0)
    tile = jnp.where(j == pl.num_programs(0) - 1, last, a)
    o_ref[...] = tile.T.astype(bf16)


def _wprep(wt):
    assert O_IW == O_IK + IDX_DH and C_SMALL == C_IK + IDX_DH and NZ == C_IK + WP_TW
    assert O_GA % SM_IW == 0 and 2 * GDN_HEADS == SM_IW and C_IK % WP_TW == 0
    k = wt.shape[1]
    grid_spec = pltpu.PrefetchScalarGridSpec(
        num_scalar_prefetch=1,
        grid=(NZ // WP_TW,),
        in_specs=[pl.BlockSpec((pl.Element(WP_TW), pl.Element(k)),
                               lambda j, src: (pl.multiple_of(src[j], SM_IW), 0)),
                  pl.BlockSpec((SM_IW, k), lambda j, src: (O_GA // SM_IW, 0))],
        out_specs=pl.BlockSpec((k, WP_TW), lambda j, src: (0, j)),
    )
    return pl.pallas_call(
        _wprep_kernel,
        grid_spec=grid_spec,
        out_shape=jax.ShapeDtypeStruct((k, NZ), bf16),
        compiler_params=_cparams(("arbitrary",)),
    )(_wprep_src_rows(), wt, wt)


def _inproj_kernel(x_ref, g_ref, w_ref, o_ref, h_ref):
    @pl.when(pl.program_id(1) == 0)
    def _():
        x = x_ref[...]
        ms = jnp.mean(x * x, axis=-1, keepdims=True)
        h_ref[...] = (x * lax.rsqrt(ms + EPS) * g_ref[...]).astype(bf16)

    o_ref[...] = _dot(h_ref[...], w_ref[...])


def _inproj(x2d, gain_row, w_bf16, tm, tn):
    m, k = x2d.shape
    n = w_bf16.shape[1]
    return pl.pallas_call(
        _inproj_kernel,
        grid=(m // tm, n // tn),
        in_specs=[
            pl.BlockSpec((tm, k), lambda i, j: (i, 0)),
            pl.BlockSpec((1, k), lambda i, j: (0, 0)),
            pl.BlockSpec((k, tn), lambda i, j: (0, j)),
        ],
        out_specs=pl.BlockSpec((tm, tn), lambda i, j: (i, j)),
        out_shape=jax.ShapeDtypeStruct((m, n), f32),
        scratch_shapes=[pltpu.VMEM((tm, k), bf16)],
        compiler_params=_cparams(("parallel", "arbitrary")),
    )(x2d, gain_row, w_bf16)


def _softplus(x):
    return jnp.maximum(x, 0.0) + jnp.log1p(jnp.exp(-jnp.abs(x)))


def _gdn_heads_out(conv_fn, q_ref, k_ref, v_ref):
    for c in range(3 * GDN_HEADS):
        a = _silu(conv_fn(c))
        h = c % GDN_HEADS
        sl = slice(h * LANES, (h + 1) * LANES)
        if c < 2 * GDN_HEADS:
            nrm = lax.rsqrt(jnp.sum(a * a, axis=-1, keepdims=True) + EPS)
            if c < GDN_HEADS:
                q_ref[:, sl] = a * nrm * (GDN_DK ** -0.5)
            else:
                k_ref[:, sl] = a * nrm
        else:
            v_ref[:, sl] = a


def _gdn_prep_kernel(x_ref, halo_ref, prev_ref, cw_ref, sm_ref, a_ref, dt_ref,
                     q_ref, k_ref, v_ref, gcb_ref, btb_ref, bg_ref, xe_ref, *, tm):
    i = pl.program_id(1)
    xe_ref[pl.ds(8, tm), :] = x_ref[0]

    @pl.when(i == 0)
    def _():
        xe_ref[pl.ds(0, 8), :] = prev_ref[0]

    @pl.when(i > 0)
    def _():
        xe_ref[pl.ds(0, 8), :] = halo_ref[0]

    def conv_fn(c):
        sl = slice(c * LANES, (c + 1) * LANES)
        acc = xe_ref[pl.ds(8 - (CONV_W - 1), tm), sl] * cw_ref[0:1, sl]
        for j in range(1, CONV_W):
            acc = acc + xe_ref[pl.ds(8 - (CONV_W - 1) + j, tm), sl] * cw_ref[j:j + 1, sl]
        return acc

    _gdn_heads_out(conv_fn, q_ref.at[0], k_ref.at[0], v_ref.at[0])

    sm = sm_ref[0]
    g = -jnp.exp(a_ref[...]) * _softplus(sm + dt_ref[...])
    row = lax.broadcasted_iota(i32, (tm, LANES), 0) % GDN_CHUNK
    s = 1
    while s < GDN_CHUNK:
        g = g + jnp.where(row >= s, pltpu.roll(g, s, axis=0), 0.0)
        s *= 2
    lane = lax.broadcasted_iota(i32, (tm, LANES), 1)
    bt = _sigmoid(sm)
    bg_ref[0] = jnp.where(lane < GDN_HEADS, g, bt)
    for h in range(GDN_HEADS):
        sl = slice(h * LANES, (h + 1) * LANES)
        gcb_ref[0, :, sl] = jnp.broadcast_to(g[:, SM_GA + h:SM_GA + h + 1], (tm, LANES))
        btb_ref[0, :, sl] = jnp.broadcast_to(bt[:, SM_GB + h:SM_GB + h + 1], (tm, LANES))


def _gdn_prep(z3, prev8, conv_w, a_row, dt_row, tm):
    b, t, _ = z3.shape
    nblk = tm // 8
    kern = functools.partial(_gdn_prep_kernel, tm=tm)
    tok = pl.BlockSpec((1, tm, HK), lambda bi, i: (bi, i, 0))
    out_sd = jax.ShapeDtypeStruct((b, t, HK), f32)
    return pl.pallas_call(
        kern,
        grid=(b, t // tm),
        in_specs=[
            pl.BlockSpec((1, tm, CONV_CH), lambda bi, i: (bi, i, 0)),
            pl.BlockSpec((1, 8, CONV_CH), lambda bi, i: (bi, jnp.maximum(i * nblk - 1, 0), 0)),
            pl.BlockSpec((1, 8, CONV_CH), lambda bi, i: (bi, 0, 0)),
            pl.BlockSpec((CONV_W, CONV_CH), lambda bi, i: (0, 0)),
            pl.BlockSpec((1, tm, LANES), lambda bi, i: (bi, i, C_SMALL // LANES)),
            pl.BlockSpec((1, LANES), lambda bi, i: (0, 0)),
            pl.BlockSpec((1, LANES), lambda bi, i: (0, 0)),
        ],
        out_specs=[tok] * 5 + [pl.BlockSpec((1, tm, LANES), lambda bi, i: (bi, i, 0))],
        out_shape=[out_sd] * 5 + [jax.ShapeDtypeStruct((b, t, LANES), f32)],
        scratch_shapes=[pltpu.VMEM((tm + 8, CONV_CH), f32)],
        compiler_params=_cparams(("parallel", "arbitrary")),
    )(z3, z3, prev8, conv_w, z3, a_row, dt_row)


def _gdn_prep_sample_kernel(x_ref, s0_ref, s1_ref, s2_ref, cw_ref, sm_ref, a_ref, dt_ref,
                            q_ref, k_ref, v_ref, bg_ref):
    def conv_fn(c):
        sl = slice(c * LANES, (c + 1) * LANES)
        return (s0_ref[:, sl] * cw_ref[0:1, sl] + s1_ref[:, sl] * cw_ref[1:2, sl]
                + s2_ref[:, sl] * cw_ref[2:3, sl] + x_ref[:, sl] * cw_ref[3:4, sl])

    _gdn_heads_out(conv_fn, q_ref, k_ref, v_ref)
    sm = sm_ref[...]
    g = -jnp.exp(a_ref[...]) * _softplus(sm + dt_ref[...])
    lane = lax.broadcasted_iota(i32, sm.shape, 1)
    bg_ref[...] = jnp.where(lane < GDN_HEADS, g, _sigmoid(sm))


def _gdn_prep_sample(zs, sc0, sc1, sc2, conv_w, a_row, dt_row):
    db = zs.shape[0]
    full = lambda shape: pl.BlockSpec(shape, lambda i: (0,) * len(shape))
    out_sd = jax.ShapeDtypeStruct((db, HK), f32)
    return pl.pallas_call(
        _gdn_prep_sample_kernel,
        grid=(1,),
        in_specs=[
            pl.BlockSpec((db, CONV_CH), lambda i: (0, 0)),
            full((db, CONV_CH)), full((db, CONV_CH)), full((db, CONV_CH)),
            full((CONV_W, CONV_CH)),
            pl.BlockSpec((db, LANES), lambda i: (0, C_SMALL // LANES)),
            full((1, LANES)), full((1, LANES)),
        ],
        out_specs=[full((db, HK)), full((db, HK)), full((db, HK)), full((db, LANES))],
        out_shape=[out_sd, out_sd, out_sd, jax.ShapeDtypeStruct((db, LANES), f32)],
        compiler_params=_cparams(("arbitrary",)),
    )(zs, sc0, sc1, sc2, conv_w, zs, a_row, dt_row)


GDN_TS = 512
GDN_HB = 8


def _bmm(a, b):
    return jnp.einsum('bij,bjk->bik', a, b, preferred_element_type=f32)


def _bmm_nt(a, b):
    return jnp.einsum('bid,bjd->bij', a, b, preferred_element_type=f32)


def _gdn_scan_kernel(q_ref, k_ref, v_ref, gz_ref, gcb_ref, btb_ref, gr_ref, s0_ref, nrm_ref,
                     o_ref, so_ref, s_ref):
    step = pl.program_id(2)
    nc = GDN_TS // GDN_CHUNK
    c = GDN_CHUNK

    @pl.when(step == 0)
    def _():
        s_ref[...] = s0_ref[0]

    ri = lax.broadcasted_iota(i32, (1, c, c), 1)
    ci = lax.broadcasted_iota(i32, (1, c, c), 2)
    tri = ri >= ci
    strict = ri > ci
    eye = (ri == ci).astype(f32)

    hb = GDN_HB

    def stack(ref):
        return jnp.concatenate(
            [ref[0, :, hh * LANES:(hh + 1) * LANES].reshape(nc, c, LANES) for hh in range(hb)], axis=0)

    q, k, v = stack(q_ref), stack(k_ref), stack(v_ref)
    gcb = stack(gcb_ref)
    bt = stack(btb_ref)
    gr = gr_ref[0].reshape(hb * nc, 1, c)
    glast = gr[:, :, c - 1:c]
    decay = jnp.exp(jnp.where(tri, gcb[:, :, 0:c] - gr, -jnp.inf))
    kb = k * bt
    m = jnp.where(strict, _bmm_nt(kb, k) * decay, 0.0)
    x = eye - m
    p = m
    for _ in range(5):
        p = _bmm(p, p)
        x = x + _bmm(x, p)
    eg = jnp.exp(gcb)
    by_chunk = lambda a: a.reshape((hb, nc) + a.shape[1:])
    u = by_chunk(_bmm(x, v * bt))
    w = by_chunk(_bmm(x, kb * eg))
    qk = by_chunk(_bmm_nt(q, k) * decay)
    qg = by_chunk(q * eg)
    kg = by_chunk(k * jnp.exp(glast - gcb))
    gl = by_chunk(jnp.exp(glast))

    gain = nrm_ref[...]
    s = s_ref[...]
    for ch in range(nc):
        v_new = u[:, ch] - _bmm(w[:, ch], s)
        o = _bmm(qg[:, ch], s) + _bmm(qk[:, ch], v_new)
        s = s * gl[:, ch] + jnp.einsum('hck,hcv->hkv', kg[:, ch], v_new, preferred_element_type=f32)
        ms = jnp.mean(o * o, axis=-1, keepdims=True)
        on = o * lax.rsqrt(ms + EPS) * gain
        rows = pl.ds(ch * c, c)
        for hh in range(hb):
            sl = slice(hh * LANES, (hh + 1) * LANES)
            o_ref[0, rows, sl] = (on[hh] * _silu(gz_ref[0, rows, sl])).astype(o_ref.dtype)
    s_ref[...] = s

    @pl.when(step == pl.num_programs(2) - 1)
    def _():
        so_ref[0] = s_ref[...]


def _gdn_scan(q, k, v, z3, gcb, btb, gc_row, s0, gdn_norm_row):
    b, t, _ = q.shape
    hw = GDN_HB * LANES
    nc = GDN_TS // GDN_CHUNK
    qspec = pl.BlockSpec((1, GDN_TS, hw), lambda bi, hg, i: (bi, i, hg))
    return pl.pallas_call(
        _gdn_scan_kernel,
        grid=(b, GDN_HEADS // GDN_HB, t // GDN_TS),
        in_specs=[
            qspec, qspec, qspec,
            pl.BlockSpec((1, GDN_TS, hw), lambda bi, hg, i: (bi, i, C_GZ // hw + hg)),
            qspec, qspec,
            pl.BlockSpec((1, GDN_HB, nc, 1, GDN_CHUNK), lambda bi, hg, i: (bi, hg, i, 0, 0)),
            pl.BlockSpec((1, GDN_HB, GDN_DK, GDN_DV), lambda bi, hg, i: (bi, hg, 0, 0)),
            pl.BlockSpec((1, LANES), lambda bi, hg, i: (0, 0)),
        ],
        out_specs=[
            pl.BlockSpec((1, GDN_TS, hw), lambda bi, hg, i: (bi, i, hg)),
            pl.BlockSpec((1, GDN_HB, GDN_DK, GDN_DV), lambda bi, hg, i: (bi, hg, 0, 0)),
        ],
        out_shape=[jax.ShapeDtypeStruct((b, t, GDN_WIDTH), bf16),
                   jax.ShapeDtypeStruct((b, GDN_HEADS, GDN_DK, GDN_DV), f32)],
        scratch_shapes=[pltpu.VMEM((GDN_HB, GDN_DK, GDN_DV), f32)],
        compiler_params=_cparams(("parallel", "parallel", "arbitrary")),
    )(q, k, v, z3, gcb, btb, gc_row, s0, gdn_norm_row)


GS_BB = 8


def _gdn_step_kernel(q_ref, k_ref, v_ref, gz_ref, bg_ref, s0_ref, nrm_ref, o_ref, so_ref):
    for bb in range(GS_BB):
        row = slice(bb, bb + 1)
        heads = [slice(h * LANES, (h + 1) * LANES) for h in range(GDN_HEADS)]
        cols = [(jnp.broadcast_to(k_ref[row, sl], (GDN_DK, LANES)).T,
                 jnp.broadcast_to(q_ref[row, sl], (GDN_DK, LANES)).T) for sl in heads]
        for h, sl in enumerate(heads):
            s0 = s0_ref[bb, h]
            q = q_ref[row, sl]
            k = k_ref[row, sl]
            v = v_ref[row, sl]
            eg = jnp.exp(bg_ref[row, SM_GA + h:SM_GA + h + 1])
            bt = bg_ref[row, SM_GB + h:SM_GB + h + 1]
            kc, qc = cols[h]
            ks = jnp.sum(kc * s0, axis=0, keepdims=True)
            qs = jnp.sum(qc * s0, axis=0, keepdims=True)
            qk = jnp.sum(q * k, axis=-1, keepdims=True)
            v_new = bt * v - (bt * eg) * ks
            o = eg * qs + qk * v_new
            so_ref[bb, h] = s0 * eg + kc * v_new
            ms = jnp.mean(o * o, axis=-1, keepdims=True)
            on = o * lax.rsqrt(ms + EPS) * nrm_ref[...]
            o_ref[row, sl] = on * _silu(gz_ref[row, sl])


def _gdn_step(q, k, v, zs, bgs, s0, gdn_norm_row):
    db = q.shape[0]
    tok = pl.BlockSpec((GS_BB, HK), lambda i: (i, 0))
    st = pl.BlockSpec((GS_BB, GDN_HEADS, GDN_DK, GDN_DV), lambda i: (i, 0, 0, 0))
    return pl.pallas_call(
        _gdn_step_kernel,
        grid=(db // GS_BB,),
        in_specs=[tok, tok, tok, pl.BlockSpec((GS_BB, GDN_WIDTH), lambda i: (i, C_GZ // GDN_WIDTH)),
                  pl.BlockSpec((GS_BB, LANES), lambda i: (i, 0)), st,
                  pl.BlockSpec((1, LANES), lambda i: (0, 0))],
        out_specs=[tok, st],
        out_shape=[jax.ShapeDtypeStruct((db, GDN_WIDTH), f32),
                   jax.ShapeDtypeStruct((db, GDN_HEADS, GDN_DK, GDN_DV), f32)],
        compiler_params=_cparams(("parallel",)),
    )(q, k, v, zs, bgs, s0, gdn_norm_row)


def _rope(x, cos, sin_signed):
    return x * cos + pltpu.roll(x, LANES // 2, axis=1) * sin_signed


def _dsa_prep_kernel(aq_ref, ak_ref, av_ref, iq_ref, ik_ref, sm_ref, cos_ref, sin_ref,
                     qn_ref, kn_ref, *out_refs, transposed, q_scale):
    if transposed:
        qb_ref, kf_ref, vf_ref, kb_ref, vt_ref, qi_ref, kif_ref, kib_ref, wt_ref = out_refs
    else:
        qb_ref, kf_ref, vf_ref, qi_ref, kif_ref, kib_ref = out_refs
    cos = cos_ref[...]
    sin = sin_ref[...]
    for h in range(ATT_HEADS):
        sl = slice(h * LANES, (h + 1) * LANES)
        a = aq_ref[0, :, sl]
        a = a * lax.rsqrt(jnp.mean(a * a, axis=-1, keepdims=True) + EPS) * qn_ref[...]
        qb_ref[0, :, sl] = (_rope(a, cos, sin) * q_scale).astype(qb_ref.dtype)
        a = ak_ref[0, :, sl]
        a = a * lax.rsqrt(jnp.mean(a * a, axis=-1, keepdims=True) + EPS) * kn_ref[...]
        kr = _rope(a, cos, sin)
        kf_ref[0, :, sl] = kr
        v = av_ref[0, :, sl]
        vf_ref[0, :, sl] = v
        if transposed:
            kb_ref[0, :, sl] = kr.astype(bf16)
            vt_ref[0, 0, sl, :] = v.T.astype(bf16)
    for h in range(IDX_HEADS):
        sl = slice(h * LANES, (h + 1) * LANES)
        qi_ref[0, :, sl] = _rope(iq_ref[0, :, sl], cos, sin).astype(bf16)
    ki = _rope(ik_ref[0], cos, sin)
    kif_ref[0] = ki
    kib_ref[0] = ki.astype(bf16)
    if transposed:
        wt_ref[0] = (sm_ref[0] * (IDX_HEADS ** -0.5 * IDX_DH ** -0.5)).T


def _dsa_prep(z3, cos, sin, qn_row, kn_row, tm, transposed, kc, q_scale):
    b, t, _ = z3.shape
    zspec = lambda w, off: pl.BlockSpec((1, tm, w), lambda bi, i: (bi, i, off // w))
    tok = lambda w, dt: (pl.BlockSpec((1, tm, w), lambda bi, i: (bi, i, 0)),
                         jax.ShapeDtypeStruct((b, t, w), dt))
    if transposed:
        per = kc // tm
        outs = [tok(ATT_WIDTH, bf16), tok(ATT_WIDTH, f32), tok(ATT_WIDTH, f32), tok(ATT_WIDTH, bf16),
                (pl.BlockSpec((1, 1, ATT_WIDTH, tm), lambda bi, i: (bi, i // per, 0, i % per)),
                 jax.ShapeDtypeStruct((b, t // kc, ATT_WIDTH, kc), bf16)),
                tok(IDX_HEADS * IDX_DH, bf16), tok(IDX_DH, f32), tok(IDX_DH, bf16),
                (pl.BlockSpec((1, LANES, tm), lambda bi, i: (bi, 0, i)),
                 jax.ShapeDtypeStruct((b, LANES, t), f32))]
    else:
        outs = [tok(ATT_WIDTH, f32), tok(ATT_WIDTH, f32), tok(ATT_WIDTH, f32),
                tok(IDX_HEADS * IDX_DH, bf16), tok(IDX_DH, f32), tok(IDX_DH, bf16)]
    kern = functools.partial(_dsa_prep_kernel, transposed=transposed, q_scale=q_scale)
    return pl.pallas_call(
        kern,
        grid=(b, t // tm),
        in_specs=[
            zspec(ATT_WIDTH, C_AQ), zspec(ATT_WIDTH, C_AK), zspec(ATT_WIDTH, C_AV),
            zspec(IDX_HEADS * IDX_DH, C_IQ), zspec(IDX_DH, C_IK), zspec(LANES, C_SMALL),
            pl.BlockSpec((tm, LANES), lambda bi, i: (i, 0)),
            pl.BlockSpec((tm, LANES), lambda bi, i: (i, 0)),
            pl.BlockSpec((1, LANES), lambda bi, i: (0, 0)),
            pl.BlockSpec((1, LANES), lambda bi, i: (0, 0)),
        ],
        out_specs=[o[0] for o in outs],
        out_shape=[o[1] for o in outs],
        compiler_params=_cparams(("parallel", "parallel")),
    )(z3, z3, z3, z3, z3, z3, cos, sin, qn_row, kn_row)


def _key_to_float(key):
    bits = jnp.where(key >= 0, key, key ^ jnp.int32(0x7FFFFFFF))
    return pltpu.bitcast(bits, f32)


def _kth_largest(count_ge, k, shape):
    kf = jnp.float32(k)
    zero_ok = count_ge(jnp.zeros(shape, f32)) >= kf
    cur = jnp.where(zero_ok, jnp.int32(0), jnp.int32(INT_MIN))

    def body(it, cur):
        cand = cur + jnp.left_shift(jnp.int32(1), jnp.int32(30) - it)
        ok = count_ge(_key_to_float(cand)) >= kf
        return jnp.where(ok, cand, cur)

    cur = lax.fori_loop(0, 31, body, cur)
    return _key_to_float(cur)


def _tie_bound(count_eq_le, need, nbits, shape):
    lo = jnp.full(shape, -1, i32)

    def body(it, lo):
        cand = lo + jnp.left_shift(jnp.int32(1), jnp.int32(nbits - 1) - it)
        short = count_eq_le(cand) < need
        return jnp.where(short, cand, lo)

    lo = lax.fori_loop(0, nbits, body, lo)
    return lo + 1


ATT_KC = 512
ATT_QB = 256
ATT_KS = 128
CNT_ROWS = 64
LOG2E = 1.4426950408889634


def _dsa_attend_kernel(qi_ref, wt_ref, ki_ref, qb_ref, kb_ref, vt_ref, az_ref, o_ref,
                       sc_ref, sc16_ref, *head_refs, t_total):
    i = pl.program_id(1)
    qb = ATT_QB
    kc = ATT_KC
    nch = (i * qb + qb + kc - 1) // kc
    tq = i * qb + lax.broadcasted_iota(i32, (1, qb), 1)
    ktop = min(TOPK_MAX, t_total // 4)

    def spos(c):
        return c * kc + lax.broadcasted_iota(i32, (kc, 1), 0)

    def idx_body(c, carry):
        for sub in range(kc // ATT_KS):
            base = c * kc + sub * ATT_KS
            ksub = ki_ref[0, pl.ds(base, ATT_KS), :]
            acc = jnp.zeros((ATT_KS, qb), f32)
            for h in range(IDX_HEADS):
                d = _dot_nt(ksub, qi_ref[0, :, h * LANES:(h + 1) * LANES])
                acc = acc + jnp.maximum(d, 0.0) * wt_ref[0, SM_IW + h:SM_IW + h + 1, :]
            sp = base + lax.broadcasted_iota(i32, (ATT_KS, 1), 0)
            masked = jnp.where(sp <= tq, acc, -jnp.inf)
            sc_ref[pl.ds(base, ATT_KS), :] = masked
            sc16_ref[pl.ds(base, ATT_KS), :] = masked.astype(bf16)
        return carry

    lax.fori_loop(0, nch, idx_body, 0)

    def col_count(pred_fn):
        def body(c, acc):
            m = pred_fn(sc_ref[pl.ds(c * kc, kc), :], c)
            return acc + jnp.sum(jnp.where(m, 1.0, 0.0).reshape(kc // CNT_ROWS, CNT_ROWS, qb), axis=0)
        acc = lax.fori_loop(0, nch, body, jnp.zeros((CNT_ROWS, qb), f32))
        return jnp.sum(acc, axis=0, keepdims=True)

    def col_count16(cand):
        cand16 = cand.astype(bf16)

        def body(c, acc):
            one = jnp.where(sc16_ref[pl.ds(c * kc, kc), :] >= cand16, jnp.bfloat16(1.0), jnp.bfloat16(0.0))
            for r in range(kc // CNT_ROWS):
                acc = acc + one[r * CNT_ROWS:(r + 1) * CNT_ROWS]
            return acc
        acc = lax.fori_loop(0, nch, body, jnp.zeros((CNT_ROWS, qb), bf16))
        return jnp.sum(acc.astype(f32), axis=0, keepdims=True)

    def search():
        kf = jnp.float32(ktop)
        zero_ok = col_count16(jnp.zeros((1, qb), f32)) >= kf
        cur = jnp.where(zero_ok, jnp.int32(0), jnp.int32(INT_MIN))

        def body16(it, cur):
            cand = cur + jnp.left_shift(jnp.int32(1), jnp.int32(30) - it)
            return jnp.where(col_count16(_key_to_float(cand)) >= kf, cand, cur)

        cur = lax.fori_loop(0, 15, body16, cur)
        step = 1 << 16
        lo = jnp.maximum(jnp.maximum(cur, jnp.int32(INT_MIN + step)) - step, jnp.int32(KEY_MOST_NEGATIVE))

        def body32(it, carry):
            off, cnt = carry
            cand = off + jnp.left_shift(jnp.int32(1), jnp.int32(16) - it)
            cand_f = _key_to_float(lo + cand)
            c_new = col_count(lambda s, c: s >= cand_f)
            ok = c_new >= kf
            return jnp.where(ok, cand, off), jnp.where(ok, c_new, cnt)

        lo_f = _key_to_float(lo)
        c_lo = col_count(lambda s, c: s >= lo_f)
        off, c_ge = lax.fori_loop(0, 17, body32, (jnp.zeros((1, qb), i32), c_lo))
        thr = _key_to_float(lo + off)

        def tie_bound():
            need = kf - col_count(lambda s, c: s > thr)
            return _tie_bound(
                lambda cand: col_count(lambda s, c: (s == thr) & (spos(c) <= cand)),
                need, max(1, (t_total - 1).bit_length()), (1, qb))

        jb = lax.cond(jnp.max(c_ge) > kf, tie_bound, lambda: jnp.full((1, qb), t_total, i32))
        return thr, jb

    def take_all():
        return jnp.full((1, qb), -jnp.inf, f32), jnp.full((1, qb), t_total, i32)

    thr, jb = lax.cond((i + 1) * qb <= ktop, take_all, search)

    def bias_body(c, carry):
        s = sc_ref[pl.ds(c * kc, kc), :]
        sp = spos(c)
        sel = ((s > thr) | ((s == thr) & (sp <= jb))) & (sp <= tq)
        sc_ref[pl.ds(c * kc, kc), :] = jnp.where(sel, 0.0, NEG_BIG)
        return carry

    lax.fori_loop(0, nch, bias_body, 0)

    nh = ATT_HEADS
    m_refs, l_refs, al_refs, acc_refs, s_refs, p_refs = (head_refs[j * nh:(j + 1) * nh] for j in range(6))
    for h in range(nh):
        m_refs[h][...] = jnp.full(m_refs[h].shape, NEG_BIG, f32)
        l_refs[h][...] = jnp.zeros(l_refs[h].shape, f32)
        acc_refs[h][...] = jnp.zeros(acc_refs[h].shape, f32)

    def att_body(c, carry):
        for h in range(nh):
            sl = slice(h * LANES, (h + 1) * LANES)
            s_refs[h][...] = _dot_nt(kb_ref[0, pl.ds(c * kc, kc), sl], qb_ref[0, :, sl])
        for h in range(nh):
            s = s_refs[h][...] + sc_ref[pl.ds(c * kc, kc), :]
            m = m_refs[h][...]
            m_new = jnp.maximum(m, jnp.max(s, axis=0, keepdims=True))
            alpha = jnp.exp2(m - m_new)
            p = jnp.exp2(s - m_new)
            m_refs[h][...] = m_new
            al_refs[h][...] = alpha
            l_refs[h][...] = alpha * l_refs[h][...] + jnp.sum(p, axis=0, keepdims=True)
            p_refs[h][...] = p.astype(bf16)
        for h in range(nh):
            sl = slice(h * LANES, (h + 1) * LANES)
            acc_refs[h][...] = (al_refs[h][...] * acc_refs[h][...]
                                + _dot(vt_ref[0, c, sl, :], p_refs[h][...]))
        return carry

    lax.fori_loop(0, nch, att_body, 0)

    for h in range(ATT_HEADS):
        sl = slice(h * LANES, (h + 1) * LANES)
        out = (acc_refs[h][...] / l_refs[h][...]).T
        o_ref[0, :, sl] = (out * _silu(az_ref[0, :, sl])).astype(o_ref.dtype)


def _dsa_attend(qi, wt, ki, qb, kb, vt, z3):
    b, t, _ = qb.shape
    kern = functools.partial(_dsa_attend_kernel, t_total=t)
    return pl.pallas_call(
        kern,
        grid=(b, t // ATT_QB),
        in_specs=[
            pl.BlockSpec((1, ATT_QB, IDX_HEADS * IDX_DH), lambda bi, i: (bi, i, 0)),
            pl.BlockSpec((1, LANES, ATT_QB), lambda bi, i: (bi, 0, i)),
            pl.BlockSpec((1, t, IDX_DH), lambda bi, i: (bi, 0, 0)),
            pl.BlockSpec((1, ATT_QB, ATT_WIDTH), lambda bi, i: (bi, i, 0)),
            pl.BlockSpec((1, t, ATT_WIDTH), lambda bi, i: (bi, 0, 0)),
            pl.BlockSpec((1, t // ATT_KC, ATT_WIDTH, ATT_KC), lambda bi, i: (bi, 0, 0, 0)),
            pl.BlockSpec((1, ATT_QB, ATT_WIDTH), lambda bi, i: (bi, i, C_AZ // ATT_WIDTH)),
        ],
        out_specs=pl.BlockSpec((1, ATT_QB, ATT_WIDTH), lambda bi, i: (bi, i, 0)),
        out_shape=jax.ShapeDtypeStruct((b, t, ATT_WIDTH), bf16),
        scratch_shapes=([pltpu.VMEM((t, ATT_QB), f32), pltpu.VMEM((t, ATT_QB), bf16)]
                        + [pltpu.VMEM((1, ATT_QB), f32) for _ in range(3 * ATT_HEADS)]
                        + [pltpu.VMEM((LANES, ATT_QB), f32) for _ in range(ATT_HEADS)]
                        + [pltpu.VMEM((ATT_KC, ATT_QB), f32) for _ in range(ATT_HEADS)]
                        + [pltpu.VMEM((ATT_KC, ATT_QB), bf16) for _ in range(ATT_HEADS)]),
        compiler_params=_cparams(("parallel", "arbitrary")),
    )(qi, wt, ki, qb, kb, vt, z3)


SC_PG = 16
DMA_UNROLL = 8


def _sample_scores_kernel(pt_s, qi_ref, w_ref, cik_ref, o_ref, buf, sem, *, n_pages):
    b = pl.program_id(0)
    nb = pl.num_programs(0)

    def page_copy(bb, slot, p):
        return pltpu.make_async_copy(cik_ref.at[pt_s[bb * n_pages + p]], buf.at[slot, p], sem.at[slot])

    def issue(bb, slot):
        def body(p, carry):
            page_copy(bb, slot, p).start()
            return carry
        lax.fori_loop(0, n_pages, body, 0, unroll=DMA_UNROLL)

    def wait_all(bb, slot):
        def body(p, carry):
            page_copy(bb, slot, p).wait()
            return carry
        lax.fori_loop(0, n_pages, body, 0, unroll=DMA_UNROLL)

    slot = b % 2

    @pl.when(b == 0)
    def _():
        issue(b, slot)

    @pl.when(b + 1 < nb)
    def _():
        issue(b + 1, 1 - slot)

    wait_all(b, slot)

    qi = qi_ref[0]
    w = w_ref[0]
    gk = SC_PG * PAGE_SIZE
    for g in range(n_pages // SC_PG):
        keys = buf[slot, g * SC_PG:(g + 1) * SC_PG].reshape(gk, IDX_DH).astype(bf16)
        d = _dot_nt(qi, keys)
        o_ref[0, :, g * gk:(g + 1) * gk] = jnp.sum(jnp.maximum(d, 0.0) * w, axis=0, keepdims=True)


def _sample_scores(pt_flat, qi3, w3, cache_ik):
    db = qi3.shape[0]
    n_pages = pt_flat.shape[0] // db
    kern = functools.partial(_sample_scores_kernel, n_pages=n_pages)
    grid_spec = pltpu.PrefetchScalarGridSpec(
        num_scalar_prefetch=1,
        grid=(db,),
        in_specs=[pl.BlockSpec((1, IDX_HEADS, IDX_DH), lambda bi, pt: (bi, 0, 0)),
                  pl.BlockSpec((1, IDX_HEADS, 1), lambda bi, pt: (bi, 0, 0)),
                  pl.BlockSpec(memory_space=pl.ANY)],
        out_specs=pl.BlockSpec((1, 1, n_pages * PAGE_SIZE), lambda bi, pt: (bi, 0, 0)),
        scratch_shapes=[pltpu.VMEM((2, n_pages, PAGE_SIZE, IDX_DH), f32), pltpu.SemaphoreType.DMA((2,))],
    )
    return pl.pallas_call(
        kern,
        grid_spec=grid_spec,
        out_shape=jax.ShapeDtypeStruct((db, 1, n_pages * PAGE_SIZE), f32),
        compiler_params=_cparams(("arbitrary",)),
    )(pt_flat, qi3, w3, cache_ik)


def _sample_select_kernel(sc_ref, qi_ref, ki_ref, w_ref, pt_ref, idx_ref, rows_ref, *, n_past, ktop):
    db, n_pages, ps = sc_ref.shape
    sc = sc_ref[...]

    def red(x):
        return jnp.sum(jnp.sum(x, axis=1, keepdims=True), axis=2, keepdims=True)

    dn = jnp.sum(qi_ref[...].astype(f32) * ki_ref[...].astype(f32), axis=2, keepdims=True)
    s_new = jnp.sum(jnp.maximum(dn, 0.0) * w_ref[...], axis=1, keepdims=True)

    def count(pred):
        return red(pred(sc).astype(f32)) + pred(s_new).astype(f32)

    shape = (db, 1, 1)
    thr = _kth_largest(lambda cand: count(lambda s: s >= cand), ktop, shape)
    c_ge = count(lambda s: s >= thr)
    c_gt = count(lambda s: s > thr)
    need = jnp.float32(ktop) - c_gt
    pos = (lax.broadcasted_iota(i32, (1, n_pages, ps), 1) * ps
           + lax.broadcasted_iota(i32, (1, n_pages, ps), 2))

    def eq_le(cand):
        return (red(((sc == thr) & (pos <= cand)).astype(f32))
                + ((s_new == thr) & (n_past <= cand)).astype(f32))

    jb = lax.cond(jnp.max(c_ge) > jnp.float32(ktop),
                  lambda: _tie_bound(eq_le, need, (n_past + 1).bit_length(), shape),
                  lambda: jnp.full(shape, n_past + 1, i32))
    sel = (sc > thr) | ((sc == thr) & (pos <= jb))

    r_i = lax.broadcasted_iota(i32, (ps, ps), 0)
    c_i = lax.broadcasted_iota(i32, (ps, ps), 1)
    upper = (r_i <= c_i).astype(bf16)
    upper_pg = (lax.broadcasted_iota(i32, (n_pages, n_pages), 0)
                <= lax.broadcasted_iota(i32, (n_pages, n_pages), 1)).astype(bf16)
    jcol = lax.broadcasted_iota(i32, (ktop, 1), 0).astype(f32)
    plane = lax.broadcasted_iota(i32, (1, n_pages), 1).astype(f32)
    ones8 = jnp.ones((8, ps), bf16)

    selb = jnp.where(sel, 1.0, 0.0).astype(bf16)
    for b in range(db):
        sb = selb[b]
        incl = _dot(sb, upper)
        tot_row = _dot_nt(ones8, sb)[0:1]
        cum_row = _dot(jnp.broadcast_to(tot_row, (8, n_pages)).astype(bf16), upper_pg)[0:1]
        page_of = jnp.sum((cum_row <= jcol).astype(f32), axis=1, keepdims=True)
        onehot = (page_of == plane)
        before = jnp.sum(jnp.where(onehot, cum_row - tot_row, 0.0), axis=1, keepdims=True)
        rloc = jcol - before
        incl_rows = _dot(onehot.astype(bf16), incl.astype(bf16))
        off_of = jnp.sum((incl_rows <= rloc).astype(f32), axis=1, keepdims=True)
        idx = jnp.minimum(page_of * ps + off_of, jnp.float32(n_past))
        idx_ref[b] = idx.astype(i32)
        pidx = jnp.minimum(idx, jnp.float32(n_past - 1))
        pg = jnp.floor(pidx * (1.0 / ps))
        phys = jnp.sum(jnp.where(pg == plane, pt_ref[b], 0.0), axis=1, keepdims=True)
        rows_ref[b] = (phys * ps + (pidx - pg * ps)).astype(i32)


def _sample_select(sc3, qi3, ki3, w3, pt3, n_past, ktop):
    db = sc3.shape[0]
    kern = functools.partial(_sample_select_kernel, n_past=n_past, ktop=ktop)
    full = lambda a: pl.BlockSpec(a.shape, lambda i: (0,) * a.ndim)
    out_spec = pl.BlockSpec((db, ktop, 1), lambda i: (0, 0, 0))
    out_sd = jax.ShapeDtypeStruct((db, ktop, 1), i32)
    return pl.pallas_call(
        kern,
        grid=(1,),
        in_specs=[full(sc3), full(qi3), full(ki3), full(w3), full(pt3)],
        out_specs=[out_spec, out_spec],
        out_shape=[out_sd, out_sd],
        compiler_params=_cparams(("arbitrary",)),
    )(sc3, qi3, ki3, w3, pt3)


def _sample_attend_kernel(rows_s, idxv_ref, q_ref, kn_ref, vn_ref, az_ref, ck_ref, cv_ref,
                          o_ref, kbuf, vbuf, sem, *, n_past, ktop):
    b = pl.program_id(0)
    nb = pl.num_programs(0)

    def row_copies(bb, slot, j):
        row = rows_s[bb * ktop + j]
        ck = pltpu.make_async_copy(ck_ref.at[row], kbuf.at[slot, :, j, :], sem.at[0, slot])
        cv = pltpu.make_async_copy(cv_ref.at[row], vbuf.at[slot, :, j, :], sem.at[1, slot])
        return ck, cv

    def issue(bb, slot):
        def body(j, carry):
            ck, cv = row_copies(bb, slot, j)
            ck.start()
            cv.start()
            return carry
        lax.fori_loop(0, ktop, body, 0, unroll=DMA_UNROLL)

    def wait_all(bb, slot):
        def body(j, carry):
            ck, cv = row_copies(bb, slot, j)
            ck.wait()
            cv.wait()
            return carry
        lax.fori_loop(0, ktop, body, 0, unroll=DMA_UNROLL)

    slot = b % 2

    @pl.when(b == 0)
    def _():
        issue(b, slot)

    @pl.when(b + 1 < nb)
    def _():
        issue(b + 1, 1 - slot)

    wait_all(b, slot)

    is_new = idxv_ref[0] >= n_past
    newf = is_new.astype(f32)
    heads = [slice(h * LANES, (h + 1) * LANES) for h in range(ATT_HEADS)]
    scores = []
    for h, sl in enumerate(heads):
        q8 = jnp.broadcast_to(q_ref[0, :, sl], (8, LANES)).astype(bf16)
        scores.append(_dot_nt(q8, kbuf[slot, h].astype(bf16))[0:1])
    probs = []
    for h, sl in enumerate(heads):
        qh = q_ref[0, :, sl]
        s_new = jnp.sum(qh.astype(bf16).astype(f32) * kn_ref[0, :, sl].astype(bf16).astype(f32),
                        axis=-1, keepdims=True)
        s = jnp.where(is_new, s_new, scores[h])
        p = jnp.exp(s - jnp.max(s, axis=-1, keepdims=True))
        probs.append((p, jnp.sum(p, axis=-1, keepdims=True)))
    for h, sl in enumerate(heads):
        p, l = probs[h]
        p_old = jnp.broadcast_to(p * (1.0 - newf), (8, ktop)).astype(bf16)
        pv = _dot(p_old, vbuf[slot, h].astype(bf16))[0:1]
        pv = pv + jnp.sum(p * newf, axis=-1, keepdims=True) * vn_ref[0, :, sl]
        o_ref[0, :, sl] = ((pv / l) * _silu(az_ref[0, :, sl])).astype(o_ref.dtype)


def _sample_attend(rows_flat, idx_row, q3, kn3, vn3, az3, ck3, cv3, n_past, ktop):
    db = q3.shape[0]
    kern = functools.partial(_sample_attend_kernel, n_past=n_past, ktop=ktop)
    tok = pl.BlockSpec((1, 1, ATT_WIDTH), lambda bi, r: (bi, 0, 0))
    grid_spec = pltpu.PrefetchScalarGridSpec(
        num_scalar_prefetch=1,
        grid=(db,),
        in_specs=[pl.BlockSpec((1, 1, ktop), lambda bi, r: (bi, 0, 0)), tok, tok, tok, tok,
                  pl.BlockSpec(memory_space=pl.ANY), pl.BlockSpec(memory_space=pl.ANY)],
        out_specs=tok,
        scratch_shapes=[pltpu.VMEM((2, ATT_HEADS, ktop, ATT_DH), f32),
                        pltpu.VMEM((2, ATT_HEADS, ktop, ATT_DH), f32),
                        pltpu.SemaphoreType.DMA((2, 2))],
    )
    return pl.pallas_call(
        kern,
        grid_spec=grid_spec,
        out_shape=jax.ShapeDtypeStruct((db, 1, ATT_WIDTH), bf16),
        compiler_params=_cparams(("arbitrary",)),
    )(rows_flat, idx_row, q3, kn3, vn3, az3, ck3, cv3)


def _tail_kernel(oa_ref, ob_ref, mga_ref, mgb_ref, x_ref, p_ref, wa_ref, wb_ref, wo_ref,
                 pn_ref, wg_ref, wp_ref, y_ref):
    a = _dot(oa_ref[...].astype(bf16), wa_ref[...])
    b = _dot(ob_ref[...].astype(bf16), wb_ref[...])
    merged = _sigmoid(mga_ref[...]) * a + _sigmoid(mgb_ref[...]) * b
    x2 = x_ref[...] + _dot(merged.astype(bf16), wo_ref[...])
    ms = jnp.mean(x2 * x2, axis=-1, keepdims=True)
    hn = (x2 * lax.rsqrt(ms + EPS) * pn_ref[...]).astype(bf16)
    gate = _sigmoid(_dot(hn, wg_ref[...]))
    y_ref[...] = x2 + gate * _dot(p_ref[...].astype(bf16), wp_ref[...])


def _tail(oa, ob, z2d, x2d, p2d, wa, wb, wo, pn_row, wg, wp, tm):
    m = x2d.shape[0]
    const = lambda a: pl.BlockSpec(a.shape, lambda i: (0, 0), pipeline_mode=pl.Buffered(1))
    return pl.pallas_call(
        _tail_kernel,
        grid=(m // tm,),
        in_specs=[
            pl.BlockSpec((tm, GDN_WIDTH), lambda i: (i, 0)),
            pl.BlockSpec((tm, ATT_WIDTH), lambda i: (i, 0)),
            pl.BlockSpec((tm, D_MODEL), lambda i: (i, C_MGA // D_MODEL)),
            pl.BlockSpec((tm, D_MODEL), lambda i: (i, C_MGB // D_MODEL)),
            pl.BlockSpec((tm, D_MODEL), lambda i: (i, 0)),
            pl.BlockSpec((tm, PLE_DIM), lambda i: (i, 0)),
            const(wa), const(wb), const(wo), const(pn_row), const(wg), const(wp),
        ],
        out_specs=pl.BlockSpec((tm, D_MODEL), lambda i: (i, 0)),
        out_shape=jax.ShapeDtypeStruct((m, D_MODEL), f32),
        compiler_params=_cparams(("parallel",)),
    )(oa, ob, z2d, z2d, x2d, p2d, wa, wb, wo, pn_row, wg, wp)


def _lane_row(v, offset=0):
    row = jnp.zeros((1, LANES), f32)
    return row.at[0, offset:offset + v.shape[0]].set(v.astype(f32))


def _rope_tables(pos):
    half = ATT_DH // 2
    inv = ROPE_THETA ** (-jnp.arange(half, dtype=f32) * 2.0 / ATT_DH)
    ang = pos.astype(f32)[:, None] * inv[None, :]
    cos, sin = jnp.cos(ang), jnp.sin(ang)
    return jnp.concatenate([cos, cos], axis=1), jnp.concatenate([-sin, sin], axis=1)


def kernel(x_prompt, x_sample, p_prompt, p_sample, cache_k, cache_v, cache_idx_k, state_gdn, state_conv,
           page_table, norm_in, w_in, conv_w, a_log, dt_bias, gdn_norm, q_norm, k_norm,
           w_proj_a, w_proj_b, w_out, ple_norm, w_ple_gate, w_ple_proj):
    depth = norm_in.shape[0]
    assert depth == 1, "single-layer trunk"
    B, T, _ = x_prompt.shape
    DB, S_new, _ = x_sample.shape
    assert S_new == 1
    n_pages = page_table.shape[1]
    n_past = n_pages * PAGE_SIZE
    n_pool = cache_k.shape[1]
    li = 0

    assert w_in.shape[2] == D_IN
    w_perm = _wprep(jnp.swapaxes(w_in[li], 0, 1))
    gain_in = norm_in[li].reshape(1, D_MODEL)
    a_row = _lane_row(a_log[li], SM_GA)
    dt_row = _lane_row(dt_bias[li], SM_GA)
    gdn_norm_row = gdn_norm[li].reshape(1, LANES)
    qn_row = q_norm[li].reshape(1, LANES)
    kn_row = k_norm[li].reshape(1, LANES)
    wa = w_proj_a[li].astype(bf16)
    wb = w_proj_b[li].astype(bf16)
    wo = w_out[li].astype(bf16)
    wg = w_ple_gate[li].astype(bf16)
    wp = w_ple_proj[li].astype(bf16)
    pn_row = ple_norm[li].reshape(1, D_MODEL)
    cw = conv_w[li]

    xp2 = x_prompt.reshape(B * T, D_MODEL)
    zp = _inproj(xp2, gain_in, w_perm, tm=1024, tn=768)
    zp3 = zp.reshape(B, T, NZ)

    prev8 = jnp.zeros((B, 8, CONV_CH), f32)
    qa, ka, va, gcb, btb, bg = _gdn_prep(zp3, prev8, cw, a_row, dt_row, tm=256)
    gc_row = jnp.swapaxes(bg[..., SM_GA:SM_GA + GDN_HEADS], 1, 2).reshape(
        B, GDN_HEADS, T // GDN_CHUNK, 1, GDN_CHUNK)
    s00 = jnp.zeros((B, GDN_HEADS, GDN_DK, GDN_DV), f32)
    oa_p, s_p = _gdn_scan(qa, ka, va, zp3, gcb, btb, gc_row, s00, gdn_norm_row)
    conv_p = zp3[:, T - (CONV_W - 1):, C_GQ:C_GQ + CONV_CH]

    cos_p, sin_p = _rope_tables(jnp.arange(T))
    qb_p, kf_p, v_p, kb_p, vt_p, qi_p, kif_p, kib_p, wt_p = _dsa_prep(
        zp3, cos_p, sin_p, qn_row, kn_row, tm=256, transposed=True, kc=ATT_KC,
        q_scale=LOG2E * ATT_DH ** -0.5)
    ob_p = _dsa_attend(qi_p, wt_p, kib_p, qb_p, kb_p, vt_p, zp3)

    y_p = _tail(oa_p.reshape(B * T, GDN_WIDTH), ob_p.reshape(B * T, ATT_WIDTH), zp, xp2,
                p_prompt[li].reshape(B * T, PLE_DIM), wa, wb, wo, pn_row, wg, wp, tm=256)

    xs2 = x_sample.reshape(DB, D_MODEL)
    zs = _inproj(xs2, gain_in, w_perm, tm=DB, tn=NZ // 6)
    sconv = state_conv[li]
    qs, ks, vs, bgs = _gdn_prep_sample(zs, sconv[:, 0], sconv[:, 1], sconv[:, 2], cw, a_row, dt_row)
    oa_s, s_s = _gdn_step(qs, ks, vs, zs, bgs, state_gdn[li], gdn_norm_row)
    conv_s = jnp.concatenate([sconv[:, 1:], zs[:, None, C_GQ:C_GQ + CONV_CH]], axis=1)

    cos_s, sin_s = _rope_tables(jnp.full((DB,), n_past))
    qf_s, kf_s, v_s, qi_s, kif_s, kib_s = _dsa_prep(
        zs.reshape(1, DB, NZ), cos_s, sin_s, qn_row, kn_row, tm=DB, transposed=False, kc=ATT_KC,
        q_scale=ATT_DH ** -0.5)
    wi_s = zs[:, C_SMALL + SM_IW:C_SMALL + SM_IW + IDX_HEADS] * (IDX_HEADS ** -0.5 * IDX_DH ** -0.5)
    qi3 = qi_s.reshape(DB, IDX_HEADS, IDX_DH)
    w3 = wi_s.reshape(DB, IDX_HEADS, 1)
    pt_flat = page_table.reshape(-1).astype(i32)
    sc = _sample_scores(pt_flat, qi3, w3, cache_idx_k[li])
    ktop = min(TOPK_MAX, (n_past + S_new) // 4)
    idx, rows = _sample_select(sc.reshape(DB, n_pages, PAGE_SIZE), qi3, kib_s.reshape(DB, 1, IDX_DH), w3,
                               page_table.astype(f32).reshape(DB, 1, n_pages), n_past, ktop)
    ob_s = _sample_attend(rows.reshape(-1), idx.reshape(DB, 1, ktop),
                          qf_s.reshape(DB, 1, ATT_WIDTH), kf_s.reshape(DB, 1, ATT_WIDTH),
                          v_s.reshape(DB, 1, ATT_WIDTH), zs[:, C_AZ:C_AZ + ATT_WIDTH].reshape(DB, 1, ATT_WIDTH),
                          cache_k[li].reshape(n_pool * PAGE_SIZE, ATT_HEADS, ATT_DH),
                          cache_v[li].reshape(n_pool * PAGE_SIZE, ATT_HEADS, ATT_DH), n_past, ktop)
    y_s = _tail(oa_s.reshape(DB, GDN_WIDTH), ob_s.reshape(DB, ATT_WIDTH), zs, xs2,
                p_sample[li].reshape(DB, PLE_DIM), wa, wb, wo, pn_row, wg, wp, tm=DB)

    shp = (B, T, ATT_HEADS, ATT_DH)
    shs = (DB, S_new, ATT_HEADS, ATT_DH)
    return (y_p.reshape(B, T, D_MODEL), y_s.reshape(DB, S_new, D_MODEL),
            kf_p.reshape(shp)[None], v_p.reshape(shp)[None], kif_p[None],
            s_p[None], conv_p[None],
            kf_s.reshape(shs)[None], v_s.reshape(shs)[None], kif_s.reshape(DB, S_new, IDX_DH)[None],
            s_s[None], conv_s[None])
```

```python
import functools
import math

import jax
import jax.numpy as jnp
from jax import lax
from jax.experimental import pallas as pl
from jax.experimental.pallas import tpu as pltpu

f32 = jnp.float32
bf16 = jnp.bfloat16
i32 = jnp.int32

D_MODEL = 2048
PAGE_SIZE = 128
GDN_HEADS = 8
GDN_DK = 128
GDN_DV = 128
GDN_WIDTH = GDN_HEADS * GDN_DV
CONV_W = 4
CONV_CH = 2 * GDN_HEADS * GDN_DK + GDN_WIDTH
GDN_CHUNK = 64
ATT_HEADS = 8
ATT_DH = 128
ATT_WIDTH = ATT_HEADS * ATT_DH
IDX_HEADS = 16
IDX_DH = 128
TOPK_MAX = 256
QUERY_BLOCK = 128
ROPE_THETA = 10000.0
PLE_DIM = 256
EPS = 1e-6

LANES = 128
VMEM_LIMIT = 56 * 1024 * 1024
NEG_BIG = -1e30
INT_MIN = -(2 ** 31)
KEY_MOST_NEGATIVE = INT_MIN + 0x00800000

HK = GDN_HEADS * GDN_DK
C_GQ, C_GK, C_GV, C_GZ = 0, HK, 2 * HK, 3 * HK
C_AQ = 4 * HK
C_AK = C_AQ + ATT_WIDTH
C_AV = C_AK + ATT_WIDTH
C_AZ = C_AV + ATT_WIDTH
C_IQ = C_AZ + ATT_WIDTH
C_MGA = C_IQ + IDX_HEADS * IDX_DH
C_MGB = C_MGA + D_MODEL
C_IK = C_MGB + D_MODEL
C_SMALL = C_IK + IDX_DH
NZ = C_SMALL + LANES
SM_GA, SM_GB, SM_IW = 0, GDN_HEADS, 2 * GDN_HEADS


def _cparams(sem):
    return pltpu.CompilerParams(dimension_semantics=sem, vmem_limit_bytes=VMEM_LIMIT)


def _dot(a, b):
    return jnp.dot(a, b, preferred_element_type=f32)


def _dot_nt(a, b):
    return lax.dot_general(a, b, (((1,), (1,)), ((), ())), preferred_element_type=f32)


def _sigmoid(x):
    return 1.0 / (1.0 + jnp.exp(-x))


def _silu(x):
    return x * _sigmoid(x)


O_GA = 4 * HK
O_AQ = O_GA + 2 * GDN_HEADS
O_IQ = O_AQ + 4 * ATT_WIDTH
O_IK = O_IQ + IDX_HEADS * IDX_DH
O_IW = O_IK + IDX_DH
O_MGA = O_IW + IDX_HEADS
D_IN = O_MGA + 2 * D_MODEL


WP_TW = 2 * LANES


def _wprep_src_rows():
    rows = []
    for j in range(NZ // WP_TW):
        c = j * WP_TW
        if c < C_AQ:
            rows.append(c)
        elif c < C_MGA:
            rows.append(c - C_AQ + O_AQ)
        elif c < C_IK:
            rows.append(c - C_MGA + O_MGA)
        else:
            rows.append(O_IK)
    return jnp.asarray(rows, i32)


def _wprep_kernel(src_ref, a_ref, b_ref, o_ref):
    j = pl.program_id(0)
    a = a_ref[...]
    k = a.shape[1]
    small = jnp.concatenate([b_ref[...], a[IDX_DH:IDX_DH + IDX_HEADS],
                             jnp.zeros((LANES - SM_IW - IDX_HEADS, k), f32)], axis=0)
    last = jnp.concatenate([a[0:IDX_DH], small], axis=0)
    tile = jnp.where(j == pl.num_programs(0) - 1, last, a)
    o_ref[...] = tile.T.astype(bf16)


def _wprep(wt):
    assert O_IW == O_IK + IDX_DH and C_SMALL == C_IK + IDX_DH and NZ == C_IK + WP_TW
    assert O_GA % SM_IW == 0 and 2 * GDN_HEADS == SM_IW and C_IK % WP_TW == 0
    k = wt.shape[1]
    grid_spec = pltpu.PrefetchScalarGridSpec(
        num_scalar_prefetch=1,
        grid=(NZ // WP_TW,),
        in_specs=[pl.BlockSpec((pl.Element(WP_TW), pl.Element(k)),
                               lambda j, src: (pl.multiple_of(src[j], SM_IW), 0)),
                  pl.BlockSpec((SM_IW, k), lambda j, src: (O_GA // SM_IW, 0))],
        out_specs=pl.BlockSpec((k, WP_TW), lambda j, src: (0, j)),
    )
    return pl.pallas_call(
        _wprep_kernel,
        grid_spec=grid_spec,
        out_shape=jax.ShapeDtypeStruct((k, NZ), bf16),
        compiler_params=_cparams(("arbitrary",)),
    )(_wprep_src_rows(), wt, wt)


def _inproj_kernel(x_ref, g_ref, w_ref, o_ref, h_ref):
    @pl.when(pl.program_id(1) == 0)
    def _():
        x = x_ref[...]
        ms = jnp.mean(x * x, axis=-1, keepdims=True)
        h_ref[...] = (x * lax.rsqrt(ms + EPS) * g_ref[...]).astype(bf16)

    o_ref[...] = _dot(h_ref[...], w_ref[...])


def _inproj(x2d, gain_row, w_bf16, tm, tn):
    m, k = x2d.shape
    n = w_bf16.shape[1]
    return pl.pallas_call(
        _inproj_kernel,
        grid=(m // tm, n // tn),
        in_specs=[
            pl.BlockSpec((tm, k), lambda i, j: (i, 0), pipeline_mode=pl.Buffered(1)),
            pl.BlockSpec((1, k), lambda i, j: (0, 0)),
            pl.BlockSpec((k, tn), lambda i, j: (0, j)),
        ],
        out_specs=pl.BlockSpec((tm, tn), lambda i, j: (i, j)),
        out_shape=jax.ShapeDtypeStruct((m, n), f32),
        scratch_shapes=[pltpu.VMEM((tm, k), bf16)],
        compiler_params=_cparams(("parallel", "arbitrary")),
    )(x2d, gain_row, w_bf16)


def _softplus(x):
    return jnp.maximum(x, 0.0) + jnp.log1p(jnp.exp(-jnp.abs(x)))


def _gdn_heads_out(conv_fn, q_ref, k_ref, v_ref):
    for c in range(3 * GDN_HEADS):
        a = _silu(conv_fn(c))
        h = c % GDN_HEADS
        sl = slice(h * LANES, (h + 1) * LANES)
        if c < 2 * GDN_HEADS:
            nrm = lax.rsqrt(jnp.sum(a * a, axis=-1, keepdims=True) + EPS)
            if c < GDN_HEADS:
                q_ref[:, sl] = a * nrm * (GDN_DK ** -0.5)
            else:
                k_ref[:, sl] = a * nrm
        else:
            v_ref[:, sl] = a


def _gdn_prep_kernel(x_ref, halo_ref, prev_ref, cw_ref, sm_ref, a_ref, dt_ref,
                     q_ref, k_ref, v_ref, gcb_ref, btb_ref, bg_ref, xe_ref, *, tm):
    i = pl.program_id(1)
    xe_ref[pl.ds(8, tm), :] = x_ref[0]

    @pl.when(i == 0)
    def _():
        xe_ref[pl.ds(0, 8), :] = prev_ref[0]

    @pl.when(i > 0)
    def _():
        xe_ref[pl.ds(0, 8), :] = halo_ref[0]

    def conv_fn(c):
        sl = slice(c * LANES, (c + 1) * LANES)
        acc = xe_ref[pl.ds(8 - (CONV_W - 1), tm), sl] * cw_ref[0:1, sl]
        for j in range(1, CONV_W):
            acc = acc + xe_ref[pl.ds(8 - (CONV_W - 1) + j, tm), sl] * cw_ref[j:j + 1, sl]
        return acc

    _gdn_heads_out(conv_fn, q_ref.at[0], k_ref.at[0], v_ref.at[0])

    sm = sm_ref[0]
    g = -jnp.exp(a_ref[...]) * _softplus(sm + dt_ref[...])
    row = lax.broadcasted_iota(i32, (tm, LANES), 0) % GDN_CHUNK
    s = 1
    while s < GDN_CHUNK:
        g = g + jnp.where(row >= s, pltpu.roll(g, s, axis=0), 0.0)
        s *= 2
    lane = lax.broadcasted_iota(i32, (tm, LANES), 1)
    bt = _sigmoid(sm)
    bg_ref[0] = jnp.where(lane < GDN_HEADS, g, bt)
    for h in range(GDN_HEADS):
        sl = slice(h * LANES, (h + 1) * LANES)
        gcb_ref[0, :, sl] = jnp.broadcast_to(g[:, SM_GA + h:SM_GA + h + 1], (tm, LANES))
        btb_ref[0, :, sl] = jnp.broadcast_to(bt[:, SM_GB + h:SM_GB + h + 1], (tm, LANES))


def _gdn_prep(z3, prev8, conv_w, a_row, dt_row, tm):
    b, t, _ = z3.shape
    nblk = tm // 8
    kern = functools.partial(_gdn_prep_kernel, tm=tm)
    tok = pl.BlockSpec((1, tm, HK), lambda bi, i: (bi, i, 0))
    out_sd = jax.ShapeDtypeStruct((b, t, HK), f32)
    return pl.pallas_call(
        kern,
        grid=(b, t // tm),
        in_specs=[
            pl.BlockSpec((1, tm, CONV_CH), lambda bi, i: (bi, i, 0)),
            pl.BlockSpec((1, 8, CONV_CH), lambda bi, i: (bi, jnp.maximum(i * nblk - 1, 0), 0)),
            pl.BlockSpec((1, 8, CONV_CH), lambda bi, i: (bi, 0, 0)),
            pl.BlockSpec((CONV_W, CONV_CH), lambda bi, i: (0, 0)),
            pl.BlockSpec((1, tm, LANES), lambda bi, i: (bi, i, C_SMALL // LANES)),
            pl.BlockSpec((1, LANES), lambda bi, i: (0, 0)),
            pl.BlockSpec((1, LANES), lambda bi, i: (0, 0)),
        ],
        out_specs=[tok] * 5 + [pl.BlockSpec((1, tm, LANES), lambda bi, i: (bi, i, 0))],
        out_shape=[out_sd] * 5 + [jax.ShapeDtypeStruct((b, t, LANES), f32)],
        scratch_shapes=[pltpu.VMEM((tm + 8, CONV_CH), f32)],
        compiler_params=_cparams(("parallel", "arbitrary")),
    )(z3, z3, prev8, conv_w, z3, a_row, dt_row)


def _gdn_prep_sample_kernel(x_ref, s0_ref, s1_ref, s2_ref, cw_ref, sm_ref, a_ref, dt_ref,
                            q_ref, k_ref, v_ref, bg_ref):
    def conv_fn(c):
        sl = slice(c * LANES, (c + 1) * LANES)
        return (s0_ref[:, sl] * cw_ref[0:1, sl] + s1_ref[:, sl] * cw_ref[1:2, sl]
                + s2_ref[:, sl] * cw_ref[2:3, sl] + x_ref[:, sl] * cw_ref[3:4, sl])

    _gdn_heads_out(conv_fn, q_ref, k_ref, v_ref)
    sm = sm_ref[...]
    g = -jnp.exp(a_ref[...]) * _softplus(sm + dt_ref[...])
    lane = lax.broadcasted_iota(i32, sm.shape, 1)
    bg_ref[...] = jnp.where(lane < GDN_HEADS, g, _sigmoid(sm))


def _gdn_prep_sample(zs, sc0, sc1, sc2, conv_w, a_row, dt_row):
    db = zs.shape[0]
    full = lambda shape: pl.BlockSpec(shape, lambda i: (0,) * len(shape))
    out_sd = jax.ShapeDtypeStruct((db, HK), f32)
    return pl.pallas_call(
        _gdn_prep_sample_kernel,
        grid=(1,),
        in_specs=[
            pl.BlockSpec((db, CONV_CH), lambda i: (0, 0)),
            full((db, CONV_CH)), full((db, CONV_CH)), full((db, CONV_CH)),
            full((CONV_W, CONV_CH)),
            pl.BlockSpec((db, LANES), lambda i: (0, C_SMALL // LANES)),
            full((1, LANES)), full((1, LANES)),
        ],
        out_specs=[full((db, HK)), full((db, HK)), full((db, HK)), full((db, LANES))],
        out_shape=[out_sd, out_sd, out_sd, jax.ShapeDtypeStruct((db, LANES), f32)],
        compiler_params=_cparams(("arbitrary",)),
    )(zs, sc0, sc1, sc2, conv_w, zs, a_row, dt_row)


GDN_TS = 512
GDN_HB = 8


def _bmm(a, b):
    return jnp.einsum('bij,bjk->bik', a, b, preferred_element_type=f32)


def _bmm_nt(a, b):
    return jnp.einsum('bid,bjd->bij', a, b, preferred_element_type=f32)


def _gdn_scan_kernel(q_ref, k_ref, v_ref, gz_ref, gcb_ref, btb_ref, gr_ref, s0_ref, nrm_ref,
                     o_ref, so_ref, s_ref):
    step = pl.program_id(2)
    nc = GDN_TS // GDN_CHUNK
    c = GDN_CHUNK

    @pl.when(step == 0)
    def _():
        s_ref[...] = s0_ref[0]

    ri = lax.broadcasted_iota(i32, (1, c, c), 1)
    ci = lax.broadcasted_iota(i32, (1, c, c), 2)
    tri = ri >= ci
    strict = ri > ci
    eye = (ri == ci).astype(f32)

    hb = GDN_HB

    def stack(ref):
        return jnp.concatenate(
            [ref[0, :, hh * LANES:(hh + 1) * LANES].reshape(nc, c, LANES) for hh in range(hb)], axis=0)

    q, k, v = stack(q_ref), stack(k_ref), stack(v_ref)
    gcb = stack(gcb_ref)
    bt = stack(btb_ref)
    gr = gr_ref[0].reshape(hb * nc, 1, c)
    glast = gr[:, :, c - 1:c]
    decay = jnp.exp(jnp.where(tri, gcb[:, :, 0:c] - gr, -jnp.inf))
    kb = k * bt
    m = jnp.where(strict, _bmm_nt(kb, k) * decay, 0.0)
    x = eye - m
    p = m
    for _ in range(5):
        p = _bmm(p, p)
        x = x + _bmm(x, p)
    eg = jnp.exp(gcb)
    by_chunk = lambda a: a.reshape((hb, nc) + a.shape[1:])
    u = by_chunk(_bmm(x, v * bt))
    w = by_chunk(_bmm(x, kb * eg))
    qk = by_chunk(_bmm_nt(q, k) * decay)
    qg = by_chunk(q * eg)
    kg = by_chunk(k * jnp.exp(glast - gcb))
    gl = by_chunk(jnp.exp(glast))

    gain = nrm_ref[...]
    s = s_ref[...]
    for ch in range(nc):
        v_new = u[:, ch] - _bmm(w[:, ch], s)
        o = _bmm(qg[:, ch], s) + _bmm(qk[:, ch], v_new)
        s = s * gl[:, ch] + jnp.einsum('hck,hcv->hkv', kg[:, ch], v_new, preferred_element_type=f32)
        ms = jnp.mean(o * o, axis=-1, keepdims=True)
        on = o * lax.rsqrt(ms + EPS) * gain
        rows = pl.ds(ch * c, c)
        for hh in range(hb):
            sl = slice(hh * LANES, (hh + 1) * LANES)
            o_ref[0, rows, sl] = (on[hh] * _silu(gz_ref[0, rows, sl])).astype(o_ref.dtype)
    s_ref[...] = s

    @pl.when(step == pl.num_programs(2) - 1)
    def _():
        so_ref[0] = s_ref[...]


def _gdn_scan(q, k, v, z3, gcb, btb, gc_row, s0, gdn_norm_row):
    b, t, _ = q.shape
    hw = GDN_HB * LANES
    nc = GDN_TS // GDN_CHUNK
    qspec = pl.BlockSpec((1, GDN_TS, hw), lambda bi, hg, i: (bi, i, hg))
    return pl.pallas_call(
        _gdn_scan_kernel,
        grid=(b, GDN_HEADS // GDN_HB, t // GDN_TS),
        in_specs=[
            qspec, qspec, qspec,
            pl.BlockSpec((1, GDN_TS, hw), lambda bi, hg, i: (bi, i, C_GZ // hw + hg)),
            qspec, qspec,
            pl.BlockSpec((1, GDN_HB, nc, 1, GDN_CHUNK), lambda bi, hg, i: (bi, hg, i, 0, 0)),
            pl.BlockSpec((1, GDN_HB, GDN_DK, GDN_DV), lambda bi, hg, i: (bi, hg, 0, 0)),
            pl.BlockSpec((1, LANES), lambda bi, hg, i: (0, 0)),
        ],
        out_specs=[
            pl.BlockSpec((1, GDN_TS, hw), lambda bi, hg, i: (bi, i, hg)),
            pl.BlockSpec((1, GDN_HB, GDN_DK, GDN_DV), lambda bi, hg, i: (bi, hg, 0, 0)),
        ],
        out_shape=[jax.ShapeDtypeStruct((b, t, GDN_WIDTH), bf16),
                   jax.ShapeDtypeStruct((b, GDN_HEADS, GDN_DK, GDN_DV), f32)],
        scratch_shapes=[pltpu.VMEM((GDN_HB, GDN_DK, GDN_DV), f32)],
        compiler_params=_cparams(("parallel", "parallel", "arbitrary")),
    )(q, k, v, z3, gcb, btb, gc_row, s0, gdn_norm_row)


GS_BB = 8


def _gdn_step_kernel(q_ref, k_ref, v_ref, gz_ref, bg_ref, s0_ref, nrm_ref, o_ref, so_ref):
    for bb in range(GS_BB):
        row = slice(bb, bb + 1)
        heads = [slice(h * LANES, (h + 1) * LANES) for h in range(GDN_HEADS)]
        cols = [(jnp.broadcast_to(k_ref[row, sl], (GDN_DK, LANES)).T,
                 jnp.broadcast_to(q_ref[row, sl], (GDN_DK, LANES)).T) for sl in heads]
        for h, sl in enumerate(heads):
            s0 = s0_ref[bb, h]
            q = q_ref[row, sl]
            k = k_ref[row, sl]
            v = v_ref[row, sl]
            eg = jnp.exp(bg_ref[row, SM_GA + h:SM_GA + h + 1])
            bt = bg_ref[row, SM_GB + h:SM_GB + h + 1]
            kc, qc = cols[h]
            ks = jnp.sum(kc * s0, axis=0, keepdims=True)
            qs = jnp.sum(qc * s0, axis=0, keepdims=True)
            qk = jnp.sum(q * k, axis=-1, keepdims=True)
            v_new = bt * v - (bt * eg) * ks
            o = eg * qs + qk * v_new
            so_ref[bb, h] = s0 * eg + kc * v_new
            ms = jnp.mean(o * o, axis=-1, keepdims=True)
            on = o * lax.rsqrt(ms + EPS) * nrm_ref[...]
            o_ref[row, sl] = on * _silu(gz_ref[row, sl])


def _gdn_step(q, k, v, zs, bgs, s0, gdn_norm_row):
    db = q.shape[0]
    tok = pl.BlockSpec((GS_BB, HK), lambda i: (i, 0))
    st = pl.BlockSpec((GS_BB, GDN_HEADS, GDN_DK, GDN_DV), lambda i: (i, 0, 0, 0))
    return pl.pallas_call(
        _gdn_step_kernel,
        grid=(db // GS_BB,),
        in_specs=[tok, tok, tok, pl.BlockSpec((GS_BB, GDN_WIDTH), lambda i: (i, C_GZ // GDN_WIDTH)),
                  pl.BlockSpec((GS_BB, LANES), lambda i: (i, 0)), st,
                  pl.BlockSpec((1, LANES), lambda i: (0, 0))],
        out_specs=[tok, st],
        out_shape=[jax.ShapeDtypeStruct((db, GDN_WIDTH), f32),
                   jax.ShapeDtypeStruct((db, GDN_HEADS, GDN_DK, GDN_DV), f32)],
        compiler_params=_cparams(("parallel",)),
    )(q, k, v, zs, bgs, s0, gdn_norm_row)


def _rope(x, cos, sin_signed):
    return x * cos + pltpu.roll(x, LANES // 2, axis=1) * sin_signed


def _dsa_prep_kernel(aq_ref, ak_ref, av_ref, iq_ref, ik_ref, sm_ref, cos_ref, sin_ref,
                     qn_ref, kn_ref, *out_refs, transposed, q_scale):
    if transposed:
        qb_ref, kf_ref, vf_ref, kb_ref, vt_ref, qi_ref, kif_ref, kib_ref, wt_ref = out_refs
    else:
        qb_ref, kf_ref, vf_ref, qi_ref, kif_ref, kib_ref = out_refs
    cos = cos_ref[...]
    sin = sin_ref[...]
    for h in range(ATT_HEADS):
        sl = slice(h * LANES, (h + 1) * LANES)
        a = aq_ref[0, :, sl]
        a = a * lax.rsqrt(jnp.mean(a * a, axis=-1, keepdims=True) + EPS) * qn_ref[...]
        qb_ref[0, :, sl] = (_rope(a, cos, sin) * q_scale).astype(qb_ref.dtype)
        a = ak_ref[0, :, sl]
        a = a * lax.rsqrt(jnp.mean(a * a, axis=-1, keepdims=True) + EPS) * kn_ref[...]
        kr = _rope(a, cos, sin)
        kf_ref[0, :, sl] = kr
        v = av_ref[0, :, sl]
        vf_ref[0, :, sl] = v
        if transposed:
            kb_ref[0, :, sl] = kr.astype(bf16)
            vt_ref[0, 0, sl, :] = v.T.astype(bf16)
    for h in range(IDX_HEADS):
        sl = slice(h * LANES, (h + 1) * LANES)
        qi_ref[0, :, sl] = _rope(iq_ref[0, :, sl], cos, sin).astype(bf16)
    ki = _rope(ik_ref[0], cos, sin)
    kif_ref[0] = ki
    kib_ref[0] = ki.astype(bf16)
    if transposed:
        wt_ref[0] = (sm_ref[0] * (IDX_HEADS ** -0.5 * IDX_DH ** -0.5)).T


def _dsa_prep(z3, cos, sin, qn_row, kn_row, tm, transposed, kc, q_scale):
    b, t, _ = z3.shape
    zspec = lambda w, off: pl.BlockSpec((1, tm, w), lambda bi, i: (bi, i, off // w))
    tok = lambda w, dt: (pl.BlockSpec((1, tm, w), lambda bi, i: (bi, i, 0)),
                         jax.ShapeDtypeStruct((b, t, w), dt))
    if transposed:
        per = kc // tm
        outs = [tok(ATT_WIDTH, bf16), tok(ATT_WIDTH, f32), tok(ATT_WIDTH, f32), tok(ATT_WIDTH, bf16),
                (pl.BlockSpec((1, 1, ATT_WIDTH, tm), lambda bi, i: (bi, i // per, 0, i % per)),
                 jax.ShapeDtypeStruct((b, t // kc, ATT_WIDTH, kc), bf16)),
                tok(IDX_HEADS * IDX_DH, bf16), tok(IDX_DH, f32), tok(IDX_DH, bf16),
                (pl.BlockSpec((1, LANES, tm), lambda bi, i: (bi, 0, i)),
                 jax.ShapeDtypeStruct((b, LANES, t), f32))]
    else:
        outs = [tok(ATT_WIDTH, f32), tok(ATT_WIDTH, f32), tok(ATT_WIDTH, f32),
                tok(IDX_HEADS * IDX_DH, bf16), tok(IDX_DH, f32), tok(IDX_DH, bf16)]
    kern = functools.partial(_dsa_prep_kernel, transposed=transposed, q_scale=q_scale)
    return pl.pallas_call(
        kern,
        grid=(b, t // tm),
        in_specs=[
            zspec(ATT_WIDTH, C_AQ), zspec(ATT_WIDTH, C_AK), zspec(ATT_WIDTH, C_AV),
            zspec(IDX_HEADS * IDX_DH, C_IQ), zspec(IDX_DH, C_IK), zspec(LANES, C_SMALL),
            pl.BlockSpec((tm, LANES), lambda bi, i: (i, 0)),
            pl.BlockSpec((tm, LANES), lambda bi, i: (i, 0)),
            pl.BlockSpec((1, LANES), lambda bi, i: (0, 0)),
            pl.BlockSpec((1, LANES), lambda bi, i: (0, 0)),
        ],
        out_specs=[o[0] for o in outs],
        out_shape=[o[1] for o in outs],
        compiler_params=_cparams(("parallel", "parallel")),
    )(z3, z3, z3, z3, z3, z3, cos, sin, qn_row, kn_row)


def _key_to_float(key):
    bits = jnp.where(key >= 0, key, key ^ jnp.int32(0x7FFFFFFF))
    return pltpu.bitcast(bits, f32)


def _kth_largest(count_ge, k, shape):
    kf = jnp.float32(k)
    zero_ok = count_ge(jnp.zeros(shape, f32)) >= kf
    cur = jnp.where(zero_ok, jnp.int32(0), jnp.int32(INT_MIN))

    def body(it, cur):
        cand = cur + jnp.left_shift(jnp.int32(1), jnp.int32(30) - it)
        ok = count_ge(_key_to_float(cand)) >= kf
        return jnp.where(ok, cand, cur)

    cur = lax.fori_loop(0, 31, body, cur)
    return _key_to_float(cur)


def _tie_bound(count_eq_le, need, nbits, shape):
    lo = jnp.full(shape, -1, i32)

    def body(it, lo):
        cand = lo + jnp.left_shift(jnp.int32(1), jnp.int32(nbits - 1) - it)
        short = count_eq_le(cand) < need
        return jnp.where(short, cand, lo)

    lo = lax.fori_loop(0, nbits, body, lo)
    return lo + 1


ATT_KC = 512
ATT_QB = 256
ATT_KS = 128
CNT_ROWS = 64
LOG2E = 1.4426950408889634


def _dsa_attend_kernel(qi_ref, wt_ref, ki_ref, qb_ref, kb_ref, vt_ref, az_ref, o_ref,
                       sc_ref, sc16_ref, *head_refs, t_total):
    i = pl.program_id(1)
    qb = ATT_QB
    kc = ATT_KC
    nch = (i * qb + qb + kc - 1) // kc
    tq = i * qb + lax.broadcasted_iota(i32, (1, qb), 1)
    ktop = min(TOPK_MAX, t_total // 4)

    def spos(c):
        return c * kc + lax.broadcasted_iota(i32, (kc, 1), 0)

    def idx_body(c, carry):
        for sub in range(kc // ATT_KS):
            base = c * kc + sub * ATT_KS
            ksub = ki_ref[0, pl.ds(base, ATT_KS), :]
            acc = jnp.zeros((ATT_KS, qb), f32)
            for h in range(IDX_HEADS):
                d = _dot_nt(ksub, qi_ref[0, :, h * LANES:(h + 1) * LANES])
                acc = acc + jnp.maximum(d, 0.0) * wt_ref[0, SM_IW + h:SM_IW + h + 1, :]
            sp = base + lax.broadcasted_iota(i32, (ATT_KS, 1), 0)
            masked = jnp.where(sp <= tq, acc, -jnp.inf)
            sc_ref[pl.ds(base, ATT_KS), :] = masked
            sc16_ref[pl.ds(base, ATT_KS), :] = masked.astype(bf16)
        return carry

    lax.fori_loop(0, nch, idx_body, 0)

    def col_count(pred_fn):
        def body(c, acc):
            m = pred_fn(sc_ref[pl.ds(c * kc, kc), :], c)
            return acc + jnp.sum(jnp.where(m, 1.0, 0.0).reshape(kc // CNT_ROWS, CNT_ROWS, qb), axis=0)
        acc = lax.fori_loop(0, nch, body, jnp.zeros((CNT_ROWS, qb), f32))
        return jnp.sum(acc, axis=0, keepdims=True)

    def col_count16(cand):
        cand16 = cand.astype(bf16)

        def body(c, acc):
            one = jnp.where(sc16_ref[pl.ds(c * kc, kc), :] >= cand16, jnp.bfloat16(1.0), jnp.bfloat16(0.0))
            for r in range(kc // CNT_ROWS):
                acc = acc + one[r * CNT_ROWS:(r + 1) * CNT_ROWS]
            return acc
        acc = lax.fori_loop(0, nch, body, jnp.zeros((CNT_ROWS, qb), bf16))
        return jnp.sum(acc.astype(f32), axis=0, keepdims=True)

    def search():
        kf = jnp.float32(ktop)
        zero_ok = col_count16(jnp.zeros((1, qb), f32)) >= kf
        cur = jnp.where(zero_ok, jnp.int32(0), jnp.int32(INT_MIN))

        def body16(it, cur):
            cand = cur + jnp.left_shift(jnp.int32(1), jnp.int32(30) - it)
            return jnp.where(col_count16(_key_to_float(cand)) >= kf, cand, cur)

        cur = lax.fori_loop(0, 15, body16, cur)
        step = 1 << 16
        lo = jnp.maximum(jnp.maximum(cur, jnp.int32(INT_MIN + step)) - step, jnp.int32(KEY_MOST_NEGATIVE))

        def body32(it, carry):
            off, cnt = carry
            cand = off + jnp.left_shift(jnp.int32(1), jnp.int32(16) - it)
            cand_f = _key_to_float(lo + cand)
            c_new = col_count(lambda s, c: s >= cand_f)
            ok = c_new >= kf
            return jnp.where(ok, cand, off), jnp.where(ok, c_new, cnt)

        off, c_ge = lax.fori_loop(0, 17, body32, (jnp.zeros((1, qb), i32), jnp.full((1, qb), kf + 1.0, f32)))
        thr = _key_to_float(lo + off)

        def tie_bound():
            need = kf - col_count(lambda s, c: s > thr)
            return _tie_bound(
                lambda cand: col_count(lambda s, c: (s == thr) & (spos(c) <= cand)),
                need, max(1, (t_total - 1).bit_length()), (1, qb))

        jb = lax.cond(jnp.max(c_ge) > kf, tie_bound, lambda: jnp.full((1, qb), t_total, i32))
        return thr, jb

    def take_all():
        return jnp.full((1, qb), -jnp.inf, f32), jnp.full((1, qb), t_total, i32)

    thr, jb = lax.cond((i + 1) * qb <= ktop, take_all, search)

    def bias_body(c, carry):
        s = sc_ref[pl.ds(c * kc, kc), :]
        sp = spos(c)
        sel = ((s > thr) | ((s == thr) & (sp <= jb))) & (sp <= tq)
        sc_ref[pl.ds(c * kc, kc), :] = jnp.where(sel, 0.0, NEG_BIG)
        return carry

    lax.fori_loop(0, nch, bias_body, 0)

    nh = ATT_HEADS
    m_refs, l_refs, al_refs, acc_refs, s_refs, p_refs = (head_refs[j * nh:(j + 1) * nh] for j in range(6))
    for h in range(nh):
        m_refs[h][...] = jnp.full(m_refs[h].shape, NEG_BIG, f32)
        l_refs[h][...] = jnp.zeros(l_refs[h].shape, f32)
        acc_refs[h][...] = jnp.zeros(acc_refs[h].shape, f32)

    def att_body(c, carry):
        for h in range(nh):
            sl = slice(h * LANES, (h + 1) * LANES)
            s_refs[h][...] = _dot_nt(kb_ref[0, pl.ds(c * kc, kc), sl], qb_ref[0, :, sl])
        for h in range(nh):
            s = s_refs[h][...] + sc_ref[pl.ds(c * kc, kc), :]
            m = m_refs[h][...]
            m_new = jnp.maximum(m, jnp.max(s, axis=0, keepdims=True))
            alpha = jnp.exp2(m - m_new)
            p = jnp.exp2(s - m_new)
            m_refs[h][...] = m_new
            al_refs[h][...] = alpha
            l_refs[h][...] = alpha * l_refs[h][...] + jnp.sum(p, axis=0, keepdims=True)
            p_refs[h][...] = p.astype(bf16)
        for h in range(nh):
            sl = slice(h * LANES, (h + 1) * LANES)
            acc_refs[h][...] = (al_refs[h][...] * acc_refs[h][...]
                                + _dot(vt_ref[0, c, sl, :], p_refs[h][...]))
        return carry

    lax.fori_loop(0, nch, att_body, 0)

    for h in range(ATT_HEADS):
        sl = slice(h * LANES, (h + 1) * LANES)
        out = (acc_refs[h][...] / l_refs[h][...]).T
        o_ref[0, :, sl] = (out * _silu(az_ref[0, :, sl])).astype(o_ref.dtype)


def _dsa_attend(qi, wt, ki, qb, kb, vt, z3):
    b, t, _ = qb.shape
    kern = functools.partial(_dsa_attend_kernel, t_total=t)
    return pl.pallas_call(
        kern,
        grid=(b, t // ATT_QB),
        in_specs=[
            pl.BlockSpec((1, ATT_QB, IDX_HEADS * IDX_DH), lambda bi, i: (bi, i, 0)),
            pl.BlockSpec((1, LANES, ATT_QB), lambda bi, i: (bi, 0, i)),
            pl.BlockSpec((1, t, IDX_DH), lambda bi, i: (bi, 0, 0)),
            pl.BlockSpec((1, ATT_QB, ATT_WIDTH), lambda bi, i: (bi, i, 0)),
            pl.BlockSpec((1, t, ATT_WIDTH), lambda bi, i: (bi, 0, 0)),
            pl.BlockSpec((1, t // ATT_KC, ATT_WIDTH, ATT_KC), lambda bi, i: (bi, 0, 0, 0)),
            pl.BlockSpec((1, ATT_QB, ATT_WIDTH), lambda bi, i: (bi, i, C_AZ // ATT_WIDTH)),
        ],
        out_specs=pl.BlockSpec((1, ATT_QB, ATT_WIDTH), lambda bi, i: (bi, i, 0)),
        out_shape=jax.ShapeDtypeStruct((b, t, ATT_WIDTH), bf16),
        scratch_shapes=([pltpu.VMEM((t, ATT_QB), f32), pltpu.VMEM((t, ATT_QB), bf16)]
                        + [pltpu.VMEM((1, ATT_QB), f32) for _ in range(3 * ATT_HEADS)]
                        + [pltpu.VMEM((LANES, ATT_QB), f32) for _ in range(ATT_HEADS)]
                        + [pltpu.VMEM((ATT_KC, ATT_QB), f32) for _ in range(ATT_HEADS)]
                        + [pltpu.VMEM((ATT_KC, ATT_QB), bf16) for _ in range(ATT_HEADS)]),
        compiler_params=_cparams(("parallel", "arbitrary")),
    )(qi, wt, ki, qb, kb, vt, z3)


SC_PG = 16
DMA_UNROLL = 8


def _sample_scores_kernel(pt_s, qi_ref, w_ref, cik_ref, o_ref, buf, sem, *, n_pages):
    b = pl.program_id(0)
    nb = pl.num_programs(0)

    def page_copy(bb, slot, p):
        return pltpu.make_async_copy(cik_ref.at[pt_s[bb * n_pages + p]], buf.at[slot, p], sem.at[slot])

    def issue(bb, slot):
        def body(p, carry):
            page_copy(bb, slot, p).start()
            return carry
        lax.fori_loop(0, n_pages, body, 0, unroll=DMA_UNROLL)

    def wait_all(bb, slot):
        def body(p, carry):
            page_copy(bb, slot, p).wait()
            return carry
        lax.fori_loop(0, n_pages, body, 0, unroll=DMA_UNROLL)

    slot = b % 2

    @pl.when(b == 0)
    def _():
        issue(b, slot)

    @pl.when(b + 1 < nb)
    def _():
        issue(b + 1, 1 - slot)

    wait_all(b, slot)

    qi = qi_ref[0]
    w = w_ref[0]
    gk = SC_PG * PAGE_SIZE
    for g in range(n_pages // SC_PG):
        keys = buf[slot, g * SC_PG:(g + 1) * SC_PG].reshape(gk, IDX_DH).astype(bf16)
        d = _dot_nt(qi, keys)
        o_ref[0, :, g * gk:(g + 1) * gk] = jnp.sum(jnp.maximum(d, 0.0) * w, axis=0, keepdims=True)


def _sample_scores(pt_flat, qi3, w3, cache_ik):
    db = qi3.shape[0]
    n_pages = pt_flat.shape[0] // db
    kern = functools.partial(_sample_scores_kernel, n_pages=n_pages)
    grid_spec = pltpu.PrefetchScalarGridSpec(
        num_scalar_prefetch=1,
        grid=(db,),
        in_specs=[pl.BlockSpec((1, IDX_HEADS, IDX_DH), lambda bi, pt: (bi, 0, 0)),
                  pl.BlockSpec((1, IDX_HEADS, 1), lambda bi, pt: (bi, 0, 0)),
                  pl.BlockSpec(memory_space=pl.ANY)],
        out_specs=pl.BlockSpec((1, 1, n_pages * PAGE_SIZE), lambda bi, pt: (bi, 0, 0)),
        scratch_shapes=[pltpu.VMEM((2, n_pages, PAGE_SIZE, IDX_DH), f32), pltpu.SemaphoreType.DMA((2,))],
    )
    return pl.pallas_call(
        kern,
        grid_spec=grid_spec,
        out_shape=jax.ShapeDtypeStruct((db, 1, n_pages * PAGE_SIZE), f32),
        compiler_params=_cparams(("arbitrary",)),
    )(pt_flat, qi3, w3, cache_ik)


def _sample_select_kernel(sc_ref, qi_ref, ki_ref, w_ref, pt_ref, idx_ref, rows_ref, *, n_past, ktop):
    db, n_pages, ps = sc_ref.shape
    sc = sc_ref[...]

    def red(x):
        return jnp.sum(jnp.sum(x, axis=1, keepdims=True), axis=2, keepdims=True)

    dn = jnp.sum(qi_ref[...].astype(f32) * ki_ref[...].astype(f32), axis=2, keepdims=True)
    s_new = jnp.sum(jnp.maximum(dn, 0.0) * w_ref[...], axis=1, keepdims=True)

    def count(pred):
        return red(pred(sc).astype(f32)) + pred(s_new).astype(f32)

    shape = (db, 1, 1)
    thr = _kth_largest(lambda cand: count(lambda s: s >= cand), ktop, shape)
    c_ge = count(lambda s: s >= thr)
    c_gt = count(lambda s: s > thr)
    need = jnp.float32(ktop) - c_gt
    pos = (lax.broadcasted_iota(i32, (1, n_pages, ps), 1) * ps
           + lax.broadcasted_iota(i32, (1, n_pages, ps), 2))

    def eq_le(cand):
        return (red(((sc == thr) & (pos <= cand)).astype(f32))
                + ((s_new == thr) & (n_past <= cand)).astype(f32))

    jb = lax.cond(jnp.max(c_ge) > jnp.float32(ktop),
                  lambda: _tie_bound(eq_le, need, (n_past + 1).bit_length(), shape),
                  lambda: jnp.full(shape, n_past + 1, i32))
    sel = (sc > thr) | ((sc == thr) & (pos <= jb))

    r_i = lax.broadcasted_iota(i32, (ps, ps), 0)
    c_i = lax.broadcasted_iota(i32, (ps, ps), 1)
    upper = (r_i <= c_i).astype(bf16)
    upper_pg = (lax.broadcasted_iota(i32, (n_pages, n_pages), 0)
                <= lax.broadcasted_iota(i32, (n_pages, n_pages), 1)).astype(bf16)
    jcol = lax.broadcasted_iota(i32, (ktop, 1), 0).astype(f32)
    plane = lax.broadcasted_iota(i32, (1, n_pages), 1).astype(f32)
    ones8 = jnp.ones((8, ps), bf16)

    selb = jnp.where(sel, 1.0, 0.0).astype(bf16)
    for b in range(db):
        sb = selb[b]
        incl = _dot(sb, upper)
        tot_row = _dot_nt(ones8, sb)[0:1]
        cum_row = _dot(jnp.broadcast_to(tot_row, (8, n_pages)).astype(bf16), upper_pg)[0:1]
        page_of = jnp.sum((cum_row <= jcol).astype(f32), axis=1, keepdims=True)
        onehot = (page_of == plane)
        before = jnp.sum(jnp.where(onehot, cum_row - tot_row, 0.0), axis=1, keepdims=True)
        rloc = jcol - before
        incl_rows = _dot(onehot.astype(bf16), incl.astype(bf16))
        off_of = jnp.sum((incl_rows <= rloc).astype(f32), axis=1, keepdims=True)
        idx = jnp.minimum(page_of * ps + off_of, jnp.float32(n_past))
        idx_ref[b] = idx.astype(i32)
        pidx = jnp.minimum(idx, jnp.float32(n_past - 1))
        pg = jnp.floor(pidx * (1.0 / ps))
        phys = jnp.sum(jnp.where(pg == plane, pt_ref[b], 0.0), axis=1, keepdims=True)
        rows_ref[b] = (phys * ps + (pidx - pg * ps)).astype(i32)


def _sample_select(sc3, qi3, ki3, w3, pt3, n_past, ktop):
    db = sc3.shape[0]
    kern = functools.partial(_sample_select_kernel, n_past=n_past, ktop=ktop)
    full = lambda a: pl.BlockSpec(a.shape, lambda i: (0,) * a.ndim)
    out_spec = pl.BlockSpec((db, ktop, 1), lambda i: (0, 0, 0))
    out_sd = jax.ShapeDtypeStruct((db, ktop, 1), i32)
    return pl.pallas_call(
        kern,
        grid=(1,),
        in_specs=[full(sc3), full(qi3), full(ki3), full(w3), full(pt3)],
        out_specs=[out_spec, out_spec],
        out_shape=[out_sd, out_sd],
        compiler_params=_cparams(("arbitrary",)),
    )(sc3, qi3, ki3, w3, pt3)


def _sample_attend_kernel(rows_s, idxv_ref, q_ref, kn_ref, vn_ref, az_ref, ck_ref, cv_ref,
                          o_ref, kbuf, vbuf, sem, *, n_past, ktop):
    b = pl.program_id(0)
    nb = pl.num_programs(0)

    def row_copies(bb, slot, j):
        row = rows_s[bb * ktop + j]
        ck = pltpu.make_async_copy(ck_ref.at[row], kbuf.at[slot, :, j, :], sem.at[0, slot])
        cv = pltpu.make_async_copy(cv_ref.at[row], vbuf.at[slot, :, j, :], sem.at[1, slot])
        return ck, cv

    def issue(bb, slot):
        def body(j, carry):
            ck, cv = row_copies(bb, slot, j)
            ck.start()
            cv.start()
            return carry
        lax.fori_loop(0, ktop, body, 0, unroll=DMA_UNROLL)

    def wait_all(bb, slot):
        def body(j, carry):
            ck, cv = row_copies(bb, slot, j)
            ck.wait()
            cv.wait()
            return carry
        lax.fori_loop(0, ktop, body, 0, unroll=DMA_UNROLL)

    slot = b % 2

    @pl.when(b == 0)
    def _():
        issue(b, slot)

    @pl.when(b + 1 < nb)
    def _():
        issue(b + 1, 1 - slot)

    wait_all(b, slot)

    is_new = idxv_ref[0] >= n_past
    newf = is_new.astype(f32)
    heads = [slice(h * LANES, (h + 1) * LANES) for h in range(ATT_HEADS)]
    scores = []
    for h, sl in enumerate(heads):
        q8 = jnp.broadcast_to(q_ref[0, :, sl], (8, LANES)).astype(bf16)
        scores.append(_dot_nt(q8, kbuf[slot, h].astype(bf16))[0:1])
    probs = []
    for h, sl in enumerate(heads):
        qh = q_ref[0, :, sl]
        s_new = jnp.sum(qh.astype(bf16).astype(f32) * kn_ref[0, :, sl].astype(bf16).astype(f32),
                        axis=-1, keepdims=True)
        s = jnp.where(is_new, s_new, scores[h])
        p = jnp.exp(s - jnp.max(s, axis=-1, keepdims=True))
        probs.append((p, jnp.sum(p, axis=-1, keepdims=True)))
    for h, sl in enumerate(heads):
        p, l = probs[h]
        p_old = jnp.broadcast_to(p * (1.0 - newf), (8, ktop)).astype(bf16)
        pv = _dot(p_old, vbuf[slot, h].astype(bf16))[0:1]
        pv = pv + jnp.sum(p * newf, axis=-1, keepdims=True) * vn_ref[0, :, sl]
        o_ref[0, :, sl] = ((pv / l) * _silu(az_ref[0, :, sl])).astype(o_ref.dtype)


def _sample_attend(rows_flat, idx_row, q3, kn3, vn3, az3, ck3, cv3, n_past, ktop):
    db = q3.shape[0]
    kern = functools.partial(_sample_attend_kernel, n_past=n_past, ktop=ktop)
    tok = pl.BlockSpec((1, 1, ATT_WIDTH), lambda bi, r: (bi, 0, 0))
    grid_spec = pltpu.PrefetchScalarGridSpec(
        num_scalar_prefetch=1,
        grid=(db,),
        in_specs=[pl.BlockSpec((1, 1, ktop), lambda bi, r: (bi, 0, 0)), tok, tok, tok, tok,
                  pl.BlockSpec(memory_space=pl.ANY), pl.BlockSpec(memory_space=pl.ANY)],
        out_specs=tok,
        scratch_shapes=[pltpu.VMEM((2, ATT_HEADS, ktop, ATT_DH), f32),
                        pltpu.VMEM((2, ATT_HEADS, ktop, ATT_DH), f32),
                        pltpu.SemaphoreType.DMA((2, 2))],
    )
    return pl.pallas_call(
        kern,
        grid_spec=grid_spec,
        out_shape=jax.ShapeDtypeStruct((db, 1, ATT_WIDTH), bf16),
        compiler_params=_cparams(("arbitrary",)),
    )(rows_flat, idx_row, q3, kn3, vn3, az3, ck3, cv3)


def _tail_kernel(oa_ref, ob_ref, mga_ref, mgb_ref, x_ref, p_ref, wa_ref, wb_ref, wo_ref,
                 pn_ref, wg_ref, wp_ref, y_ref):
    a = _dot(oa_ref[...].astype(bf16), wa_ref[...])
    b = _dot(ob_ref[...].astype(bf16), wb_ref[...])
    merged = _sigmoid(mga_ref[...]) * a + _sigmoid(mgb_ref[...]) * b
    x2 = x_ref[...] + _dot(merged.astype(bf16), wo_ref[...])
    ms = jnp.mean(x2 * x2, axis=-1, keepdims=True)
    hn = (x2 * lax.rsqrt(ms + EPS) * pn_ref[...]).astype(bf16)
    gate = _sigmoid(_dot(hn, wg_ref[...]))
    y_ref[...] = x2 + gate * _dot(p_ref[...].astype(bf16), wp_ref[...])


def _tail(oa, ob, z2d, x2d, p2d, wa, wb, wo, pn_row, wg, wp, tm):
    m = x2d.shape[0]
    const = lambda a: pl.BlockSpec(a.shape, lambda i: (0, 0), pipeline_mode=pl.Buffered(1))
    return pl.pallas_call(
        _tail_kernel,
        grid=(m // tm,),
        in_specs=[
            pl.BlockSpec((tm, GDN_WIDTH), lambda i: (i, 0)),
            pl.BlockSpec((tm, ATT_WIDTH), lambda i: (i, 0)),
            pl.BlockSpec((tm, D_MODEL), lambda i: (i, C_MGA // D_MODEL)),
            pl.BlockSpec((tm, D_MODEL), lambda i: (i, C_MGB // D_MODEL)),
            pl.BlockSpec((tm, D_MODEL), lambda i: (i, 0)),
            pl.BlockSpec((tm, PLE_DIM), lambda i: (i, 0)),
            const(wa), const(wb), const(wo), const(pn_row), const(wg), const(wp),
        ],
        out_specs=pl.BlockSpec((tm, D_MODEL), lambda i: (i, 0)),
        out_shape=jax.ShapeDtypeStruct((m, D_MODEL), f32),
        compiler_params=_cparams(("parallel",)),
    )(oa, ob, z2d, z2d, x2d, p2d, wa, wb, wo, pn_row, wg, wp)


def _lane_row(v, offset=0):
    row = jnp.zeros((1, LANES), f32)
    return row.at[0, offset:offset + v.shape[0]].set(v.astype(f32))


def _rope_tables(pos):
    half = ATT_DH // 2
    inv = ROPE_THETA ** (-jnp.arange(half, dtype=f32) * 2.0 / ATT_DH)
    ang = pos.astype(f32)[:, None] * inv[None, :]
    cos, sin = jnp.cos(ang), jnp.sin(ang)
    return jnp.concatenate([cos, cos], axis=1), jnp.concatenate([-sin, sin], axis=1)


def kernel(x_prompt, x_sample, p_prompt, p_sample, cache_k, cache_v, cache_idx_k, state_gdn, state_conv,
           page_table, norm_in, w_in, conv_w, a_log, dt_bias, gdn_norm, q_norm, k_norm,
           w_proj_a, w_proj_b, w_out, ple_norm, w_ple_gate, w_ple_proj):
    depth = norm_in.shape[0]
    assert depth == 1, "single-layer trunk"
    B, T, _ = x_prompt.shape
    DB, S_new, _ = x_sample.shape
    assert S_new == 1
    n_pages = page_table.shape[1]
    n_past = n_pages * PAGE_SIZE
    n_pool = cache_k.shape[1]
    li = 0

    assert w_in.shape[2] == D_IN
    w_perm = _wprep(jnp.swapaxes(w_in[li], 0, 1))
    gain_in = norm_in[li].reshape(1, D_MODEL)
    a_row = _lane_row(a_log[li], SM_GA)
    dt_row = _lane_row(dt_bias[li], SM_GA)
    gdn_norm_row = gdn_norm[li].reshape(1, LANES)
    qn_row = q_norm[li].reshape(1, LANES)
    kn_row = k_norm[li].reshape(1, LANES)
    wa = w_proj_a[li].astype(bf16)
    wb = w_proj_b[li].astype(bf16)
    wo = w_out[li].astype(bf16)
    wg = w_ple_gate[li].astype(bf16)
    wp = w_ple_proj[li].astype(bf16)
    pn_row = ple_norm[li].reshape(1, D_MODEL)
    cw = conv_w[li]

    xp2 = x_prompt.reshape(B * T, D_MODEL)
    zp = _inproj(xp2, gain_in, w_perm, tm=1024, tn=NZ // 6)
    zp3 = zp.reshape(B, T, NZ)

    prev8 = jnp.zeros((B, 8, CONV_CH), f32)
    qa, ka, va, gcb, btb, bg = _gdn_prep(zp3, prev8, cw, a_row, dt_row, tm=256)
    gc_row = jnp.swapaxes(bg[..., SM_GA:SM_GA + GDN_HEADS], 1, 2).reshape(
        B, GDN_HEADS, T // GDN_CHUNK, 1, GDN_CHUNK)
    s00 = jnp.zeros((B, GDN_HEADS, GDN_DK, GDN_DV), f32)
    oa_p, s_p = _gdn_scan(qa, ka, va, zp3, gcb, btb, gc_row, s00, gdn_norm_row)
    conv_p = zp3[:, T - (CONV_W - 1):, C_GQ:C_GQ + CONV_CH]

    cos_p, sin_p = _rope_tables(jnp.arange(T))
    qb_p, kf_p, v_p, kb_p, vt_p, qi_p, kif_p, kib_p, wt_p = _dsa_prep(
        zp3, cos_p, sin_p, qn_row, kn_row, tm=256, transposed=True, kc=ATT_KC,
        q_scale=LOG2E * ATT_DH ** -0.5)
    ob_p = _dsa_attend(qi_p, wt_p, kib_p, qb_p, kb_p, vt_p, zp3)

    y_p = _tail(oa_p.reshape(B * T, GDN_WIDTH), ob_p.reshape(B * T, ATT_WIDTH), zp, xp2,
                p_prompt[li].reshape(B * T, PLE_DIM), wa, wb, wo, pn_row, wg, wp, tm=256)

    xs2 = x_sample.reshape(DB, D_MODEL)
    zs = _inproj(xs2, gain_in, w_perm, tm=DB, tn=NZ // 6)
    sconv = state_conv[li]
    qs, ks, vs, bgs = _gdn_prep_sample(zs, sconv[:, 0], sconv[:, 1], sconv[:, 2], cw, a_row, dt_row)
    oa_s, s_s = _gdn_step(qs, ks, vs, zs, bgs, state_gdn[li], gdn_norm_row)
    conv_s = jnp.concatenate([sconv[:, 1:], zs[:, None, C_GQ:C_GQ + CONV_CH]], axis=1)

    cos_s, sin_s = _rope_tables(jnp.full((DB,), n_past))
    qf_s, kf_s, v_s, qi_s, kif_s, kib_s = _dsa_prep(
        zs.reshape(1, DB, NZ), cos_s, sin_s, qn_row, kn_row, tm=DB, transposed=False, kc=ATT_KC,
        q_scale=ATT_DH ** -0.5)
    wi_s = zs[:, C_SMALL + SM_IW:C_SMALL + SM_IW + IDX_HEADS] * (IDX_HEADS ** -0.5 * IDX_DH ** -0.5)
    qi3 = qi_s.reshape(DB, IDX_HEADS, IDX_DH)
    w3 = wi_s.reshape(DB, IDX_HEADS, 1)
    pt_flat = page_table.reshape(-1).astype(i32)
    sc = _sample_scores(pt_flat, qi3, w3, cache_idx_k[li])
    ktop = min(TOPK_MAX, (n_past + S_new) // 4)
    idx, rows = _sample_select(sc.reshape(DB, n_pages, PAGE_SIZE), qi3, kib_s.reshape(DB, 1, IDX_DH), w3,
                               page_table.astype(f32).reshape(DB, 1, n_pages), n_past, ktop)
    ob_s = _sample_attend(rows.reshape(-1), idx.reshape(DB, 1, ktop),
                          qf_s.reshape(DB, 1, ATT_WIDTH), kf_s.reshape(DB, 1, ATT_WIDTH),
                          v_s.reshape(DB, 1, ATT_WIDTH), zs[:, C_AZ:C_AZ + ATT_WIDTH].reshape(DB, 1, ATT_WIDTH),
                          cache_k[li].reshape(n_pool * PAGE_SIZE, ATT_HEADS, ATT_DH),
                          cache_v[li].reshape(n_pool * PAGE_SIZE, ATT_HEADS, ATT_DH), n_past, ktop)
    y_s = _tail(oa_s.reshape(DB, GDN_WIDTH), ob_s.reshape(DB, ATT_WIDTH), zs, xs2,
                p_sample[li].reshape(DB, PLE_DIM), wa, wb, wo, pn_row, wg, wp, tm=DB)

    shp = (B, T, ATT_HEADS, ATT_DH)
    shs = (DB, S_new, ATT_HEADS, ATT_DH)
    return (y_p.reshape(B, T, D_MODEL), y_s.reshape(DB, S_new, D_MODEL),
            kf_p.reshape(shp)[None], v_p.reshape(shp)[None], kif_p[None],
            s_p[None], conv_p[None],
            kf_s.reshape(shs)[None], v_s.reshape(shs)[None], kif_s.reshape(DB, S_new, IDX_DH)[None],
            s_s[None], conv_s[None])
```

```python
import functools
import math

import jax
import jax.numpy as jnp
from jax import lax
from jax.experimental import pallas as pl
from jax.experimental.pallas import tpu as pltpu

f32 = jnp.float32
bf16 = jnp.bfloat16
i32 = jnp.int32

D_MODEL = 2048
PAGE_SIZE = 128
GDN_HEADS = 8
GDN_DK = 128
GDN_DV = 128
GDN_WIDTH = GDN_HEADS * GDN_DV
CONV_W = 4
CONV_CH = 2 * GDN_HEADS * GDN_DK + GDN_WIDTH
GDN_CHUNK = 64
ATT_HEADS = 8
ATT_DH = 128
ATT_WIDTH = ATT_HEADS * ATT_DH
IDX_HEADS = 16
IDX_DH = 128
TOPK_MAX = 256
QUERY_BLOCK = 128
ROPE_THETA = 10000.0
PLE_DIM = 256
EPS = 1e-6

LANES = 128
VMEM_LIMIT = 56 * 1024 * 1024
NEG_BIG = -1e30
INT_MIN = -(2 ** 31)
KEY_MOST_NEGATIVE = INT_MIN + 0x00800000

HK = GDN_HEADS * GDN_DK
C_GQ, C_GK, C_GV, C_GZ = 0, HK, 2 * HK, 3 * HK
C_AQ = 4 * HK
C_AK = C_AQ + ATT_WIDTH
C_AV = C_AK + ATT_WIDTH
C_AZ = C_AV + ATT_WIDTH
C_IQ = C_AZ + ATT_WIDTH
C_MGA = C_IQ + IDX_HEADS * IDX_DH
C_MGB = C_MGA + D_MODEL
C_IK = C_MGB + D_MODEL
C_SMALL = C_IK + IDX_DH
NZ = C_SMALL + LANES
SM_GA, SM_GB, SM_IW = 0, GDN_HEADS, 2 * GDN_HEADS


def _cparams(sem):
    return pltpu.CompilerParams(dimension_semantics=sem, vmem_limit_bytes=VMEM_LIMIT)


def _dot(a, b):
    return jnp.dot(a, b, preferred_element_type=f32)


def _dot_nt(a, b):
    return lax.dot_general(a, b, (((1,), (1,)), ((), ())), preferred_element_type=f32)


def _sigmoid(x):
    return 1.0 / (1.0 + jnp.exp(-x))


def _silu(x):
    return x * _sigmoid(x)


O_GA = 4 * HK
O_AQ = O_GA + 2 * GDN_HEADS
O_IQ = O_AQ + 4 * ATT_WIDTH
O_IK = O_IQ + IDX_HEADS * IDX_DH
O_IW = O_IK + IDX_DH
O_MGA = O_IW + IDX_HEADS
D_IN = O_MGA + 2 * D_MODEL


WP_TW = 2 * LANES


def _wprep_src_rows():
    rows = []
    for j in range(NZ // WP_TW):
        c = j * WP_TW
        if c < C_AQ:
            rows.append(c)
        elif c < C_MGA:
            rows.append(c - C_AQ + O_AQ)
        elif c < C_IK:
            rows.append(c - C_MGA + O_MGA)
        else:
            rows.append(O_IK)
    return jnp.asarray(rows, i32)


def _wprep_kernel(src_ref, a_ref, b_ref, o_ref):
    j = pl.program_id(0)
    a = a_ref[...]
    k = a.shape[1]
    small = jnp.concatenate([b_ref[...], a[IDX_DH:IDX_DH + IDX_HEADS],
                             jnp.zeros((LANES - SM_IW - IDX_HEADS, k), f32)], axis=0)
    last = jnp.concatenate([a[0:IDX_DH], small], axis=0)
    tile = jnp.where(j == pl.num_programs(0) - 1, last, a)
    o_ref[...] = tile.T.astype(bf16)


def _wprep(wt):
    assert O_IW == O_IK + IDX_DH and C_SMALL == C_IK + IDX_DH and NZ == C_IK + WP_TW
    assert O_GA % SM_IW == 0 and 2 * GDN_HEADS == SM_IW and C_IK % WP_TW == 0
    k = wt.shape[1]
    grid_spec = pltpu.PrefetchScalarGridSpec(
        num_scalar_prefetch=1,
        grid=(NZ // WP_TW,),
        in_specs=[pl.BlockSpec((pl.Element(WP_TW), pl.Element(k)),
                               lambda j, src: (pl.multiple_of(src[j], SM_IW), 0)),
                  pl.BlockSpec((SM_IW, k), lambda j, src: (O_GA // SM_IW, 0))],
        out_specs=pl.BlockSpec((k, WP_TW), lambda j, src: (0, j)),
    )
    return pl.pallas_call(
        _wprep_kernel,
        grid_spec=grid_spec,
        out_shape=jax.ShapeDtypeStruct((k, NZ), bf16),
        compiler_params=_cparams(("arbitrary",)),
    )(_wprep_src_rows(), wt, wt)


def _inproj_kernel(x_ref, g_ref, w_ref, o_ref, h_ref):
    @pl.when(pl.program_id(1) == 0)
    def _():
        x = x_ref[...]
        ms = jnp.mean(x * x, axis=-1, keepdims=True)
        h_ref[...] = (x * lax.rsqrt(ms + EPS) * g_ref[...]).astype(bf16)

    o_ref[...] = _dot(h_ref[...], w_ref[...])


def _inproj(x2d, gain_row, w_bf16, tm, tn):
    m, k = x2d.shape
    n = w_bf16.shape[1]
    return pl.pallas_call(
        _inproj_kernel,
        grid=(m // tm, n // tn),
        in_specs=[
            pl.BlockSpec((tm, k), lambda i, j: (i, 0)),
            pl.BlockSpec((1, k), lambda i, j: (0, 0)),
            pl.BlockSpec((k, tn), lambda i, j: (0, j)),
        ],
        out_specs=pl.BlockSpec((tm, tn), lambda i, j: (i, j)),
        out_shape=jax.ShapeDtypeStruct((m, n), f32),
        scratch_shapes=[pltpu.VMEM((tm, k), bf16)],
        compiler_params=_cparams(("parallel", "arbitrary")),
    )(x2d, gain_row, w_bf16)


def _softplus(x):
    return jnp.maximum(x, 0.0) + jnp.log1p(jnp.exp(-jnp.abs(x)))


def _gdn_heads_out(conv_fn, q_ref, k_ref, v_ref):
    for c in range(3 * GDN_HEADS):
        a = _silu(conv_fn(c))
        h = c % GDN_HEADS
        sl = slice(h * LANES, (h + 1) * LANES)
        if c < 2 * GDN_HEADS:
            nrm = lax.rsqrt(jnp.sum(a * a, axis=-1, keepdims=True) + EPS)
            if c < GDN_HEADS:
                q_ref[:, sl] = a * nrm * (GDN_DK ** -0.5)
            else:
                k_ref[:, sl] = a * nrm
        else:
            v_ref[:, sl] = a


def _gdn_prep_kernel(x_ref, halo_ref, prev_ref, cw_ref, sm_ref, a_ref, dt_ref,
                     q_ref, k_ref, v_ref, gcb_ref, btb_ref, bg_ref, xe_ref, *, tm):
    i = pl.program_id(1)
    xe_ref[pl.ds(8, tm), :] = x_ref[0]

    @pl.when(i == 0)
    def _():
        xe_ref[pl.ds(0, 8), :] = prev_ref[0]

    @pl.when(i > 0)
    def _():
        xe_ref[pl.ds(0, 8), :] = halo_ref[0]

    def conv_fn(c):
        sl = slice(c * LANES, (c + 1) * LANES)
        acc = xe_ref[pl.ds(8 - (CONV_W - 1), tm), sl] * cw_ref[0:1, sl]
        for j in range(1, CONV_W):
            acc = acc + xe_ref[pl.ds(8 - (CONV_W - 1) + j, tm), sl] * cw_ref[j:j + 1, sl]
        return acc

    _gdn_heads_out(conv_fn, q_ref.at[0], k_ref.at[0], v_ref.at[0])

    sm = sm_ref[0]
    g = -jnp.exp(a_ref[...]) * _softplus(sm + dt_ref[...])
    row = lax.broadcasted_iota(i32, (tm, LANES), 0) % GDN_CHUNK
    s = 1
    while s < GDN_CHUNK:
        g = g + jnp.where(row >= s, pltpu.roll(g, s, axis=0), 0.0)
        s *= 2
    lane = lax.broadcasted_iota(i32, (tm, LANES), 1)
    bt = _sigmoid(sm)
    bg_ref[0] = jnp.where(lane < GDN_HEADS, g, bt)
    for h in range(GDN_HEADS):
        sl = slice(h * LANES, (h + 1) * LANES)
        gcb_ref[0, :, sl] = jnp.broadcast_to(g[:, SM_GA + h:SM_GA + h + 1], (tm, LANES))
        btb_ref[0, :, sl] = jnp.broadcast_to(bt[:, SM_GB + h:SM_GB + h + 1], (tm, LANES))


def _gdn_prep(z3, prev8, conv_w, a_row, dt_row, tm):
    b, t, _ = z3.shape
    nblk = tm // 8
    kern = functools.partial(_gdn_prep_kernel, tm=tm)
    tok = pl.BlockSpec((1, tm, HK), lambda bi, i: (bi, i, 0))
    out_sd = jax.ShapeDtypeStruct((b, t, HK), f32)
    return pl.pallas_call(
        kern,
        grid=(b, t // tm),
        in_specs=[
            pl.BlockSpec((1, tm, CONV_CH), lambda bi, i: (bi, i, 0)),
            pl.BlockSpec((1, 8, CONV_CH), lambda bi, i: (bi, jnp.maximum(i * nblk - 1, 0), 0)),
            pl.BlockSpec((1, 8, CONV_CH), lambda bi, i: (bi, 0, 0)),
            pl.BlockSpec((CONV_W, CONV_CH), lambda bi, i: (0, 0)),
            pl.BlockSpec((1, tm, LANES), lambda bi, i: (bi, i, C_SMALL // LANES)),
            pl.BlockSpec((1, LANES), lambda bi, i: (0, 0)),
            pl.BlockSpec((1, LANES), lambda bi, i: (0, 0)),
        ],
        out_specs=[tok] * 5 + [pl.BlockSpec((1, tm, LANES), lambda bi, i: (bi, i, 0))],
        out_shape=[out_sd] * 5 + [jax.ShapeDtypeStruct((b, t, LANES), f32)],
        scratch_shapes=[pltpu.VMEM((tm + 8, CONV_CH), f32)],
        compiler_params=_cparams(("parallel", "arbitrary")),
    )(z3, z3, prev8, conv_w, z3, a_row, dt_row)


def _gdn_prep_sample_kernel(x_ref, s0_ref, s1_ref, s2_ref, cw_ref, sm_ref, a_ref, dt_ref,
                            q_ref, k_ref, v_ref, bg_ref):
    def conv_fn(c):
        sl = slice(c * LANES, (c + 1) * LANES)
        return (s0_ref[:, sl] * cw_ref[0:1, sl] + s1_ref[:, sl] * cw_ref[1:2, sl]
                + s2_ref[:, sl] * cw_ref[2:3, sl] + x_ref[:, sl] * cw_ref[3:4, sl])

    _gdn_heads_out(conv_fn, q_ref, k_ref, v_ref)
    sm = sm_ref[...]
    g = -jnp.exp(a_ref[...]) * _softplus(sm + dt_ref[...])
    lane = lax.broadcasted_iota(i32, sm.shape, 1)
    bg_ref[...] = jnp.where(lane < GDN_HEADS, g, _sigmoid(sm))


def _gdn_prep_sample(zs, sc0, sc1, sc2, conv_w, a_row, dt_row):
    db = zs.shape[0]
    full = lambda shape: pl.BlockSpec(shape, lambda i: (0,) * len(shape))
    out_sd = jax.ShapeDtypeStruct((db, HK), f32)
    return pl.pallas_call(
        _gdn_prep_sample_kernel,
        grid=(1,),
        in_specs=[
            pl.BlockSpec((db, CONV_CH), lambda i: (0, 0)),
            full((db, CONV_CH)), full((db, CONV_CH)), full((db, CONV_CH)),
            full((CONV_W, CONV_CH)),
            pl.BlockSpec((db, LANES), lambda i: (0, C_SMALL // LANES)),
            full((1, LANES)), full((1, LANES)),
        ],
        out_specs=[full((db, HK)), full((db, HK)), full((db, HK)), full((db, LANES))],
        out_shape=[out_sd, out_sd, out_sd, jax.ShapeDtypeStruct((db, LANES), f32)],
        compiler_params=_cparams(("arbitrary",)),
    )(zs, sc0, sc1, sc2, conv_w, zs, a_row, dt_row)


GDN_TS = 512
GDN_HB = 8


def _bmm(a, b):
    return jnp.einsum('bij,bjk->bik', a, b, preferred_element_type=f32)


def _bmm_nt(a, b):
    return jnp.einsum('bid,bjd->bij', a, b, preferred_element_type=f32)


def _gdn_scan_kernel(q_ref, k_ref, v_ref, gz_ref, gcb_ref, btb_ref, gr_ref, s0_ref, nrm_ref,
                     o_ref, so_ref, s_ref):
    step = pl.program_id(2)
    nc = GDN_TS // GDN_CHUNK
    c = GDN_CHUNK

    @pl.when(step == 0)
    def _():
        s_ref[...] = s0_ref[0]

    ri = lax.broadcasted_iota(i32, (1, c, c), 1)
    ci = lax.broadcasted_iota(i32, (1, c, c), 2)
    tri = ri >= ci
    strict = ri > ci
    eye = (ri == ci).astype(f32)

    hb = GDN_HB

    def stack(ref):
        return jnp.concatenate(
            [ref[0, :, hh * LANES:(hh + 1) * LANES].reshape(nc, c, LANES) for hh in range(hb)], axis=0)

    q, k, v = stack(q_ref), stack(k_ref), stack(v_ref)
    gcb = stack(gcb_ref)
    bt = stack(btb_ref)
    gr = gr_ref[0].reshape(hb * nc, 1, c)
    glast = gr[:, :, c - 1:c]
    decay = jnp.exp(jnp.where(tri, gcb[:, :, 0:c] - gr, -jnp.inf))
    kb = k * bt
    m = jnp.where(strict, _bmm_nt(kb, k) * decay, 0.0)
    x = eye - m
    p = m
    for _ in range(5):
        p = _bmm(p, p)
        x = x + _bmm(x, p)
    eg = jnp.exp(gcb)
    by_chunk = lambda a: a.reshape((hb, nc) + a.shape[1:])
    u = by_chunk(_bmm(x, v * bt))
    w = by_chunk(_bmm(x, kb * eg))
    qk = by_chunk(_bmm_nt(q, k) * decay)
    qg = by_chunk(q * eg)
    kg = by_chunk(k * jnp.exp(glast - gcb))
    gl = by_chunk(jnp.exp(glast))

    gain = nrm_ref[...]
    s = s_ref[...]
    for ch in range(nc):
        v_new = u[:, ch] - _bmm(w[:, ch], s)
        o = _bmm(qg[:, ch], s) + _bmm(qk[:, ch], v_new)
        s = s * gl[:, ch] + jnp.einsum('hck,hcv->hkv', kg[:, ch], v_new, preferred_element_type=f32)
        ms = jnp.mean(o * o, axis=-1, keepdims=True)
        on = o * lax.rsqrt(ms + EPS) * gain
        rows = pl.ds(ch * c, c)
        for hh in range(hb):
            sl = slice(hh * LANES, (hh + 1) * LANES)
            o_ref[0, rows, sl] = (on[hh] * _silu(gz_ref[0, rows, sl])).astype(o_ref.dtype)
    s_ref[...] = s

    @pl.when(step == pl.num_programs(2) - 1)
    def _():
        so_ref[0] = s_ref[...]


def _gdn_scan(q, k, v, z3, gcb, btb, gc_row, s0, gdn_norm_row):
    b, t, _ = q.shape
    hw = GDN_HB * LANES
    nc = GDN_TS // GDN_CHUNK
    qspec = pl.BlockSpec((1, GDN_TS, hw), lambda bi, hg, i: (bi, i, hg))
    return pl.pallas_call(
        _gdn_scan_kernel,
        grid=(b, GDN_HEADS // GDN_HB, t // GDN_TS),
        in_specs=[
            qspec, qspec, qspec,
            pl.BlockSpec((1, GDN_TS, hw), lambda bi, hg, i: (bi, i, C_GZ // hw + hg)),
            qspec, qspec,
            pl.BlockSpec((1, GDN_HB, nc, 1, GDN_CHUNK), lambda bi, hg, i: (bi, hg, i, 0, 0)),
            pl.BlockSpec((1, GDN_HB, GDN_DK, GDN_DV), lambda bi, hg, i: (bi, hg, 0, 0)),
            pl.BlockSpec((1, LANES), lambda bi, hg, i: (0, 0)),
        ],
        out_specs=[
            pl.BlockSpec((1, GDN_TS, hw), lambda bi, hg, i: (bi, i, hg)),
            pl.BlockSpec((1, GDN_HB, GDN_DK, GDN_DV), lambda bi, hg, i: (bi, hg, 0, 0)),
        ],
        out_shape=[jax.ShapeDtypeStruct((b, t, GDN_WIDTH), bf16),
                   jax.ShapeDtypeStruct((b, GDN_HEADS, GDN_DK, GDN_DV), f32)],
        scratch_shapes=[pltpu.VMEM((GDN_HB, GDN_DK, GDN_DV), f32)],
        compiler_params=_cparams(("parallel", "parallel", "arbitrary")),
    )(q, k, v, z3, gcb, btb, gc_row, s0, gdn_norm_row)


GS_BB = 8


def _gdn_step_kernel(q_ref, k_ref, v_ref, gz_ref, bg_ref, s0_ref, nrm_ref, o_ref, so_ref):
    for bb in range(GS_BB):
        row = slice(bb, bb + 1)
        heads = [slice(h * LANES, (h + 1) * LANES) for h in range(GDN_HEADS)]
        cols = [(jnp.broadcast_to(k_ref[row, sl], (GDN_DK, LANES)).T,
                 jnp.broadcast_to(q_ref[row, sl], (GDN_DK, LANES)).T) for sl in heads]
        for h, sl in enumerate(heads):
            s0 = s0_ref[bb, h]
            q = q_ref[row, sl]
            k = k_ref[row, sl]
            v = v_ref[row, sl]
            eg = jnp.exp(bg_ref[row, SM_GA + h:SM_GA + h + 1])
            bt = bg_ref[row, SM_GB + h:SM_GB + h + 1]
            kc, qc = cols[h]
            ks = jnp.sum(kc * s0, axis=0, keepdims=True)
            qs = jnp.sum(qc * s0, axis=0, keepdims=True)
            qk = jnp.sum(q * k, axis=-1, keepdims=True)
            v_new = bt * v - (bt * eg) * ks
            o = eg * qs + qk * v_new
            so_ref[bb, h] = s0 * eg + kc * v_new
            ms = jnp.mean(o * o, axis=-1, keepdims=True)
            on = o * lax.rsqrt(ms + EPS) * nrm_ref[...]
            o_ref[row, sl] = on * _silu(gz_ref[row, sl])


def _gdn_step(q, k, v, zs, bgs, s0, gdn_norm_row):
    db = q.shape[0]
    tok = pl.BlockSpec((GS_BB, HK), lambda i: (i, 0))
    st = pl.BlockSpec((GS_BB, GDN_HEADS, GDN_DK, GDN_DV), lambda i: (i, 0, 0, 0))
    return pl.pallas_call(
        _gdn_step_kernel,
        grid=(db // GS_BB,),
        in_specs=[tok, tok, tok, pl.BlockSpec((GS_BB, GDN_WIDTH), lambda i: (i, C_GZ // GDN_WIDTH)),
                  pl.BlockSpec((GS_BB, LANES), lambda i: (i, 0)), st,
                  pl.BlockSpec((1, LANES), lambda i: (0, 0))],
        out_specs=[tok, st],
        out_shape=[jax.ShapeDtypeStruct((db, GDN_WIDTH), f32),
                   jax.ShapeDtypeStruct((db, GDN_HEADS, GDN_DK, GDN_DV), f32)],
        compiler_params=_cparams(("parallel",)),
    )(q, k, v, zs, bgs, s0, gdn_norm_row)


def _rope(x, cos, sin_signed):
    return x * cos + pltpu.roll(x, LANES // 2, axis=1) * sin_signed


def _dsa_prep_kernel(aq_ref, ak_ref, av_ref, iq_ref, ik_ref, sm_ref, cos_ref, sin_ref,
                     qn_ref, kn_ref, *out_refs, transposed, q_scale):
    if transposed:
        qb_ref, kf_ref, vf_ref, kb_ref, vt_ref, qi_ref, kif_ref, kib_ref, wt_ref = out_refs
    else:
        qb_ref, kf_ref, vf_ref, qi_ref, kif_ref, kib_ref = out_refs
    cos = cos_ref[...]
    sin = sin_ref[...]
    for h in range(ATT_HEADS):
        sl = slice(h * LANES, (h + 1) * LANES)
        a = aq_ref[0, :, sl]
        a = a * lax.rsqrt(jnp.mean(a * a, axis=-1, keepdims=True) + EPS) * qn_ref[...]
        qb_ref[0, :, sl] = (_rope(a, cos, sin) * q_scale).astype(qb_ref.dtype)
        a = ak_ref[0, :, sl]
        a = a * lax.rsqrt(jnp.mean(a * a, axis=-1, keepdims=True) + EPS) * kn_ref[...]
        kr = _rope(a, cos, sin)
        kf_ref[0, :, sl] = kr
        v = av_ref[0, :, sl]
        vf_ref[0, :, sl] = v
        if transposed:
            kb_ref[0, :, sl] = kr.astype(bf16)
            vt_ref[0, 0, sl, :] = v.T.astype(bf16)
    for h in range(IDX_HEADS):
        sl = slice(h * LANES, (h + 1) * LANES)
        qi_ref[0, :, sl] = _rope(iq_ref[0, :, sl], cos, sin).astype(bf16)
    ki = _rope(ik_ref[0], cos, sin)
    kif_ref[0] = ki
    kib_ref[0] = ki.astype(bf16)
    if transposed:
        wt_ref[0] = (sm_ref[0] * (IDX_HEADS ** -0.5 * IDX_DH ** -0.5)).T


def _dsa_prep(z3, cos, sin, qn_row, kn_row, tm, transposed, kc, q_scale):
    b, t, _ = z3.shape
    zspec = lambda w, off: pl.BlockSpec((1, tm, w), lambda bi, i: (bi, i, off // w))
    tok = lambda w, dt: (pl.BlockSpec((1, tm, w), lambda bi, i: (bi, i, 0)),
                         jax.ShapeDtypeStruct((b, t, w), dt))
    if transposed:
        per = kc // tm
        outs = [tok(ATT_WIDTH, bf16), tok(ATT_WIDTH, f32), tok(ATT_WIDTH, f32), tok(ATT_WIDTH, bf16),
                (pl.BlockSpec((1, 1, ATT_WIDTH, tm), lambda bi, i: (bi, i // per, 0, i % per)),
                 jax.ShapeDtypeStruct((b, t // kc, ATT_WIDTH, kc), bf16)),
                tok(IDX_HEADS * IDX_DH, bf16), tok(IDX_DH, f32), tok(IDX_DH, bf16),
                (pl.BlockSpec((1, LANES, tm), lambda bi, i: (bi, 0, i)),
                 jax.ShapeDtypeStruct((b, LANES, t), f32))]
    else:
        outs = [tok(ATT_WIDTH, f32), tok(ATT_WIDTH, f32), tok(ATT_WIDTH, f32),
                tok(IDX_HEADS * IDX_DH, bf16), tok(IDX_DH, f32), tok(IDX_DH, bf16)]
    kern = functools.partial(_dsa_prep_kernel, transposed=transposed, q_scale=q_scale)
    return pl.pallas_call(
        kern,
        grid=(b, t // tm),
        in_specs=[
            zspec(ATT_WIDTH, C_AQ), zspec(ATT_WIDTH, C_AK), zspec(ATT_WIDTH, C_AV),
            zspec(IDX_HEADS * IDX_DH, C_IQ), zspec(IDX_DH, C_IK), zspec(LANES, C_SMALL),
            pl.BlockSpec((tm, LANES), lambda bi, i: (i, 0)),
            pl.BlockSpec((tm, LANES), lambda bi, i: (i, 0)),
            pl.BlockSpec((1, LANES), lambda bi, i: (0, 0)),
            pl.BlockSpec((1, LANES), lambda bi, i: (0, 0)),
        ],
        out_specs=[o[0] for o in outs],
        out_shape=[o[1] for o in outs],
        compiler_params=_cparams(("parallel", "parallel")),
    )(z3, z3, z3, z3, z3, z3, cos, sin, qn_row, kn_row)


def _key_to_float(key):
    bits = jnp.where(key >= 0, key, key ^ jnp.int32(0x7FFFFFFF))
    return pltpu.bitcast(bits, f32)


def _kth_largest(count_ge, k, shape):
    kf = jnp.float32(k)
    zero_ok = count_ge(jnp.zeros(shape, f32)) >= kf
    cur = jnp.where(zero_ok, jnp.int32(0), jnp.int32(INT_MIN))

    def body(it, cur):
        cand = cur + jnp.left_shift(jnp.int32(1), jnp.int32(30) - it)
        ok = count_ge(_key_to_float(cand)) >= kf
        return jnp.where(ok, cand, cur)

    cur = lax.fori_loop(0, 31, body, cur)
    return _key_to_float(cur)


def _tie_bound(count_eq_le, need, nbits, shape):
    lo = jnp.full(shape, -1, i32)

    def body(it, lo):
        cand = lo + jnp.left_shift(jnp.int32(1), jnp.int32(nbits - 1) - it)
        short = count_eq_le(cand) < need
        return jnp.where(short, cand, lo)

    lo = lax.fori_loop(0, nbits, body, lo)
    return lo + 1


ATT_KC = 512
ATT_QB = 256
ATT_KS = 128
CNT_ROWS = 64
LOG2E = 1.4426950408889634


def _dsa_attend_kernel(qi_ref, wt_ref, ki_ref, qb_ref, kb_ref, vt_ref, az_ref, o_ref,
                       sc_ref, sc16_ref, *head_refs, t_total):
    i = pl.program_id(1)
    qb = ATT_QB
    kc = ATT_KC
    nch = (i * qb + qb + kc - 1) // kc
    tq = i * qb + lax.broadcasted_iota(i32, (1, qb), 1)
    ktop = min(TOPK_MAX, t_total // 4)

    def spos(c):
        return c * kc + lax.broadcasted_iota(i32, (kc, 1), 0)

    def idx_body(c, carry):
        for sub in range(kc // ATT_KS):
            base = c * kc + sub * ATT_KS
            ksub = ki_ref[0, pl.ds(base, ATT_KS), :]
            acc = jnp.zeros((ATT_KS, qb), f32)
            for h in range(IDX_HEADS):
                d = _dot_nt(ksub, qi_ref[0, :, h * LANES:(h + 1) * LANES])
                acc = acc + jnp.maximum(d, 0.0) * wt_ref[0, SM_IW + h:SM_IW + h + 1, :]
            sp = base + lax.broadcasted_iota(i32, (ATT_KS, 1), 0)
            masked = jnp.where(sp <= tq, acc, -jnp.inf)
            sc_ref[pl.ds(base, ATT_KS), :] = masked
            sc16_ref[pl.ds(base, ATT_KS), :] = masked.astype(bf16)
        return carry

    lax.fori_loop(0, nch, idx_body, 0)

    def col_count(pred_fn):
        def body(c, acc):
            m = pred_fn(sc_ref[pl.ds(c * kc, kc), :], c)
            return acc + jnp.sum(jnp.where(m, 1.0, 0.0).reshape(kc // CNT_ROWS, CNT_ROWS, qb), axis=0)
        acc = lax.fori_loop(0, nch, body, jnp.zeros((CNT_ROWS, qb), f32))
        return jnp.sum(acc, axis=0, keepdims=True)

    def col_count16(cand):
        cand16 = cand.astype(bf16)

        def body(c, acc):
            one = jnp.where(sc16_ref[pl.ds(c * kc, kc), :] >= cand16, jnp.bfloat16(1.0), jnp.bfloat16(0.0))
            for r in range(kc // CNT_ROWS):
                acc = acc + one[r * CNT_ROWS:(r + 1) * CNT_ROWS]
            return acc
        acc = lax.fori_loop(0, nch, body, jnp.zeros((CNT_ROWS, qb), bf16))
        return jnp.sum(acc.astype(f32), axis=0, keepdims=True)

    def search():
        kf = jnp.float32(ktop)
        zero_ok = col_count16(jnp.zeros((1, qb), f32)) >= kf
        cur = jnp.where(zero_ok, jnp.int32(0), jnp.int32(INT_MIN))

        def body16(it, cur):
            cand = cur + jnp.left_shift(jnp.int32(1), jnp.int32(30) - it)
            return jnp.where(col_count16(_key_to_float(cand)) >= kf, cand, cur)

        cur = lax.fori_loop(0, 15, body16, cur)
        step = 1 << 16
        lo = jnp.maximum(jnp.maximum(cur, jnp.int32(INT_MIN + step)) - step, jnp.int32(KEY_MOST_NEGATIVE))

        def body32(it, carry):
            off, cnt = carry
            cand = off + jnp.left_shift(jnp.int32(1), jnp.int32(16) - it)
            cand_f = _key_to_float(lo + cand)
            c_new = col_count(lambda s, c: s >= cand_f)
            ok = c_new >= kf
            return jnp.where(ok, cand, off), jnp.where(ok, c_new, cnt)

        off, c_ge = lax.fori_loop(0, 17, body32, (jnp.zeros((1, qb), i32), jnp.full((1, qb), kf + 1.0, f32)))
        thr = _key_to_float(lo + off)

        def tie_bound():
            need = kf - col_count(lambda s, c: s > thr)
            return _tie_bound(
                lambda cand: col_count(lambda s, c: (s == thr) & (spos(c) <= cand)),
                need, max(1, (t_total - 1).bit_length()), (1, qb))

        jb = lax.cond(jnp.max(c_ge) > kf, tie_bound, lambda: jnp.full((1, qb), t_total, i32))
        return thr, jb

    def take_all():
        return jnp.full((1, qb), -jnp.inf, f32), jnp.full((1, qb), t_total, i32)

    thr, jb = lax.cond((i + 1) * qb <= ktop, take_all, search)

    def bias_body(c, carry):
        s = sc_ref[pl.ds(c * kc, kc), :]
        sp = spos(c)
        sel = ((s > thr) | ((s == thr) & (sp <= jb))) & (sp <= tq)
        sc_ref[pl.ds(c * kc, kc), :] = jnp.where(sel, 0.0, NEG_BIG)
        return carry

    lax.fori_loop(0, nch, bias_body, 0)

    nh = ATT_HEADS
    m_refs, l_refs, al_refs, acc_refs, s_refs, p_refs = (head_refs[j * nh:(j + 1) * nh] for j in range(6))
    for h in range(nh):
        m_refs[h][...] = jnp.full(m_refs[h].shape, NEG_BIG, f32)
        l_refs[h][...] = jnp.zeros(l_refs[h].shape, f32)
        acc_refs[h][...] = jnp.zeros(acc_refs[h].shape, f32)

    def att_body(c, carry):
        for h in range(nh):
            sl = slice(h * LANES, (h + 1) * LANES)
            s_refs[h][...] = _dot_nt(kb_ref[0, pl.ds(c * kc, kc), sl], qb_ref[0, :, sl])
        for h in range(nh):
            s = s_refs[h][...] + sc_ref[pl.ds(c * kc, kc), :]
            m = m_refs[h][...]
            m_new = jnp.maximum(m, jnp.max(s, axis=0, keepdims=True))
            alpha = jnp.exp2(m - m_new)
            p = jnp.exp2(s - m_new)
            m_refs[h][...] = m_new
            al_refs[h][...] = alpha
            l_refs[h][...] = alpha * l_refs[h][...] + jnp.sum(p, axis=0, keepdims=True)
            p_refs[h][...] = p.astype(bf16)
        for h in range(nh):
            sl = slice(h * LANES, (h + 1) * LANES)
            acc_refs[h][...] = (al_refs[h][...] * acc_refs[h][...]
                                + _dot(vt_ref[0, c, sl, :], p_refs[h][...]))
        return carry

    lax.fori_loop(0, nch, att_body, 0)

    for h in range(ATT_HEADS):
        sl = slice(h * LANES, (h + 1) * LANES)
        out = (acc_refs[h][...] / l_refs[h][...]).T
        o_ref[0, :, sl] = (out * _silu(az_ref[0, :, sl])).astype(o_ref.dtype)


def _dsa_attend(qi, wt, ki, qb, kb, vt, z3):
    b, t, _ = qb.shape
    kern = functools.partial(_dsa_attend_kernel, t_total=t)
    return pl.pallas_call(
        kern,
        grid=(b, t // ATT_QB),
        in_specs=[
            pl.BlockSpec((1, ATT_QB, IDX_HEADS * IDX_DH), lambda bi, i: (bi, i, 0)),
            pl.BlockSpec((1, LANES, ATT_QB), lambda bi, i: (bi, 0, i)),
            pl.BlockSpec((1, t, IDX_DH), lambda bi, i: (bi, 0, 0)),
            pl.BlockSpec((1, ATT_QB, ATT_WIDTH), lambda bi, i: (bi, i, 0)),
            pl.BlockSpec((1, t, ATT_WIDTH), lambda bi, i: (bi, 0, 0)),
            pl.BlockSpec((1, t // ATT_KC, ATT_WIDTH, ATT_KC), lambda bi, i: (bi, 0, 0, 0)),
            pl.BlockSpec((1, ATT_QB, ATT_WIDTH), lambda bi, i: (bi, i, C_AZ // ATT_WIDTH)),
        ],
        out_specs=pl.BlockSpec((1, ATT_QB, ATT_WIDTH), lambda bi, i: (bi, i, 0)),
        out_shape=jax.ShapeDtypeStruct((b, t, ATT_WIDTH), bf16),
        scratch_shapes=([pltpu.VMEM((t, ATT_QB), f32), pltpu.VMEM((t, ATT_QB), bf16)]
                        + [pltpu.VMEM((1, ATT_QB), f32) for _ in range(3 * ATT_HEADS)]
                        + [pltpu.VMEM((LANES, ATT_QB), f32) for _ in range(ATT_HEADS)]
                        + [pltpu.VMEM((ATT_KC, ATT_QB), f32) for _ in range(ATT_HEADS)]
                        + [pltpu.VMEM((ATT_KC, ATT_QB), bf16) for _ in range(ATT_HEADS)]),
        compiler_params=_cparams(("parallel", "arbitrary")),
    )(qi, wt, ki, qb, kb, vt, z3)


SC_PG = 16
DMA_UNROLL = 8


def _sample_scores_kernel(pt_s, qi_ref, w_ref, cik_ref, o_ref, buf, sem, *, n_pages):
    b = pl.program_id(0)
    nb = pl.num_programs(0)

    def page_copy(bb, slot, p):
        return pltpu.make_async_copy(cik_ref.at[pt_s[bb * n_pages + p]], buf.at[slot, p], sem.at[slot])

    def issue(bb, slot):
        def body(p, carry):
            page_copy(bb, slot, p).start()
            return carry
        lax.fori_loop(0, n_pages, body, 0, unroll=DMA_UNROLL)

    def wait_all(bb, slot):
        def body(p, carry):
            page_copy(bb, slot, p).wait()
            return carry
        lax.fori_loop(0, n_pages, body, 0, unroll=DMA_UNROLL)

    slot = b % 2

    @pl.when(b == 0)
    def _():
        issue(b, slot)

    @pl.when(b + 1 < nb)
    def _():
        issue(b + 1, 1 - slot)

    wait_all(b, slot)

    qi = qi_ref[0]
    w = w_ref[0]
    gk = SC_PG * PAGE_SIZE
    for g in range(n_pages // SC_PG):
        keys = buf[slot, g * SC_PG:(g + 1) * SC_PG].reshape(gk, IDX_DH).astype(bf16)
        d = _dot_nt(qi, keys)
        o_ref[0, :, g * gk:(g + 1) * gk] = jnp.sum(jnp.maximum(d, 0.0) * w, axis=0, keepdims=True)


def _sample_scores(pt_flat, qi3, w3, cache_ik):
    db = qi3.shape[0]
    n_pages = pt_flat.shape[0] // db
    kern = functools.partial(_sample_scores_kernel, n_pages=n_pages)
    grid_spec = pltpu.PrefetchScalarGridSpec(
        num_scalar_prefetch=1,
        grid=(db,),
        in_specs=[pl.BlockSpec((1, IDX_HEADS, IDX_DH), lambda bi, pt: (bi, 0, 0)),
                  pl.BlockSpec((1, IDX_HEADS, 1), lambda bi, pt: (bi, 0, 0)),
                  pl.BlockSpec(memory_space=pl.ANY)],
        out_specs=pl.BlockSpec((1, 1, n_pages * PAGE_SIZE), lambda bi, pt: (bi, 0, 0)),
        scratch_shapes=[pltpu.VMEM((2, n_pages, PAGE_SIZE, IDX_DH), f32), pltpu.SemaphoreType.DMA((2,))],
    )
    return pl.pallas_call(
        kern,
        grid_spec=grid_spec,
        out_shape=jax.ShapeDtypeStruct((db, 1, n_pages * PAGE_SIZE), f32),
        compiler_params=_cparams(("arbitrary",)),
    )(pt_flat, qi3, w3, cache_ik)


def _sample_select_kernel(sc_ref, qi_ref, ki_ref, w_ref, pt_ref, idx_ref, rows_ref, *, n_past, ktop):
    db, n_pages, ps = sc_ref.shape
    sc = sc_ref[...]

    def red(x):
        return jnp.sum(jnp.sum(x, axis=1, keepdims=True), axis=2, keepdims=True)

    dn = jnp.sum(qi_ref[...].astype(f32) * ki_ref[...].astype(f32), axis=2, keepdims=True)
    s_new = jnp.sum(jnp.maximum(dn, 0.0) * w_ref[...], axis=1, keepdims=True)

    def count(pred):
        return red(pred(sc).astype(f32)) + pred(s_new).astype(f32)

    shape = (db, 1, 1)
    thr = _kth_largest(lambda cand: count(lambda s: s >= cand), ktop, shape)
    c_ge = count(lambda s: s >= thr)
    c_gt = count(lambda s: s > thr)
    need = jnp.float32(ktop) - c_gt
    pos = (lax.broadcasted_iota(i32, (1, n_pages, ps), 1) * ps
           + lax.broadcasted_iota(i32, (1, n_pages, ps), 2))

    def eq_le(cand):
        return (red(((sc == thr) & (pos <= cand)).astype(f32))
                + ((s_new == thr) & (n_past <= cand)).astype(f32))

    jb = lax.cond(jnp.max(c_ge) > jnp.float32(ktop),
                  lambda: _tie_bound(eq_le, need, (n_past + 1).bit_length(), shape),
                  lambda: jnp.full(shape, n_past + 1, i32))
    sel = (sc > thr) | ((sc == thr) & (pos <= jb))

    r_i = lax.broadcasted_iota(i32, (ps, ps), 0)
    c_i = lax.broadcasted_iota(i32, (ps, ps), 1)
    upper = (r_i <= c_i).astype(bf16)
    upper_pg = (lax.broadcasted_iota(i32, (n_pages, n_pages), 0)
                <= lax.broadcasted_iota(i32, (n_pages, n_pages), 1)).astype(bf16)
    jcol = lax.broadcasted_iota(i32, (ktop, 1), 0).astype(f32)
    plane = lax.broadcasted_iota(i32, (1, n_pages), 1).astype(f32)
    ones8 = jnp.ones((8, ps), bf16)

    selb = jnp.where(sel, 1.0, 0.0).astype(bf16)
    for b in range(db):
        sb = selb[b]
        incl = _dot(sb, upper)
        tot_row = _dot_nt(ones8, sb)[0:1]
        cum_row = _dot(jnp.broadcast_to(tot_row, (8, n_pages)).astype(bf16), upper_pg)[0:1]
        page_of = jnp.sum((cum_row <= jcol).astype(f32), axis=1, keepdims=True)
        onehot = (page_of == plane)
        before = jnp.sum(jnp.where(onehot, cum_row - tot_row, 0.0), axis=1, keepdims=True)
        rloc = jcol - before
        incl_rows = _dot(onehot.astype(bf16), incl.astype(bf16))
        off_of = jnp.sum((incl_rows <= rloc).astype(f32), axis=1, keepdims=True)
        idx = jnp.minimum(page_of * ps + off_of, jnp.float32(n_past))
        idx_ref[b] = idx.astype(i32)
        pidx = jnp.minimum(idx, jnp.float32(n_past - 1))
        pg = jnp.floor(pidx * (1.0 / ps))
        phys = jnp.sum(jnp.where(pg == plane, pt_ref[b], 0.0), axis=1, keepdims=True)
        rows_ref[b] = (phys * ps + (pidx - pg * ps)).astype(i32)


def _sample_select(sc3, qi3, ki3, w3, pt3, n_past, ktop):
    db = sc3.shape[0]
    kern = functools.partial(_sample_select_kernel, n_past=n_past, ktop=ktop)
    full = lambda a: pl.BlockSpec(a.shape, lambda i: (0,) * a.ndim)
    out_spec = pl.BlockSpec((db, ktop, 1), lambda i: (0, 0, 0))
    out_sd = jax.ShapeDtypeStruct((db, ktop, 1), i32)
    return pl.pallas_call(
        kern,
        grid=(1,),
        in_specs=[full(sc3), full(qi3), full(ki3), full(w3), full(pt3)],
        out_specs=[out_spec, out_spec],
        out_shape=[out_sd, out_sd],
        compiler_params=_cparams(("arbitrary",)),
    )(sc3, qi3, ki3, w3, pt3)


def _sample_attend_kernel(rows_s, idxv_ref, q_ref, kn_ref, vn_ref, az_ref, ck_ref, cv_ref,
                          o_ref, kbuf, vbuf, sem, *, n_past, ktop):
    b = pl.program_id(0)
    nb = pl.num_programs(0)

    def row_copies(bb, slot, j):
        row = rows_s[bb * ktop + j]
        ck = pltpu.make_async_copy(ck_ref.at[row], kbuf.at[slot, :, j, :], sem.at[0, slot])
        cv = pltpu.make_async_copy(cv_ref.at[row], vbuf.at[slot, :, j, :], sem.at[1, slot])
        return ck, cv

    def issue(bb, slot):
        def body(j, carry):
            ck, cv = row_copies(bb, slot, j)
            ck.start()
            cv.start()
            return carry
        lax.fori_loop(0, ktop, body, 0, unroll=DMA_UNROLL)

    def wait_all(bb, slot):
        def body(j, carry):
            ck, cv = row_copies(bb, slot, j)
            ck.wait()
            cv.wait()
            return carry
        lax.fori_loop(0, ktop, body, 0, unroll=DMA_UNROLL)

    slot = b % 2

    @pl.when(b == 0)
    def _():
        issue(b, slot)

    @pl.when(b + 1 < nb)
    def _():
        issue(b + 1, 1 - slot)

    wait_all(b, slot)

    is_new = idxv_ref[0] >= n_past
    newf = is_new.astype(f32)
    heads = [slice(h * LANES, (h + 1) * LANES) for h in range(ATT_HEADS)]
    scores = []
    for h, sl in enumerate(heads):
        q8 = jnp.broadcast_to(q_ref[0, :, sl], (8, LANES)).astype(bf16)
        scores.append(_dot_nt(q8, kbuf[slot, h].astype(bf16))[0:1])
    probs = []
    for h, sl in enumerate(heads):
        qh = q_ref[0, :, sl]
        s_new = jnp.sum(qh.astype(bf16).astype(f32) * kn_ref[0, :, sl].astype(bf16).astype(f32),
                        axis=-1, keepdims=True)
        s = jnp.where(is_new, s_new, scores[h])
        p = jnp.exp(s - jnp.max(s, axis=-1, keepdims=True))
        probs.append((p, jnp.sum(p, axis=-1, keepdims=True)))
    for h, sl in enumerate(heads):
        p, l = probs[h]
        p_old = jnp.broadcast_to(p * (1.0 - newf), (8, ktop)).astype(bf16)
        pv = _dot(p_old, vbuf[slot, h].astype(bf16))[0:1]
        pv = pv + jnp.sum(p * newf, axis=-1, keepdims=True) * vn_ref[0, :, sl]
        o_ref[0, :, sl] = ((pv / l) * _silu(az_ref[0, :, sl])).astype(o_ref.dtype)


def _sample_attend(rows_flat, idx_row, q3, kn3, vn3, az3, ck3, cv3, n_past, ktop):
    db = q3.shape[0]
    kern = functools.partial(_sample_attend_kernel, n_past=n_past, ktop=ktop)
    tok = pl.BlockSpec((1, 1, ATT_WIDTH), lambda bi, r: (bi, 0, 0))
    grid_spec = pltpu.PrefetchScalarGridSpec(
        num_scalar_prefetch=1,
        grid=(db,),
        in_specs=[pl.BlockSpec((1, 1, ktop), lambda bi, r: (bi, 0, 0)), tok, tok, tok, tok,
                  pl.BlockSpec(memory_space=pl.ANY), pl.BlockSpec(memory_space=pl.ANY)],
        out_specs=tok,
        scratch_shapes=[pltpu.VMEM((2, ATT_HEADS, ktop, ATT_DH), f32),
                        pltpu.VMEM((2, ATT_HEADS, ktop, ATT_DH), f32),
                        pltpu.SemaphoreType.DMA((2, 2))],
    )
    return pl.pallas_call(
        kern,
        grid_spec=grid_spec,
        out_shape=jax.ShapeDtypeStruct((db, 1, ATT_WIDTH), bf16),
        compiler_params=_cparams(("arbitrary",)),
    )(rows_flat, idx_row, q3, kn3, vn3, az3, ck3, cv3)


def _tail_kernel(oa_ref, ob_ref, mga_ref, mgb_ref, x_ref, p_ref, wa_ref, wb_ref, wo_ref,
                 pn_ref, wg_ref, wp_ref, y_ref):
    a = _dot(oa_ref[...].astype(bf16), wa_ref[...])
    b = _dot(ob_ref[...].astype(bf16), wb_ref[...])
    merged = _sigmoid(mga_ref[...]) * a + _sigmoid(mgb_ref[...]) * b
    x2 = x_ref[...] + _dot(merged.astype(bf16), wo_ref[...])
    ms = jnp.mean(x2 * x2, axis=-1, keepdims=True)
    hn = (x2 * lax.rsqrt(ms + EPS) * pn_ref[...]).astype(bf16)
    gate = _sigmoid(_dot(hn, wg_ref[...]))
    y_ref[...] = x2 + gate * _dot(p_ref[...].astype(bf16), wp_ref[...])


def _tail(oa, ob, z2d, x2d, p2d, wa, wb, wo, pn_row, wg, wp, tm):
    m = x2d.shape[0]
    const = lambda a: pl.BlockSpec(a.shape, lambda i: (0, 0), pipeline_mode=pl.Buffered(1))
    return pl.pallas_call(
        _tail_kernel,
        grid=(m // tm,),
        in_specs=[
            pl.BlockSpec((tm, GDN_WIDTH), lambda i: (i, 0)),
            pl.BlockSpec((tm, ATT_WIDTH), lambda i: (i, 0)),
            pl.BlockSpec((tm, D_MODEL), lambda i: (i, C_MGA // D_MODEL)),
            pl.BlockSpec((tm, D_MODEL), lambda i: (i, C_MGB // D_MODEL)),
            pl.BlockSpec((tm, D_MODEL), lambda i: (i, 0)),
            pl.BlockSpec((tm, PLE_DIM), lambda i: (i, 0)),
            const(wa), const(wb), const(wo), const(pn_row), const(wg), const(wp),
        ],
        out_specs=pl.BlockSpec((tm, D_MODEL), lambda i: (i, 0)),
        out_shape=jax.ShapeDtypeStruct((m, D_MODEL), f32),
        compiler_params=_cparams(("parallel",)),
    )(oa, ob, z2d, z2d, x2d, p2d, wa, wb, wo, pn_row, wg, wp)


def _lane_row(v, offset=0):
    row = jnp.zeros((1, LANES), f32)
    return row.at[0, offset:offset + v.shape[0]].set(v.astype(f32))


def _rope_tables(pos):
    half = ATT_DH // 2
    inv = ROPE_THETA ** (-jnp.arange(half, dtype=f32) * 2.0 / ATT_DH)
    ang = pos.astype(f32)[:, None] * inv[None, :]
    cos, sin = jnp.cos(ang), jnp.sin(ang)
    return jnp.concatenate([cos, cos], axis=1), jnp.concatenate([-sin, sin], axis=1)


def kernel(x_prompt, x_sample, p_prompt, p_sample, cache_k, cache_v, cache_idx_k, state_gdn, state_conv,
           page_table, norm_in, w_in, conv_w, a_log, dt_bias, gdn_norm, q_norm, k_norm,
           w_proj_a, w_proj_b, w_out, ple_norm, w_ple_gate, w_ple_proj):
    depth = norm_in.shape[0]
    assert depth == 1, "single-layer trunk"
    B, T, _ = x_prompt.shape
    DB, S_new, _ = x_sample.shape
    assert S_new == 1
    n_pages = page_table.shape[1]
    n_past = n_pages * PAGE_SIZE
    n_pool = cache_k.shape[1]
    li = 0

    assert w_in.shape[2] == D_IN
    w_perm = _wprep(jnp.swapaxes(w_in[li], 0, 1))
    gain_in = norm_in[li].reshape(1, D_MODEL)
    a_row = _lane_row(a_log[li], SM_GA)
    dt_row = _lane_row(dt_bias[li], SM_GA)
    gdn_norm_row = gdn_norm[li].reshape(1, LANES)
    qn_row = q_norm[li].reshape(1, LANES)
    kn_row = k_norm[li].reshape(1, LANES)
    wa = w_proj_a[li].astype(bf16)
    wb = w_proj_b[li].astype(bf16)
    wo = w_out[li].astype(bf16)
    wg = w_ple_gate[li].astype(bf16)
    wp = w_ple_proj[li].astype(bf16)
    pn_row = ple_norm[li].reshape(1, D_MODEL)
    cw = conv_w[li]

    xp2 = x_prompt.reshape(B * T, D_MODEL)
    zp = _inproj(xp2, gain_in, w_perm, tm=1024, tn=768)
    zp3 = zp.reshape(B, T, NZ)

    prev8 = jnp.zeros((B, 8, CONV_CH), f32)
    qa, ka, va, gcb, btb, bg = _gdn_prep(zp3, prev8, cw, a_row, dt_row, tm=256)
    gc_row = jnp.swapaxes(bg[..., SM_GA:SM_GA + GDN_HEADS], 1, 2).reshape(
        B, GDN_HEADS, T // GDN_CHUNK, 1, GDN_CHUNK)
    s00 = jnp.zeros((B, GDN_HEADS, GDN_DK, GDN_DV), f32)
    oa_p, s_p = _gdn_scan(qa, ka, va, zp3, gcb, btb, gc_row, s00, gdn_norm_row)
    conv_p = zp3[:, T - (CONV_W - 1):, C_GQ:C_GQ + CONV_CH]

    cos_p, sin_p = _rope_tables(jnp.arange(T))
    qb_p, kf_p, v_p, kb_p, vt_p, qi_p, kif_p, kib_p, wt_p = _dsa_prep(
        zp3, cos_p, sin_p, qn_row, kn_row, tm=256, transposed=True, kc=ATT_KC,
        q_scale=LOG2E * ATT_DH ** -0.5)
    ob_p = _dsa_attend(qi_p, wt_p, kib_p, qb_p, kb_p, vt_p, zp3)

    y_p = _tail(oa_p.reshape(B * T, GDN_WIDTH), ob_p.reshape(B * T, ATT_WIDTH), zp, xp2,
                p_prompt[li].reshape(B * T, PLE_DIM), wa, wb, wo, pn_row, wg, wp, tm=256)

    xs2 = x_sample.reshape(DB, D_MODEL)
    zs = _inproj(xs2, gain_in, w_perm, tm=DB, tn=NZ // 6)
    sconv = state_conv[li]
    qs, ks, vs, bgs = _gdn_prep_sample(zs, sconv[:, 0], sconv[:, 1], sconv[:, 2], cw, a_row, dt_row)
    oa_s, s_s = _gdn_step(qs, ks, vs, zs, bgs, state_gdn[li], gdn_norm_row)
    conv_s = jnp.concatenate([sconv[:, 1:], zs[:, None, C_GQ:C_GQ + CONV_CH]], axis=1)

    cos_s, sin_s = _rope_tables(jnp.full((DB,), n_past))
    qf_s, kf_s, v_s, qi_s, kif_s, kib_s = _dsa_prep(
        zs.reshape(1, DB, NZ), cos_s, sin_s, qn_row, kn_row, tm=DB, transposed=False, kc=ATT_KC,
        q_scale=ATT_DH ** -0.5)
    wi_s = zs[:, C_SMALL + SM_IW:C_SMALL + SM_IW + IDX_HEADS] * (IDX_HEADS ** -0.5 * IDX_DH ** -0.5)
    qi3 = qi_s.reshape(DB, IDX_HEADS, IDX_DH)
    w3 = wi_s.reshape(DB, IDX_HEADS, 1)
    pt_flat = page_table.reshape(-1).astype(i32)
    sc = _sample_scores(pt_flat, qi3, w3, cache_idx_k[li])
    ktop = min(TOPK_MAX, (n_past + S_new) // 4)
    idx, rows = _sample_select(sc.reshape(DB, n_pages, PAGE_SIZE), qi3, kib_s.reshape(DB, 1, IDX_DH), w3,
                               page_table.astype(f32).reshape(DB, 1, n_pages), n_past, ktop)
    ob_s = _sample_attend(rows.reshape(-1), idx.reshape(DB, 1, ktop),
                          qf_s.reshape(DB, 1, ATT_WIDTH), kf_s.reshape(DB, 1, ATT_WIDTH),
                          v_s.reshape(DB, 1, ATT_WIDTH), zs[:, C_AZ:C_AZ + ATT_WIDTH].reshape(DB, 1, ATT_WIDTH),
                          cache_k[li].reshape(n_pool * PAGE_SIZE, ATT_HEADS, ATT_DH),
                          cache_v[li].reshape(n_pool * PAGE_SIZE, ATT_HEADS, ATT_DH), n_past, ktop)
    y_s = _tail(oa_s.reshape(DB, GDN_WIDTH), ob_s.reshape(DB, ATT_WIDTH), zs, xs2,
                p_sample[li].reshape(DB, PLE_DIM), wa, wb, wo, pn_row, wg, wp, tm=DB)

    shp = (B, T, ATT_HEADS, ATT_DH)
    shs = (DB, S_new, ATT_HEADS, ATT_DH)
    return (y_p.reshape(B, T, D_MODEL), y_s.reshape(DB, S_new, D_MODEL),
            kf_p.reshape(shp)[None], v_p.reshape(shp)[None], kif_p[None],
            s_p[None], conv_p[None],
            kf_s.reshape(shs)[None], v_s.reshape(shs)[None], kif_s.reshape(DB, S_new, IDX_DH)[None],
            s_s[None], conv_s[None])
```
